```python
import math
import jax, jax.numpy as jnp
from jax import lax
import numpy as np

D_MODEL = 2048
BATCH = 4
SEQ = 4096
DEPTH = 1
DEC_BATCH = 32
DEC_SEQ = 1
PAST_LEN = 16384
PAGE_SIZE = 128

N_META = 16
D_MIX = D_MODEL
D_ATTN = D_MIX // 2
D_CONV = D_MIX - D_ATTN
N_HEADS = 8
DV = D_ATTN // N_HEADS
DK = DV // 2
CONV_W = 31
N_BUCKETS = 32
MAX_DISTANCE = 128
D_FF = 128 * ((8 * D_MODEL // 3 + 127) // 128)
FFN_CONV_W = 3
Q_BLOCK = 128
EPS = 1e-6
D_IN = 3 * D_ATTN + 2 * D_CONV

kernel_name = "hybrid_diffattn_conformerconv_convffn_step"


def lambda_init(layer):
    return 0.8 - 0.6 * math.exp(-0.3 * layer)


def rmsnorm(x, g):
    xf = x.astype(jnp.float32)
    y = xf * lax.rsqrt(jnp.mean(xf * xf, axis=-1, keepdims=True) + EPS)
    return (y * g.astype(jnp.float32)).astype(x.dtype)


def layernorm(x, g, b):
    xf = x.astype(jnp.float32)
    mu = jnp.mean(xf, axis=-1, keepdims=True)
    var = jnp.mean(jnp.square(xf - mu), axis=-1, keepdims=True)
    y = (xf - mu) * lax.rsqrt(var + EPS)
    return (y * g.astype(jnp.float32) + b.astype(jnp.float32)).astype(x.dtype)


def t5_bucket(rel):
    n = jnp.maximum(rel, 0)
    max_exact = N_BUCKETS // 2
    nf = jnp.maximum(n, 1).astype(jnp.float32)
    large = max_exact + (jnp.log(nf / max_exact) / math.log(MAX_DISTANCE / max_exact)
                         * (N_BUCKETS - max_exact)).astype(jnp.int32)
    large = jnp.minimum(large, N_BUCKETS - 1)
    return jnp.where(n < max_exact, n, large)


def causal_dwconv(u_ext, w, b):
    C = w.shape[-1]
    y = lax.conv_general_dilated(u_ext, w[:, None, :].astype(u_ext.dtype), window_strides=(1,),
                                 padding='VALID', dimension_numbers=('NWC', 'WIO', 'NWC'),
                                 feature_group_count=C)
    return y + b.astype(y.dtype)


def project(xn, w_in):
    B, T, _ = xn.shape
    z = xn @ w_in
    q = z[..., :D_ATTN].reshape(B, T, N_HEADS, 2, DK)
    k = z[..., D_ATTN:2 * D_ATTN].reshape(B, T, N_HEADS, 2, DK)
    v = z[..., 2 * D_ATTN:3 * D_ATTN].reshape(B, T, N_HEADS, DV)
    a, g = jnp.split(z[..., 3 * D_ATTN:], 2, axis=-1)
    u = a * jax.nn.sigmoid(g)
    return q, k, v, u


def diff_lambda(lq1, lk1, lq2, lk2, li):
    f = lambda t: t.astype(jnp.float32)
    return jnp.exp(jnp.sum(f(lq1) * f(lk1))) - jnp.exp(jnp.sum(f(lq2) * f(lk2))) + li


def diff_attend(q, k, v, q_pos, k_pos, rel_bias, lam):
    s = jnp.einsum('bqhcd,bkhcd->bhcqk', q, k, preferred_element_type=jnp.float32) * (DK ** -0.5)
    rel = q_pos[:, None] - k_pos[None, :]
    bias = jnp.transpose(rel_bias[t5_bucket(rel)], (2, 0, 1)).astype(jnp.float32)
    s = s + bias[None, :, None]
    s = jnp.where((rel >= 0)[None, None, None], s, -jnp.inf)
    p = jax.nn.softmax(s, axis=-1)
    a = p[:, :, 0] - lam * p[:, :, 1]
    return jnp.einsum('bhqk,bkhd->bqhd', a.astype(v.dtype), v)


def conv_module(u_ext, conv_w, conv_b, ln_g, ln_b):
    y = causal_dwconv(u_ext, conv_w, conv_b)
    return jax.nn.silu(layernorm(y, ln_g, ln_b))


def merge_out(o, c, subln_g, li, w_o):
    B, T = o.shape[:2]
    o = rmsnorm(o, subln_g) * (1.0 - li)
    return jnp.concatenate([o.reshape(B, T, D_ATTN), c], axis=-1) @ w_o


def conv_ffn_tail(up_ext, cw, cb, w_down):
    c = causal_dwconv(up_ext, cw, cb)
    a, g = jnp.split(c, 2, axis=-1)
    return (jax.nn.gelu(g, approximate=False) * a) @ w_down


def setup_inputs(seed: int = 0) -> dict:
    key = jax.random.key(seed)
    ks = jax.random.split(key, 32)
    f32 = jnp.float32
    n_pages = PAST_LEN // PAGE_SIZE
    n_used = DEC_BATCH * n_pages
    n_pool = (5 * n_used + 3) // 4
    nrm = lambda k, shape, s: jax.random.normal(k, shape, f32) * s
    perm = jax.random.permutation(ks[0], n_pool)[:n_used]
    return {
        "x_prompt": nrm(ks[1], (BATCH, SEQ, D_MODEL), 1.0),
        "x_sample": nrm(ks[2], (DEC_BATCH, DEC_SEQ, D_MODEL), 1.0),
        "cache_k": nrm(ks[3], (DEPTH, n_pool, PAGE_SIZE, N_HEADS, 2 * DK), 1.0),
        "cache_v": nrm(ks[4], (DEPTH, n_pool, PAGE_SIZE, N_HEADS, DV), 1.0),
        "page_table": perm.reshape(DEC_BATCH, n_pages).astype(jnp.int32),
        "state_conv": nrm(ks[5], (DEPTH, DEC_BATCH, CONV_W - 1, D_CONV), 0.5),
        "state_ffn_conv": nrm(ks[6], (DEPTH, DEC_BATCH, FFN_CONV_W - 1, 2 * D_FF), 1.0),
        "meta_tokens": nrm(ks[7], (N_META, D_MODEL), 1.0),
        "rel_bias": nrm(ks[8], (N_BUCKETS, N_HEADS), 0.1),
        "norm1_g": 1.0 + nrm(ks[9], (DEPTH, D_MODEL), 0.02),
        "w_in": nrm(ks[10], (DEPTH, D_MODEL, D_IN), D_MODEL ** -0.5),
        "lambda_q1": nrm(ks[11], (DEPTH, DK), 0.1),
        "lambda_k1": nrm(ks[12], (DEPTH, DK), 0.1),
        "lambda_q2": nrm(ks[13], (DEPTH, DK), 0.1),
        "lambda_k2": nrm(ks[14], (DEPTH, DK), 0.1),
        "subln_g": 1.0 + nrm(ks[15], (DEPTH, DV), 0.02),
        "conv_w": nrm(ks[16], (DEPTH, CONV_W, D_CONV), CONV_W ** -0.5),
        "conv_b": nrm(ks[17], (DEPTH, D_CONV), 0.02),
        "conv_ln_g": 1.0 + nrm(ks[18], (DEPTH, D_CONV), 0.02),
        "conv_ln_b": nrm(ks[19], (DEPTH, D_CONV), 0.02),
        "w_o": nrm(ks[20], (DEPTH, D_MIX, D_MODEL), D_MIX ** -0.5),
        "norm2_g": 1.0 + nrm(ks[21], (DEPTH, D_MODEL), 0.02),
        "ffn_w_up": nrm(ks[22], (DEPTH, D_MODEL, 2 * D_FF), D_MODEL ** -0.5),
        "ffn_conv_w": nrm(ks[23], (DEPTH, FFN_CONV_W, 2 * D_FF), FFN_CONV_W ** -0.5),
        "ffn_conv_b": nrm(ks[24], (DEPTH, 2 * D_FF), 0.02),
        "ffn_w_down": nrm(ks[25], (DEPTH, D_FF, D_MODEL), D_FF ** -0.5),
        "final_norm_g": 1.0 + nrm(ks[26], (D_MODEL,), 0.02),
    }


def reference(x_prompt, x_sample, cache_k, cache_v, page_table, state_conv, state_ffn_conv,
              meta_tokens, rel_bias, norm1_g, w_in, lambda_q1, lambda_k1, lambda_q2, lambda_k2,
              subln_g, conv_w, conv_b, conv_ln_g, conv_ln_b, w_o, norm2_g, ffn_w_up,
              ffn_conv_w, ffn_conv_b, ffn_w_down, final_norm_g):
    B, S, D = x_prompt.shape
    Bd, T = x_sample.shape[:2]
    L = N_META + S
    n_blk = S // Q_BLOCK
    past = page_table.shape[1] * cache_k.shape[2]

    hp = jnp.concatenate([jnp.broadcast_to(meta_tokens[None].astype(x_prompt.dtype), (B, N_META, D)),
                          x_prompt], axis=1)
    hs = x_sample
    pos_p = jnp.arange(L, dtype=jnp.int32)
    q_pos_s = past + jnp.arange(T, dtype=jnp.int32)
    k_pos_s = jnp.arange(past + T, dtype=jnp.int32)

    kp_l, vp_l, cp_l, fp_l, ks_l, vs_l, cs_l, fs_l = [], [], [], [], [], [], [], []
    for l in range(DEPTH):
        li = lambda_init(l)
        lam = diff_lambda(lambda_q1[l], lambda_k1[l], lambda_q2[l], lambda_k2[l], li)

        xn = rmsnorm(hp, norm1_g[l])
        q, k, v, u = project(xn, w_in[l])
        o_meta = diff_attend(q[:, :N_META], k, v, pos_p[:N_META], pos_p, rel_bias, lam)
        qb = jnp.moveaxis(q[:, N_META:].reshape(B, n_blk, Q_BLOCK, N_HEADS, 2, DK), 1, 0)
        pb = pos_p[N_META:].reshape(n_blk, Q_BLOCK)
        o_blk = lax.map(lambda a: diff_attend(a[0], k, v, a[1], pos_p, rel_bias, lam), (qb, pb))
        o_real = jnp.moveaxis(o_blk, 0, 1).reshape(B, S, N_HEADS, DV)
        o = jnp.concatenate([o_meta, o_real], axis=1)
        u_ext = jnp.pad(u, ((0, 0), (CONV_W - 1, 0), (0, 0)))
        c = conv_module(u_ext, conv_w[l], conv_b[l], conv_ln_g[l], conv_ln_b[l])
        hp = hp + merge_out(o, c, subln_g[l], li, w_o[l])
        up = rmsnorm(hp, norm2_g[l]) @ ffn_w_up[l]
        up_ext = jnp.pad(up, ((0, 0), (FFN_CONV_W - 1, 0), (0, 0)))
        hp = hp + conv_ffn_tail(up_ext, ffn_conv_w[l], ffn_conv_b[l], ffn_w_down[l])
        kp_l.append(k.reshape(B, L, N_HEADS, 2 * DK))
        vp_l.append(v)
        cp_l.append(u[:, L - (CONV_W - 1):])
        fp_l.append(up[:, L - (FFN_CONV_W - 1):])

        xn = rmsnorm(hs, norm1_g[l])
        q, k, v, u = project(xn, w_in[l])
        ck, cv = cache_k[l], cache_v[l]

        def one_seq(a):
            q_i, k_i, v_i, pages = a
            k_past = ck[pages].reshape(past, N_HEADS, 2, DK)
            v_past = cv[pages].reshape(past, N_HEADS, DV)
            kk = jnp.concatenate([k_past, k_i], axis=0)
            vv = jnp.concatenate([v_past, v_i], axis=0)
            return diff_attend(q_i[None], kk[None], vv[None], q_pos_s, k_pos_s, rel_bias, lam)[0]

        o = lax.map(one_seq, (q, k, v, page_table))
        u_ext = jnp.concatenate([state_conv[l].astype(u.dtype), u], axis=1)
        c = conv_module(u_ext, conv_w[l], conv_b[l], conv_ln_g[l], conv_ln_b[l])
        hs = hs + merge_out(o, c, subln_g[l], li, w_o[l])
        up = rmsnorm(hs, norm2_g[l]) @ ffn_w_up[l]
        up_ext = jnp.concatenate([state_ffn_conv[l].astype(up.dtype), up], axis=1)
        hs = hs + conv_ffn_tail(up_ext, ffn_conv_w[l], ffn_conv_b[l], ffn_w_down[l])
        ks_l.append(k.reshape(Bd, T, N_HEADS, 2 * DK))
        vs_l.append(v)
        cs_l.append(u_ext[:, u_ext.shape[1] - (CONV_W - 1):])
        fs_l.append(up_ext[:, up_ext.shape[1] - (FFN_CONV_W - 1):])

    y_prompt = rmsnorm(hp, final_norm_g)[:, N_META:]
    y_sample = rmsnorm(hs, final_norm_g)
    return (y_prompt, y_sample, jnp.stack(kp_l), jnp.stack(vp_l), jnp.stack(cp_l), jnp.stack(fp_l),
            jnp.stack(ks_l), jnp.stack(vs_l), jnp.stack(cs_l), jnp.stack(fs_l))
```

```python
import functools
import math

import jax
import jax.numpy as jnp
from jax import lax
from jax.experimental import pallas as pl
from jax.experimental.pallas import tpu as pltpu

F32 = jnp.float32
BF16 = jnp.bfloat16

EPS = 1e-6
MAX_DISTANCE = 128
LAMBDA_INIT_0 = 0.8 - 0.6 * math.exp(-0.3 * 0)
NEG = -1e30

LANES = 128
VMEM_LIMIT = 56 * 1024 * 1024

ATTN_T = 256
ROW_TILE = 512
FF_TILE = 512
CONV_ROWS = 32
DEC_PAGES = 8


def _cparams(*sem):
    return pltpu.CompilerParams(dimension_semantics=sem, vmem_limit_bytes=VMEM_LIMIT)


def _rms(x, g):
    return x * lax.rsqrt(jnp.mean(x * x, axis=-1, keepdims=True) + EPS) * g


def _layernorm(y, g, b):
    mu = jnp.mean(y, axis=-1, keepdims=True)
    d = y - mu
    var = jnp.mean(d * d, axis=-1, keepdims=True)
    return d * lax.rsqrt(var + EPS) * g + b


def _silu(x):
    return x * jax.nn.sigmoid(x)


def _lambda(lamv_ref):
    a = jnp.sum(lamv_ref[0:1, :] * lamv_ref[1:2, :], axis=-1, keepdims=True)
    b = jnp.sum(lamv_ref[2:3, :] * lamv_ref[3:4, :], axis=-1, keepdims=True)
    return jnp.exp(a) - jnp.exp(b) + LAMBDA_INIT_0


def _t5_bucket(rel, n_buckets):
    n = jnp.maximum(rel, 0)
    max_exact = n_buckets // 2
    nf = jnp.maximum(n, 1).astype(F32)
    large = max_exact + (jnp.log(nf / max_exact) / math.log(MAX_DISTANCE / max_exact)
                         * (n_buckets - max_exact)).astype(jnp.int32)
    large = jnp.minimum(large, n_buckets - 1)
    return jnp.where(n < max_exact, n, large)


def _bias_tile(rel_bias, rel):
    b = rel_bias[_t5_bucket(rel, rel_bias.shape[0])]
    b = jnp.where((rel >= 0)[..., None], b, NEG)
    return jnp.transpose(b, (2, 0, 1)).astype(F32)


def _rmsnorm_cast_kernel(x_ref, g_ref, o_ref):
    o_ref[...] = _rms(x_ref[...], g_ref[...]).astype(o_ref.dtype)


def rmsnorm_cast(x, g, dtype):
    return pl.pallas_call(
        _rmsnorm_cast_kernel,
        out_shape=jax.ShapeDtypeStruct(x.shape, dtype),
        name="rmsnorm_cast",
    )(x, g.reshape(1, -1))


def _mm_small_kernel(x_ref, w_ref, o_ref):
    o_ref[...] = jnp.dot(x_ref[...], w_ref[...], preferred_element_type=F32)


def mm_small(x, w, tn=512):
    m, k = x.shape
    n = w.shape[1]
    return pl.pallas_call(
        _mm_small_kernel,
        grid=(n // tn,),
        in_specs=[pl.BlockSpec((m, k), lambda j: (0, 0)),
                  pl.BlockSpec((k, tn), lambda j: (0, j))],
        out_specs=pl.BlockSpec((m, tn), lambda j: (0, j)),
        out_shape=jax.ShapeDtypeStruct((m, n), F32),
        compiler_params=_cparams("arbitrary"),
        name="mm_small",
    )(x, w)


def _inproj_kernel(x_ref, g_ref, w_ref, wg_ref, q_ref, k_ref, v_ref, kb_ref, vb_ref, u_ref,
                   xn_ref, *, q_scale):
    j = pl.program_id(1)

    @pl.when(j == 0)
    def _():
        xn_ref[...] = _rms(x_ref[...], g_ref[...]).astype(BF16)

    z = jnp.dot(xn_ref[...], w_ref[...], preferred_element_type=F32)

    @pl.when(j == 0)
    def _():
        q_ref[...] = (z * q_scale).astype(BF16)

    @pl.when(j == 1)
    def _():
        k_ref[...] = z
        kb_ref[...] = z.astype(BF16)

    @pl.when(j == 2)
    def _():
        v_ref[...] = z
        vb_ref[...] = z.astype(BF16)

    @pl.when(j == 3)
    def _():
        gate = jnp.dot(xn_ref[...], wg_ref[...], preferred_element_type=F32)
        u_ref[...] = z * jax.nn.sigmoid(gate)


def prompt_inproj(x, g, w_in, d_attn, q_scale, tm):
    m, d = x.shape
    c = d_attn
    row = lambda i, j: (i, 0)
    out_f32 = jax.ShapeDtypeStruct((m, c), F32)
    out_bf16 = jax.ShapeDtypeStruct((m, c), BF16)
    return pl.pallas_call(
        functools.partial(_inproj_kernel, q_scale=q_scale),
        grid=(m // tm, 4),
        in_specs=[pl.BlockSpec((tm, d), row),
                  pl.BlockSpec((1, d), lambda i, j: (0, 0)),
                  pl.BlockSpec((d, c), lambda i, j: (0, j)),
                  pl.BlockSpec((d, c), lambda i, j: (0, 4))],
        out_specs=[pl.BlockSpec((tm, c), row)] * 6,
        out_shape=[out_bf16, out_f32, out_f32, out_bf16, out_bf16, out_f32],
        scratch_shapes=[pltpu.VMEM((tm, d), BF16)],
        compiler_params=_cparams("arbitrary", "arbitrary"),
        name="prompt_inproj",
    )(x, g.reshape(1, -1), w_in, w_in)


def _stack_q(q, dk):
    lane = lax.broadcasted_iota(jnp.int32, q.shape, 1)
    zero = jnp.zeros_like(q)
    return jnp.concatenate([jnp.where(lane < dk, q, zero), jnp.where(lane >= dk, q, zero)], axis=0)


def _online_step(qq, kblk, vblk, bias, carry):
    m, l, acc = carry
    s = lax.dot_general(qq, kblk, (((1,), (1,)), ((), ())), preferred_element_type=F32) + bias
    m_new = jnp.maximum(m, jnp.max(s, axis=-1, keepdims=True))
    alpha = jnp.exp(m - m_new)
    p = jnp.exp(s - m_new)
    l = alpha * l + jnp.sum(p, axis=-1, keepdims=True)
    acc = alpha * acc + jnp.dot(p.astype(BF16), vblk, preferred_element_type=F32)
    return m_new, l, acc


def _combine_heads_out(acc, l, lam, g, t):
    o = acc[:t] / l[:t] - lam * (acc[t:] / l[t:])
    return _rms(o, g) * (1.0 - LAMBDA_INIT_0)


def _prompt_attn_kernel(b_far_ref, q_ref, k_ref, v_ref, km_ref, vm_ref, d0_ref, d1_ref, dm_ref,
                        lamv_ref, g_ref, o_ref, *, t, dk):
    h = pl.program_id(1)
    qi = pl.program_id(2)
    b_far = b_far_ref[h]
    qq = _stack_q(q_ref[...], dk)
    r = 2 * t
    dv = v_ref.shape[-1]

    def two(tile):
        return jnp.concatenate([tile, tile], axis=0)

    carry = (jnp.full((r, 1), NEG, F32), jnp.zeros((r, 1), F32), jnp.zeros((r, dv), F32))

    bias_m = jnp.where(qi == 0, two(dm_ref[...]), b_far)
    carry = _online_step(qq, km_ref[...], vm_ref[...], bias_m, carry)

    def far_body(j, c):
        off = pl.multiple_of(j * t, t)
        return _online_step(qq, k_ref[pl.ds(off, t), :], v_ref[pl.ds(off, t), :], b_far, c)

    carry = lax.fori_loop(0, jnp.maximum(qi - 1, 0), far_body, carry)

    def sub_diag(c):
        off = pl.multiple_of((qi - 1) * t, t)
        return _online_step(qq, k_ref[pl.ds(off, t), :], v_ref[pl.ds(off, t), :],
                            two(d1_ref[...]), c)

    carry = lax.cond(qi >= 1, sub_diag, lambda c: c, carry)

    off = pl.multiple_of(qi * t, t)
    m, l, acc = _online_step(qq, k_ref[pl.ds(off, t), :], v_ref[pl.ds(off, t), :],
                             two(d0_ref[...]), carry)
    o_ref[...] = _combine_heads_out(acc, l, _lambda(lamv_ref), g_ref[...], t).astype(o_ref.dtype)


def prompt_attention(q, kb, vb, kmb, vmb, d0, d1, dm, b_far, lamv, subln_g, n_heads, t):
    b, s, _ = q.shape
    e = kb.shape[-1] // n_heads
    dv = vb.shape[-1] // n_heads
    nm = kmb.shape[0]
    grid_spec = pltpu.PrefetchScalarGridSpec(
        num_scalar_prefetch=0,
        grid=(b, n_heads, s // t),
        in_specs=[
            pl.BlockSpec(memory_space=pltpu.SMEM),
            pl.BlockSpec((None, t, e), lambda bi, h, qi: (bi, qi, h)),
            pl.BlockSpec((None, s, e), lambda bi, h, qi: (bi, 0, h)),
            pl.BlockSpec((None, s, dv), lambda bi, h, qi: (bi, 0, h)),
            pl.BlockSpec((nm, e), lambda bi, h, qi: (0, h)),
            pl.BlockSpec((nm, dv), lambda bi, h, qi: (0, h)),
            pl.BlockSpec((None, t, t), lambda bi, h, qi: (h, 0, 0)),
            pl.BlockSpec((None, t, t), lambda bi, h, qi: (h, 0, 0)),
            pl.BlockSpec((None, t, nm), lambda bi, h, qi: (h, 0, 0)),
            pl.BlockSpec((4, e // 2), lambda bi, h, qi: (0, 0)),
            pl.BlockSpec((1, dv), lambda bi, h, qi: (0, 0)),
        ],
        out_specs=pl.BlockSpec((None, t, dv), lambda bi, h, qi: (bi, qi, h)),
    )
    return pl.pallas_call(
        functools.partial(_prompt_attn_kernel, t=t, dk=e // 2),
        grid_spec=grid_spec,
        out_shape=jax.ShapeDtypeStruct((b, s, n_heads * dv), BF16),
        compiler_params=_cparams("arbitrary", "arbitrary", "arbitrary"),
        name="prompt_attention",
    )(b_far, q, kb, vb, kmb, vmb, d0, d1, dm, lamv, subln_g.reshape(1, -1))


def _meta_attn_kernel(q_ref, k_ref, v_ref, d_ref, lamv_ref, g_ref, o_ref, *, n_heads, dk):
    t = q_ref.shape[0]
    e = 2 * dk
    dv = v_ref.shape[-1] // n_heads
    lam = _lambda(lamv_ref)
    for h in range(n_heads):
        qq = _stack_q(q_ref[:, h * e:(h + 1) * e], dk)
        kblk = k_ref[:, h * e:(h + 1) * e].astype(BF16)
        vblk = v_ref[:, h * dv:(h + 1) * dv].astype(BF16)
        tile = d_ref[h]
        carry = (jnp.full((2 * t, 1), NEG, F32), jnp.zeros((2 * t, 1), F32),
                 jnp.zeros((2 * t, dv), F32))
        m, l, acc = _online_step(qq, kblk, vblk, jnp.concatenate([tile, tile], axis=0), carry)
        o_ref[:, h * dv:(h + 1) * dv] = _combine_heads_out(acc, l, lam, g_ref[...], t).astype(o_ref.dtype)


def meta_attention(q, k, v, dmeta, lamv, subln_g, n_heads):
    t = q.shape[0]
    dv = v.shape[-1] // n_heads
    return pl.pallas_call(
        functools.partial(_meta_attn_kernel, n_heads=n_heads, dk=k.shape[-1] // n_heads // 2),
        out_shape=jax.ShapeDtypeStruct((t, n_heads * dv), BF16),
        name="meta_attention",
    )(q, k, v, dmeta, lamv, subln_g.reshape(1, -1))


def _decode_attn_kernel(pt_ref, q_ref, kn_ref, vn_ref, mb_ref, mbl_ref, b0_ref, lamv_ref, g_ref, *rest,
                        n_pages_step, n_heads, dk):
    k_refs = rest[:n_pages_step]
    v_refs = rest[n_pages_step:2 * n_pages_step]
    o_ref = rest[2 * n_pages_step]
    qq_ref, m_ref, l_ref, acc_ref = rest[2 * n_pages_step + 1:]
    del pt_ref
    g = pl.program_id(1)
    n_g = pl.num_programs(1)
    e = 2 * dk

    @pl.when(g == 0)
    def _():
        qf = q_ref[...].astype(F32)
        lane = lax.broadcasted_iota(jnp.int32, qf.shape, 1)
        qq = jnp.concatenate([jnp.where(lane < dk, qf, 0.0), jnp.where(lane >= dk, qf, 0.0)], axis=0)
        qq_ref[...] = qq.astype(BF16)
        kn = jnp.concatenate([kn_ref[...], kn_ref[...]], axis=0)
        s_new = jnp.sum(qq * kn, axis=-1, keepdims=True) + b0_ref[...]
        m_ref[...] = s_new
        l_ref[...] = jnp.ones_like(s_new)
        acc_ref[...] = jnp.concatenate([vn_ref[...], vn_ref[...]], axis=0)

    qq = qq_ref[...]
    carry = (m_ref[...], l_ref[...], acc_ref[...])
    for p in range(n_pages_step):
        bias = mb_ref[...]
        if p == n_pages_step - 1:
            bias = jnp.where(g == n_g - 1, mbl_ref[...], bias)
        rows = k_refs[p].shape[0] * n_heads
        kblk = k_refs[p][...].reshape(rows, e).astype(BF16)
        vblk = v_refs[p][...].reshape(rows, v_refs[p].shape[-1]).astype(BF16)
        carry = _online_step(qq, kblk, vblk, bias, carry)
    m_ref[...], l_ref[...], acc_ref[...] = carry

    @pl.when(g == n_g - 1)
    def _():
        m, l, acc = carry
        o = acc[:n_heads] / l[:n_heads] - _lambda(lamv_ref) * (acc[n_heads:] / l[n_heads:])
        o_ref[...] = (_rms(o, g_ref[...]) * (1.0 - LAMBDA_INIT_0)).astype(o_ref.dtype)


def decode_attention(page_table, q, k_new, v_new, cache_k, cache_v, mb, mbl, b0, lamv, subln_g,
                     pages_per_step):
    bd, n_pages = page_table.shape
    _, page, n_heads, e = cache_k.shape
    dv = cache_v.shape[-1]
    p_step = pages_per_step
    r = 2 * n_heads
    seq_blk = lambda w: pl.BlockSpec((None, n_heads, w), lambda b, g, pt: (b, 0, 0))
    const2 = lambda b, g, pt: (0, 0)

    def page_spec(p, w):
        return pl.BlockSpec((None, page, n_heads, w), lambda b, g, pt: (pt[b, g * p_step + p], 0, 0, 0))

    grid_spec = pltpu.PrefetchScalarGridSpec(
        num_scalar_prefetch=1,
        grid=(bd, n_pages // p_step),
        in_specs=[seq_blk(e), seq_blk(e), seq_blk(dv),
                  pl.BlockSpec((r, page * n_heads), const2),
                  pl.BlockSpec((r, page * n_heads), const2),
                  pl.BlockSpec((r, 1), const2),
                  pl.BlockSpec((4, e // 2), const2),
                  pl.BlockSpec((1, dv), const2)]
                 + [page_spec(p, e) for p in range(p_step)]
                 + [page_spec(p, dv) for p in range(p_step)],
        out_specs=seq_blk(dv),
        scratch_shapes=[pltpu.VMEM((r, e), BF16), pltpu.VMEM((r, 1), F32),
                        pltpu.VMEM((r, 1), F32), pltpu.VMEM((r, dv), F32)],
    )
    return pl.pallas_call(
        functools.partial(_decode_attn_kernel, n_pages_step=p_step, n_heads=n_heads, dk=e // 2),
        grid_spec=grid_spec,
        out_shape=jax.ShapeDtypeStruct((bd, n_heads, dv), BF16),
        compiler_params=_cparams("arbitrary", "arbitrary"),
        name="decode_attention",
    )(page_table, q, k_new, v_new, mb, mbl, b0, lamv, subln_g.reshape(1, -1),
      *([cache_k] * p_step), *([cache_v] * p_step))


def _conv_rows(src_ref, base, n_rows, cw_ref, conv_w):
    acc = src_ref[pl.ds(base, n_rows), :] * cw_ref[0:1, :]
    for w in range(1, conv_w):
        acc = acc + src_ref[pl.ds(base + w, n_rows), :] * cw_ref[w:w + 1, :]
    return acc


def _prompt_conv_kernel(u_ref, uh_ref, head_ref, cw_ref, cb_ref, lg_ref, lb_ref, c_ref, s_ref, win_ref,
                        *, conv_w, halo, rows):
    i = pl.program_id(1)
    tm = u_ref.shape[0]
    s_ref[0:halo, :] = jnp.where(i == 0, head_ref[...], uh_ref[...])
    s_ref[halo:, :] = u_ref[...]
    first = halo - (conv_w - 1)

    def chunk(ci, _):
        r0 = pl.multiple_of(ci * rows, rows)
        win_ref[...] = s_ref[pl.ds(r0, rows + halo), :]
        y = _conv_rows(win_ref, first, rows, cw_ref, conv_w) + cb_ref[...]
        c_ref[pl.ds(r0, rows), :] = _silu(_layernorm(y, lg_ref[...], lb_ref[...])).astype(c_ref.dtype)
        return 0

    lax.fori_loop(0, tm // rows, chunk, 0)


def prompt_conv_module(u, head, conv_w, conv_b, ln_g, ln_b, tm):
    b, s, c = u.shape
    halo = head.shape[0]
    w = conv_w.shape[0]
    per = tm // halo
    vec = lambda a: a.reshape(1, -1)
    const = lambda bi, i: (0, 0)
    return pl.pallas_call(
        functools.partial(_prompt_conv_kernel, conv_w=w, halo=halo, rows=CONV_ROWS),
        grid=(b, s // tm),
        in_specs=[pl.BlockSpec((None, tm, c), lambda bi, i: (bi, i, 0)),
                  pl.BlockSpec((None, halo, c), lambda bi, i: (bi, jnp.maximum(i * per - 1, 0), 0)),
                  pl.BlockSpec((halo, c), const),
                  pl.BlockSpec((w, c), const),
                  pl.BlockSpec((1, c), const), pl.BlockSpec((1, c), const), pl.BlockSpec((1, c), const)],
        out_specs=pl.BlockSpec((None, tm, c), lambda bi, i: (bi, i, 0)),
        out_shape=jax.ShapeDtypeStruct((b, s, c), BF16),
        scratch_shapes=[pltpu.VMEM((tm + halo, c), F32), pltpu.VMEM((CONV_ROWS + halo, c), F32)],
        compiler_params=_cparams("arbitrary", "arbitrary"),
        name="prompt_conv_module",
    )(u, u, head, conv_w, vec(conv_b), vec(ln_g), vec(ln_b))


def _small_conv_kernel(za_ref, zg_ref, st_ref, cw_ref, cb_ref, lg_ref, lb_ref, u_ref, c_ref, s_ref,
                       *, conv_w, n_sample):
    n_meta = za_ref.shape[0] - n_sample
    u = za_ref[...] * jax.nn.sigmoid(zg_ref[...])
    u_ref[...] = u
    y = u[:n_sample] * cw_ref[conv_w - 1:conv_w, :]
    for w in range(conv_w - 1):
        y = y + st_ref[w] * cw_ref[w:w + 1, :]
    y_s = y + cb_ref[...]
    hist = s_ref.shape[0] - n_meta
    s_ref[0:hist, :] = jnp.zeros((hist, s_ref.shape[1]), F32)
    s_ref[hist:, :] = u[n_sample:]
    y_m = _conv_rows(s_ref, hist - (conv_w - 1), n_meta, cw_ref, conv_w) + cb_ref[...]
    yy = jnp.concatenate([y_s, y_m], axis=0)
    c_ref[...] = _silu(_layernorm(yy, lg_ref[...], lb_ref[...])).astype(c_ref.dtype)


def small_conv_module(za, zg, state_t, conv_w, conv_b, ln_g, ln_b, n_sample):
    m, c = za.shape
    w = conv_w.shape[0]
    n_meta = m - n_sample
    hist = 8 * ((w - 1 + 7) // 8)
    vec = lambda a: a.reshape(1, -1)
    return pl.pallas_call(
        functools.partial(_small_conv_kernel, conv_w=w, n_sample=n_sample),
        out_shape=[jax.ShapeDtypeStruct((m, c), F32), jax.ShapeDtypeStruct((m, c), BF16)],
        scratch_shapes=[pltpu.VMEM((hist + n_meta, c), F32)],
        name="small_conv_module",
    )(za, zg, state_t, conv_w, vec(conv_b), vec(ln_g), vec(ln_b))


def _merge_kernel(o_ref, c_ref, x_ref, w_ref, y_ref):
    da = o_ref.shape[-1]
    y = x_ref[...] + jnp.dot(o_ref[...], w_ref[0:da, :], preferred_element_type=F32)
    y_ref[...] = y + jnp.dot(c_ref[...], w_ref[da:, :], preferred_element_type=F32)


def merge_out(o_n, c, x, w_o, tm):
    m, d = x.shape
    da = o_n.shape[-1]
    row = lambda i: (i, 0)
    return pl.pallas_call(
        _merge_kernel,
        grid=(m // tm,),
        in_specs=[pl.BlockSpec((tm, da), row), pl.BlockSpec((tm, d - da), row),
                  pl.BlockSpec((tm, d), row), pl.BlockSpec((d, d), lambda i: (0, 0))],
        out_specs=pl.BlockSpec((tm, d), row),
        out_shape=jax.ShapeDtypeStruct((m, d), F32),
        compiler_params=_cparams("arbitrary"),
        name="merge_out",
    )(o_n, c, x, w_o)


def _gelu(x):
    return 0.5 * x * (1.0 + lax.erf(x * math.sqrt(0.5)))


def _ffn_kernel(x_ref, g2_ref, gf_ref, wa_ref, wg_ref, wd_ref, cwa_ref, cwg_ref, cba_ref, cbg_ref,
                ha_ref, hg_ref, y_ref, ta_ref, tg_ref,
                xn_ref, acc_ref, ea_ref, eg_ref, ca_ref, cg_ref, *, blocks_per_seq, conv_w):
    i = pl.program_id(0)
    f = pl.program_id(1)
    n_f = pl.num_programs(1)
    tm = x_ref.shape[0]
    pad = 8

    @pl.when(f == 0)
    def _():
        xn_ref[...] = _rms(x_ref[...], g2_ref[...]).astype(BF16)
        acc_ref[...] = jnp.zeros_like(acc_ref)

    xn = xn_ref[...]
    up_a = jnp.dot(xn, wa_ref[...], preferred_element_type=F32)
    up_g = jnp.dot(xn, wg_ref[...], preferred_element_type=F32)
    ta_ref[...] = up_a[tm - pad:]
    tg_ref[...] = up_g[tm - pad:]

    first = (i % blocks_per_seq) == 0
    ea_ref[0:pad, :] = jnp.where(first, ha_ref[...], ca_ref[f])
    eg_ref[0:pad, :] = jnp.where(first, hg_ref[...], cg_ref[f])
    ea_ref[pad:, :] = up_a
    eg_ref[pad:, :] = up_g
    ca_ref[f] = up_a[tm - pad:]
    cg_ref[f] = up_g[tm - pad:]

    base = pad - (conv_w - 1)
    conv_a = _conv_rows(ea_ref, base, tm, cwa_ref, conv_w) + cba_ref[...]
    conv_g = _conv_rows(eg_ref, base, tm, cwg_ref, conv_w) + cbg_ref[...]
    hidden = (_gelu(conv_g) * conv_a).astype(BF16)
    acc_ref[...] += jnp.dot(hidden, wd_ref[...], preferred_element_type=F32)

    @pl.when(f == n_f - 1)
    def _():
        y_ref[...] = _rms(x_ref[...] + acc_ref[...], gf_ref[...])


def prompt_ffn(x, norm2_g, final_g, wa, wg, wd, cwa, cwg, cba, cbg, head_a, head_g, rows_per_seq,
               tm, tf):
    m, d = x.shape
    ffp = wa.shape[1]
    n_f = ffp // tf
    bps = rows_per_seq // tm
    w = cwa.shape[0]
    row = lambda i, f: (i, 0)
    const = lambda i, f: (0, 0)
    col = lambda i, f: (0, f)
    tail = pl.BlockSpec((None, 8, tf), lambda i, f: (i, 0, f))
    return pl.pallas_call(
        functools.partial(_ffn_kernel, blocks_per_seq=bps, conv_w=w),
        grid=(m // tm, n_f),
        in_specs=[pl.BlockSpec((tm, d), row),
                  pl.BlockSpec((1, d), const), pl.BlockSpec((1, d), const),
                  pl.BlockSpec((d, tf), col), pl.BlockSpec((d, tf), col),
                  pl.BlockSpec((tf, d), lambda i, f: (f, 0)),
                  pl.BlockSpec((w, tf), col), pl.BlockSpec((w, tf), col),
                  pl.BlockSpec((1, tf), col), pl.BlockSpec((1, tf), col),
                  pl.BlockSpec((8, tf), col), pl.BlockSpec((8, tf), col)],
        out_specs=[pl.BlockSpec((tm, d), row), tail, tail],
        out_shape=[jax.ShapeDtypeStruct((m, d), F32),
                   jax.ShapeDtypeStruct((m // tm, 8, ffp), F32),
                   jax.ShapeDtypeStruct((m // tm, 8, ffp), F32)],
        scratch_shapes=[pltpu.VMEM((tm, d), BF16), pltpu.VMEM((tm, d), F32),
                        pltpu.VMEM((tm + 8, tf), F32), pltpu.VMEM((tm + 8, tf), F32),
                        pltpu.VMEM((n_f, 8, tf), F32), pltpu.VMEM((n_f, 8, tf), F32)],
        compiler_params=_cparams("arbitrary", "arbitrary"),
        name="prompt_ffn",
    )(x, norm2_g.reshape(1, -1), final_g.reshape(1, -1), wa, wg, wd, cwa, cwg,
      cba.reshape(1, -1), cbg.reshape(1, -1), head_a, head_g)


def _small_gate_kernel(ua_ref, ug_ref, sa_ref, sg_ref, cwa_ref, cwg_ref, cba_ref, cbg_ref, h_ref,
                       *, conv_w):
    def conv(u_ref, s_ref, cw_ref, cb_ref):
        y = u_ref[...] * cw_ref[conv_w - 1:conv_w, :] + cb_ref[...]
        for w in range(conv_w - 1):
            y = y + s_ref[w] * cw_ref[w:w + 1, :]
        return y

    a = conv(ua_ref, sa_ref, cwa_ref, cba_ref)
    g = conv(ug_ref, sg_ref, cwg_ref, cbg_ref)
    h_ref[...] = (_gelu(g) * a).astype(h_ref.dtype)


def small_gate(up_a, up_g, st_a, st_g, cwa, cwg, cba, cbg):
    return pl.pallas_call(
        functools.partial(_small_gate_kernel, conv_w=cwa.shape[0]),
        out_shape=jax.ShapeDtypeStruct(up_a.shape, BF16),
        name="small_gate",
    )(up_a, up_g, st_a, st_g, cwa, cwg, cba.reshape(1, -1), cbg.reshape(1, -1))


def _residual_norm_kernel(x_ref, d_ref, g_ref, y_ref):
    y_ref[...] = _rms(x_ref[...] + d_ref[...], g_ref[...])


def residual_norm(x, delta, g):
    return pl.pallas_call(
        _residual_norm_kernel,
        out_shape=jax.ShapeDtypeStruct(x.shape, F32),
        name="residual_norm",
    )(x, delta, g.reshape(1, -1))


def kernel(x_prompt, x_sample, cache_k, cache_v, page_table, state_conv, state_ffn_conv, meta_tokens,
           rel_bias, norm1_g, w_in, lambda_q1, lambda_k1, lambda_q2, lambda_k2, subln_g, conv_w, conv_b,
           conv_ln_g, conv_ln_b, w_o, norm2_g, ffn_w_up, ffn_conv_w, ffn_conv_b, ffn_w_down,
           final_norm_g):
    depth = w_in.shape[0]
    assert depth == 1, "single-layer step only"
    b, s, d = x_prompt.shape
    bd, t_dec = x_sample.shape[:2]
    assert t_dec == 1
    n_meta = meta_tokens.shape[0]
    n_pool, page, n_heads, e = cache_k.shape[1:]
    dv = cache_v.shape[-1]
    dk = e // 2
    d_attn = n_heads * dv
    d_conv = conv_w.shape[-1]
    cw_len = conv_w.shape[1]
    fw_len = ffn_conv_w.shape[1]
    d_ff = ffn_w_down.shape[1]
    n_buckets = rel_bias.shape[0]
    t = ATTN_T
    tm = ROW_TILE
    tf = FF_TILE
    assert e == dv == LANES and d_conv == d_attn and w_in.shape[-1] == 5 * d_attn
    assert s % tm == 0 and s % t == 0 and n_meta <= t and page_table.shape[1] % DEC_PAGES == 0
    me = n_buckets // 2
    sat = math.ceil(me * (MAX_DISTANCE / me) ** ((n_buckets - 1 - me) / (n_buckets - me))) + 1
    assert t + 1 >= sat and page + 1 >= sat
    halo = 8 * ((cw_len - 1 + 7) // 8)
    assert n_meta <= halo and tm % halo == 0 and fw_len - 1 <= 8

    ffp = tf * ((d_ff + tf - 1) // tf)
    w_in_b = w_in[0].astype(BF16)
    w_o_b = w_o[0].astype(BF16)
    pad_c = lambda a: jnp.pad(a, ((0, 0), (0, ffp - d_ff)))
    wa_b = pad_c(ffn_w_up[0, :, :d_ff]).astype(BF16)
    wg_b = pad_c(ffn_w_up[0, :, d_ff:]).astype(BF16)
    wd_b = jnp.pad(ffn_w_down[0], ((0, ffp - d_ff), (0, 0))).astype(BF16)
    cwa, cwg = pad_c(ffn_conv_w[0, :, :d_ff]), pad_c(ffn_conv_w[0, :, d_ff:])
    cba, cbg = pad_c(ffn_conv_b[:, :d_ff])[0], pad_c(ffn_conv_b[:, d_ff:])[0]
    lamv = jnp.stack([lambda_q1[0], lambda_k1[0], lambda_q2[0], lambda_k2[0]]).astype(F32)
    q_scale = dk ** -0.5

    ar = lambda n: jnp.arange(n, dtype=jnp.int32)
    d0 = _bias_tile(rel_bias, ar(t)[:, None] - ar(t)[None, :])
    d1 = _bias_tile(rel_bias, t + ar(t)[:, None] - ar(t)[None, :])
    dm = _bias_tile(rel_bias, n_meta + ar(t)[:, None] - ar(n_meta)[None, :])
    dmeta = _bias_tile(rel_bias, ar(n_meta)[:, None] - ar(n_meta)[None, :])
    b_far = rel_bias[n_buckets - 1].astype(F32)
    stack2 = lambda a: jnp.concatenate([a, a], axis=0)
    same_head = (ar(n_heads)[:, None, None] == ar(n_heads)[None, None, :])
    blast = _bias_tile(rel_bias, (page - ar(page))[None, :])[:, 0, :]
    mb = stack2(jnp.where(same_head, b_far[:, None, None], NEG) * jnp.ones((1, page, 1), F32)
                ).reshape(2 * n_heads, page * n_heads)
    mbl = stack2(jnp.where(same_head, blast[:, :, None], NEG)).reshape(2 * n_heads, page * n_heads)
    b0 = stack2(rel_bias[0].astype(F32))[:, None]

    x_small = jnp.concatenate([x_sample[:, 0, :], meta_tokens.astype(F32)], axis=0)
    xn_small = rmsnorm_cast(x_small, norm1_g[0], BF16)
    z = mm_small(xn_small, w_in_b)
    q_small = (z[:, :d_attn] * q_scale).astype(BF16)
    k_small = z[:, d_attn:2 * d_attn]
    v_small = z[:, 2 * d_attn:3 * d_attn]
    state_t = jnp.transpose(state_conv[0], (1, 0, 2))
    u_small, c_small = small_conv_module(z[:, 3 * d_attn:4 * d_attn], z[:, 4 * d_attn:], state_t,
                                         conv_w[0], conv_b[0], conv_ln_g[0], conv_ln_b[0], bd)
    o_meta = meta_attention(q_small[bd:], k_small[bd:], v_small[bd:], dmeta, lamv, subln_g[0], n_heads)
    hd = lambda a: a.reshape(bd, n_heads, -1)
    o_dec = decode_attention(page_table, hd(q_small[:bd]), hd(k_small[:bd]), hd(v_small[:bd]),
                             cache_k[0], cache_v[0], mb, mbl, b0, lamv, subln_g[0], DEC_PAGES)
    o_small = jnp.concatenate([o_dec.reshape(bd, d_attn), o_meta], axis=0)
    h1_small = merge_out(o_small, c_small, x_small, w_o_b, x_small.shape[0])
    xn2_small = rmsnorm_cast(h1_small, norm2_g[0], BF16)
    up_a_small = mm_small(xn2_small, wa_b)
    up_g_small = mm_small(xn2_small, wg_b)
    st = jnp.transpose(state_ffn_conv[0], (1, 0, 2))
    hid = small_gate(up_a_small[:bd], up_g_small[:bd], pad_c(st[:, :, :d_ff].reshape(-1, d_ff)).reshape(fw_len - 1, bd, ffp),
                     pad_c(st[:, :, d_ff:].reshape(-1, d_ff)).reshape(fw_len - 1, bd, ffp), cwa, cwg, cba, cbg)
    down = mm_small(hid, wd_b)
    y_sample = residual_norm(h1_small[:bd], down, final_norm_g)

    xp = x_prompt.reshape(b * s, d)
    q_p, k_p, v_p, kb_p, vb_p, u_p = prompt_inproj(xp, norm1_g[0], w_in_b, d_attn, q_scale, tm)
    r3 = lambda a: a.reshape(b, s, -1)
    o_p = prompt_attention(r3(q_p), r3(kb_p), r3(vb_p), k_small[bd:].astype(BF16), v_small[bd:].astype(BF16),
                           d0, d1, dm, b_far, lamv, subln_g[0], n_heads, t)
    head_u = jnp.concatenate([jnp.zeros((halo - n_meta, d_conv), F32), u_small[bd:]], axis=0)
    c_p = prompt_conv_module(r3(u_p), head_u, conv_w[0], conv_b[0], conv_ln_g[0], conv_ln_b[0], tm)
    h1_p = merge_out(o_p.reshape(b * s, -1), c_p.reshape(b * s, -1), xp, w_o_b, tm)
    head_rows = lambda a: jnp.concatenate([jnp.zeros((8 - (fw_len - 1), ffp), F32), a[-(fw_len - 1):]], axis=0)
    y_p, tail_a, tail_g = prompt_ffn(h1_p, norm2_g[0], final_norm_g, wa_b, wg_b, wd_b, cwa, cwg, cba, cbg,
                                     head_rows(up_a_small[bd:]), head_rows(up_g_small[bd:]), s, tm, tf)

    def with_meta(meta_rows, real):
        m_b = jnp.broadcast_to(meta_rows[None], (b,) + meta_rows.shape)
        return jnp.concatenate([m_b, r3(real)], axis=1).reshape(1, b, n_meta + s, n_heads, -1)

    k_prompt = with_meta(k_small[bd:], k_p)
    v_prompt = with_meta(v_small[bd:], v_p)
    conv_prompt = r3(u_p)[None, :, s - (cw_len - 1):]
    n_t = fw_len - 1
    last_blk = lambda a: a[s // tm - 1::s // tm, 8 - n_t:, :d_ff]
    ffn_prompt = jnp.concatenate([last_blk(tail_a), last_blk(tail_g)], axis=-1)[None]
    k_sample = k_small[:bd].reshape(1, bd, 1, n_heads, e)
    v_sample = v_small[:bd].reshape(1, bd, 1, n_heads, dv)
    conv_sample = jnp.concatenate([state_conv[0][:, 1:], u_small[:bd, None, :]], axis=1)[None]
    up_s = jnp.concatenate([up_a_small[:bd, :d_ff], up_g_small[:bd, :d_ff]], axis=-1)
    ffn_sample = jnp.concatenate([state_ffn_conv[0][:, 1:], up_s[:, None, :]], axis=1)[None]
    return (y_p.reshape(b, s, d), y_sample.reshape(bd, 1, d), k_prompt, v_prompt, conv_prompt, ffn_prompt,
            k_sample, v_sample, conv_sample, ffn_sample)
```

```python
import functools
import math

import jax
import jax.numpy as jnp
from jax import lax
from jax.experimental import pallas as pl
from jax.experimental.pallas import tpu as pltpu

F32 = jnp.float32
BF16 = jnp.bfloat16

EPS = 1e-6
MAX_DISTANCE = 128
LAMBDA_INIT_0 = 0.8 - 0.6 * math.exp(-0.3 * 0)
NEG = -1e30
LOG2E = math.log2(math.e)

LANES = 128
SUBLANES = 8
VMEM_LIMIT = 56 * 1024 * 1024

ATTN_T = 512
ROW_TILE = 512
IN_ROWS = 1024
FF_ROWS = 1024
FF_TILE = 512
FF_SUB = 2
CONV_ROWS = 64
DEC_PAGES = 8


def _cparams(*sem):
    return pltpu.CompilerParams(dimension_semantics=sem, vmem_limit_bytes=VMEM_LIMIT)


def _rms(x, g):
    return x * lax.rsqrt(jnp.mean(x * x, axis=-1, keepdims=True) + EPS) * g


def _layernorm(y, g, b):
    mu = jnp.mean(y, axis=-1, keepdims=True)
    d = y - mu
    var = jnp.mean(d * d, axis=-1, keepdims=True)
    return d * lax.rsqrt(var + EPS) * g + b


def _silu(x):
    return x * jax.nn.sigmoid(x)


def _lambda(lamv_ref):
    a = jnp.sum(lamv_ref[0:1, :] * lamv_ref[1:2, :], axis=-1, keepdims=True)
    b = jnp.sum(lamv_ref[2:3, :] * lamv_ref[3:4, :], axis=-1, keepdims=True)
    return jnp.exp(a) - jnp.exp(b) + LAMBDA_INIT_0


def _t5_bucket(rel, n_buckets):
    n = jnp.maximum(rel, 0)
    max_exact = n_buckets // 2
    nf = jnp.maximum(n, 1).astype(F32)
    large = max_exact + (jnp.log(nf / max_exact) / math.log(MAX_DISTANCE / max_exact)
                         * (n_buckets - max_exact)).astype(jnp.int32)
    large = jnp.minimum(large, n_buckets - 1)
    return jnp.where(n < max_exact, n, large)


def _bias_tile(rel_bias, rel):
    nb = rel_bias.shape[0]
    onehot = (_t5_bucket(rel, nb)[..., None] == jnp.arange(nb, dtype=jnp.int32)).astype(F32)
    table = (rel_bias - rel_bias[nb - 1]).astype(F32) * LOG2E
    b = jnp.einsum("rcn,nh->hrc", onehot, table, precision=lax.Precision.HIGHEST)
    return jnp.where((rel >= 0)[None], b, NEG)


def _rmsnorm_cast_kernel(x_ref, g_ref, o_ref):
    o_ref[...] = _rms(x_ref[...], g_ref[...]).astype(o_ref.dtype)


def rmsnorm_cast(x, g, dtype):
    return pl.pallas_call(
        _rmsnorm_cast_kernel,
        out_shape=jax.ShapeDtypeStruct(x.shape, dtype),
        name="rmsnorm_cast",
    )(x, g.reshape(1, -1))


def _mm_small_kernel(x_ref, w_ref, o_ref):
    o_ref[...] = jnp.dot(x_ref[...], w_ref[...], preferred_element_type=F32)


def mm_small(x, w, tn=512):
    m, k = x.shape
    n = w.shape[1]
    return pl.pallas_call(
        _mm_small_kernel,
        grid=(n // tn,),
        in_specs=[pl.BlockSpec((m, k), lambda j: (0, 0)),
                  pl.BlockSpec((k, tn), lambda j: (0, j))],
        out_specs=pl.BlockSpec((m, tn), lambda j: (0, j)),
        out_shape=jax.ShapeDtypeStruct((m, n), F32),
        compiler_params=_cparams("arbitrary"),
        name="mm_small",
    )(x, w)


def _inproj_kernel(x_ref, g_ref, w_ref, wg_ref, ob_ref, of_ref, xn_ref, *, q_scale):
    j = pl.program_id(1)

    @pl.when(j == 0)
    def _():
        xn_ref[...] = _rms(x_ref[...], g_ref[...]).astype(BF16)

    z = jnp.dot(xn_ref[...], w_ref[...], preferred_element_type=F32)

    @pl.when(j == 0)
    def _():
        ob_ref[...] = (z * q_scale).astype(BF16)

    @pl.when((j == 1) | (j == 2))
    def _():
        of_ref[...] = z
        ob_ref[...] = z.astype(BF16)

    @pl.when(j == 3)
    def _():
        gate = jnp.dot(xn_ref[...], wg_ref[...], preferred_element_type=F32)
        of_ref[...] = z * jax.nn.sigmoid(gate)


def prompt_inproj(x, g, w_in, d_attn, q_scale, tm):
    m, d = x.shape
    c = d_attn
    return pl.pallas_call(
        functools.partial(_inproj_kernel, q_scale=q_scale),
        grid=(m // tm, 4),
        in_specs=[pl.BlockSpec((tm, d), lambda i, j: (i, 0)),
                  pl.BlockSpec((1, d), lambda i, j: (0, 0)),
                  pl.BlockSpec((d, c), lambda i, j: (0, j)),
                  pl.BlockSpec((d, c), lambda i, j: (0, 4), pipeline_mode=pl.Buffered(1))],
        out_specs=[pl.BlockSpec((None, tm, c), lambda i, j: (jnp.minimum(j, 2), i, 0)),
                   pl.BlockSpec((None, tm, c), lambda i, j: (jnp.maximum(j - 1, 0), i, 0))],
        out_shape=[jax.ShapeDtypeStruct((3, m, c), BF16), jax.ShapeDtypeStruct((3, m, c), F32)],
        scratch_shapes=[pltpu.VMEM((tm, d), BF16)],
        compiler_params=_cparams("arbitrary", "arbitrary"),
        name="prompt_inproj",
    )(x, g.reshape(1, -1), w_in, w_in)


def _stack_q(q, dk):
    lane = lax.broadcasted_iota(jnp.int32, q.shape, 1)
    zero = jnp.zeros_like(q)
    return jnp.concatenate([jnp.where(lane < dk, q, zero), jnp.where(lane >= dk, q, zero)], axis=0)


def _scores(qq, kblk):
    return lax.dot_general(qq, kblk, (((1,), (1,)), ((), ())), preferred_element_type=F32)


def _add_tile(s, tile):
    t, c = tile.shape
    return (s.reshape(2, t, c) + tile[None]).reshape(2 * t, c)


def _with_ones(v):
    return jnp.concatenate([v, jnp.ones_like(v)], axis=-1)


def _softmax_update(s, vx, carry):
    m, acc = carry
    m_new = jnp.maximum(m, jnp.max(s, axis=-1, keepdims=True))
    alpha = jnp.exp2(m - m_new)
    p = jnp.exp2(s - m_new).astype(BF16)
    return m_new, alpha * acc + jnp.dot(p, vx, preferred_element_type=F32)


def _combine_heads_out(acc, lam, g, t):
    dv = acc.shape[-1] // 2
    o = acc[:t, :dv] / acc[:t, dv:] - lam * (acc[t:, :dv] / acc[t:, dv:])
    return _rms(o, g) * (1.0 - LAMBDA_INIT_0)


def _prompt_attn_kernel(q_ref, k_ref, v_ref, km_ref, vm_ref, d0_ref, d1_ref, dm_ref,
                        lamv_ref, g_ref, o_ref, vx_ref, *, t, dk):
    qi = pl.program_id(2)
    r = 2 * t
    dv = v_ref.shape[-1]

    @pl.when(qi == 0)
    def _():
        vx_ref[:, :dv] = v_ref[...]
        vx_ref[:, dv:] = jnp.ones(v_ref.shape, BF16)

    qq = _stack_q(q_ref[...], dk)
    kblk = lambda j: k_ref[pl.ds(pl.multiple_of(j * t, t), t), :]
    vxblk = lambda j: vx_ref[pl.ds(pl.multiple_of(j * t, t), t), :]

    n_far = jnp.maximum(qi - 1, 0)

    def far_body(j, c):
        s_cur, m, acc = c
        s_next = _scores(qq, kblk(j + 1))
        m, acc = _softmax_update(s_cur, vxblk(j), (m, acc))
        return s_next, m, acc

    init = (_scores(qq, kblk(0)), jnp.full((r, 1), NEG, F32), jnp.zeros((r, 2 * dv), F32))
    s_prev, m, acc = lax.fori_loop(0, n_far, far_body, init)

    s_sub = _add_tile(s_prev, jnp.where(qi >= 1, d1_ref[...], NEG))
    s_diag = _add_tile(_scores(qq, kblk(qi)), d0_ref[...])
    s_meta = _add_tile(_scores(qq, km_ref[...]), jnp.where(qi == 0, dm_ref[...], 0.0))
    carry = _softmax_update(s_meta, _with_ones(vm_ref[...]), (m, acc))
    carry = _softmax_update(s_diag, vxblk(qi), carry)
    m, acc = _softmax_update(s_sub, vxblk(n_far), carry)
    o_ref[...] = _combine_heads_out(acc, _lambda(lamv_ref), g_ref[...], t).astype(o_ref.dtype)


def prompt_attention(qkv, kmb, vmb, d0, d1, dm, lamv, subln_g, n_heads, t):
    _, b, s, width = qkv.shape
    e = dv = width // n_heads
    nm = kmb.shape[0]
    return pl.pallas_call(
        functools.partial(_prompt_attn_kernel, t=t, dk=e // 2),
        grid=(b, n_heads, s // t),
        in_specs=[
            pl.BlockSpec((None, None, t, e), lambda bi, h, qi: (0, bi, qi, h)),
            pl.BlockSpec((None, None, s, e), lambda bi, h, qi: (1, bi, 0, h)),
            pl.BlockSpec((None, None, s, dv), lambda bi, h, qi: (2, bi, 0, h)),
            pl.BlockSpec((nm, e), lambda bi, h, qi: (0, h)),
            pl.BlockSpec((nm, dv), lambda bi, h, qi: (0, h)),
            pl.BlockSpec((None, t, t), lambda bi, h, qi: (h, 0, 0)),
            pl.BlockSpec((None, t, t), lambda bi, h, qi: (h, 0, 0)),
            pl.BlockSpec((None, t, nm), lambda bi, h, qi: (h, 0, 0)),
            pl.BlockSpec((4, e // 2), lambda bi, h, qi: (0, 0)),
            pl.BlockSpec((1, dv), lambda bi, h, qi: (0, 0)),
        ],
        out_specs=pl.BlockSpec((None, t, dv), lambda bi, h, qi: (bi, qi, h)),
        out_shape=jax.ShapeDtypeStruct((b, s, n_heads * dv), BF16),
        scratch_shapes=[pltpu.VMEM((s, 2 * dv), BF16)],
        compiler_params=_cparams("arbitrary", "arbitrary", "arbitrary"),
        name="prompt_attention",
    )(qkv, qkv, qkv, kmb, vmb, d0, d1, dm, lamv, subln_g.reshape(1, -1))


def _meta_attn_kernel(q_ref, k_ref, v_ref, d_ref, lamv_ref, g_ref, o_ref, *, n_heads, dk):
    t = q_ref.shape[0]
    e = 2 * dk
    dv = v_ref.shape[-1] // n_heads
    lam = _lambda(lamv_ref)
    for h in range(n_heads):
        qq = _stack_q(q_ref[:, h * e:(h + 1) * e], dk)
        kblk = k_ref[:, h * e:(h + 1) * e].astype(BF16)
        vx = _with_ones(v_ref[:, h * dv:(h + 1) * dv].astype(BF16))
        carry = (jnp.full((2 * t, 1), NEG, F32), jnp.zeros((2 * t, 2 * dv), F32))
        m, acc = _softmax_update(_add_tile(_scores(qq, kblk), d_ref[h]), vx, carry)
        o_ref[:, h * dv:(h + 1) * dv] = _combine_heads_out(acc, lam, g_ref[...], t).astype(o_ref.dtype)


def meta_attention(q, k, v, dmeta, lamv, subln_g, n_heads):
    t = q.shape[0]
    dv = v.shape[-1] // n_heads
    return pl.pallas_call(
        functools.partial(_meta_attn_kernel, n_heads=n_heads, dk=k.shape[-1] // n_heads // 2),
        out_shape=jax.ShapeDtypeStruct((t, n_heads * dv), BF16),
        name="meta_attention",
    )(q, k, v, dmeta, lamv, subln_g.reshape(1, -1))


def _decode_attn_kernel(pt_ref, q_ref, kn_ref, vn_ref, mb_ref, mbl_ref, b0_ref, lamv_ref, g_ref, *rest,
                        n_pages_step, n_heads, dk):
    k_refs = rest[:n_pages_step]
    v_refs = rest[n_pages_step:2 * n_pages_step]
    o_ref = rest[2 * n_pages_step]
    qq_ref, m_ref, l_ref, acc_ref = rest[2 * n_pages_step + 1:]
    del pt_ref
    g = pl.program_id(1)
    n_g = pl.num_programs(1)
    e = 2 * dk

    @pl.when(g == 0)
    def _():
        qf = q_ref[...].astype(F32)
        lane = lax.broadcasted_iota(jnp.int32, qf.shape, 1)
        qq = jnp.concatenate([jnp.where(lane < dk, qf, 0.0), jnp.where(lane >= dk, qf, 0.0)], axis=0)
        qq_ref[...] = qq.astype(BF16)
        kn = jnp.concatenate([kn_ref[...], kn_ref[...]], axis=0)
        s_new = jnp.sum(qq * kn, axis=-1, keepdims=True) + b0_ref[...]
        m_ref[...] = s_new
        l_ref[...] = jnp.ones_like(s_new)
        acc_ref[...] = jnp.concatenate([vn_ref[...], vn_ref[...]], axis=0)

    qq = qq_ref[...]
    s_pages = []
    for p in range(n_pages_step):
        bias = mb_ref[...]
        if p == n_pages_step - 1:
            bias = jnp.where(g == n_g - 1, mbl_ref[...], bias)
        rows = k_refs[p].shape[0] * n_heads
        s_pages.append(_scores(qq, k_refs[p][...].reshape(rows, e).astype(BF16)) + bias)
    m_old = m_ref[...]
    m_loc = s_pages[0]
    for s in s_pages[1:]:
        m_loc = jnp.maximum(m_loc, s)
    m_new = jnp.maximum(m_old, jnp.max(m_loc, axis=-1, keepdims=True))
    alpha = jnp.exp2(m_old - m_new)
    l_sum = None
    pv = None
    for p in range(n_pages_step):
        w = jnp.exp2(s_pages[p] - m_new)
        rows = v_refs[p].shape[0] * n_heads
        vblk = v_refs[p][...].reshape(rows, v_refs[p].shape[-1]).astype(BF16)
        d = jnp.dot(w.astype(BF16), vblk, preferred_element_type=F32)
        l_sum = w if l_sum is None else l_sum + w
        pv = d if pv is None else pv + d
    l_new = alpha * l_ref[...] + jnp.sum(l_sum, axis=-1, keepdims=True)
    acc_new = alpha * acc_ref[...] + pv
    m_ref[...] = m_new
    l_ref[...] = l_new
    acc_ref[...] = acc_new

    @pl.when(g == n_g - 1)
    def _():
        o = (acc_new[:n_heads] / l_new[:n_heads]
             - _lambda(lamv_ref) * (acc_new[n_heads:] / l_new[n_heads:]))
        o_ref[...] = (_rms(o, g_ref[...]) * (1.0 - LAMBDA_INIT_0)).astype(o_ref.dtype)


def decode_attention(page_table, q, k_new, v_new, cache_k, cache_v, mb, mbl, b0, lamv, subln_g,
                     pages_per_step):
    bd, n_pages = page_table.shape
    _, page, n_heads, e = cache_k.shape
    dv = cache_v.shape[-1]
    p_step = pages_per_step
    r = 2 * n_heads
    seq_blk = lambda w: pl.BlockSpec((None, n_heads, w), lambda b, g, pt: (b, 0, 0))
    const2 = lambda b, g, pt: (0, 0)

    def page_spec(p, w):
        return pl.BlockSpec((None, page, n_heads, w), lambda b, g, pt: (pt[b, g * p_step + p], 0, 0, 0))

    grid_spec = pltpu.PrefetchScalarGridSpec(
        num_scalar_prefetch=1,
        grid=(bd, n_pages // p_step),
        in_specs=[seq_blk(e), seq_blk(e), seq_blk(dv),
                  pl.BlockSpec((r, page * n_heads), const2),
                  pl.BlockSpec((r, page * n_heads), const2),
                  pl.BlockSpec((r, 1), const2),
                  pl.BlockSpec((4, e // 2), const2),
                  pl.BlockSpec((1, dv), const2)]
                 + [page_spec(p, e) for p in range(p_step)]
                 + [page_spec(p, dv) for p in range(p_step)],
        out_specs=seq_blk(dv),
        scratch_shapes=[pltpu.VMEM((r, e), BF16), pltpu.VMEM((r, 1), F32),
                        pltpu.VMEM((r, 1), F32), pltpu.VMEM((r, dv), F32)],
    )
    return pl.pallas_call(
        functools.partial(_decode_attn_kernel, n_pages_step=p_step, n_heads=n_heads, dk=e // 2),
        grid_spec=grid_spec,
        out_shape=jax.ShapeDtypeStruct((bd, n_heads, dv), BF16),
        compiler_params=_cparams("arbitrary", "arbitrary"),
        name="decode_attention",
    )(page_table, q, k_new, v_new, mb, mbl, b0, lamv, subln_g.reshape(1, -1),
      *([cache_k] * p_step), *([cache_v] * p_step))


def _conv_rows(src_ref, base, n_rows, cw_ref, conv_w):
    acc = src_ref[pl.ds(base, n_rows), :] * cw_ref[0:1, :]
    for w in range(1, conv_w):
        acc = acc + src_ref[pl.ds(base + w, n_rows), :] * cw_ref[w:w + 1, :]
    return acc


def _conv_tiles(win, cwb_ref, lanes, conv_w, first, n_out):
    y = None
    for k in range(SUBLANES):
        taps = [w for w in range(conv_w) if (first + w) % SUBLANES == k]
        if not taps:
            continue
        n_p = n_out if k == 0 else n_out + 1
        part = None
        for w in taps:
            j = (first + w) // SUBLANES
            term = win[j:j + n_p] * cwb_ref[w, :, lanes][None]
            part = term if part is None else part + term
        part = part.reshape(n_p * SUBLANES, part.shape[-1])
        part = part if k == 0 else part[k:k + n_out * SUBLANES]
        y = part if y is None else y + part
    return y


def _prompt_conv_kernel(u_ref, uh_ref, head_ref, cwb_ref, cb_ref, lg_ref, lb_ref, c_ref, s_ref, y_ref,
                        *, conv_w, halo, rows):
    i = pl.program_id(1)
    tm, c = u_ref.shape
    s_ref[0:halo, :] = jnp.where(i == 0, head_ref[...], uh_ref[...])
    s_ref[halo:, :] = u_ref[...]
    first = halo - (conv_w - 1)
    n_win = (rows + halo) // SUBLANES

    def chunk(ci, _):
        r0 = pl.multiple_of(ci * rows, rows)
        for lb in range(c // LANES):
            lanes = slice(lb * LANES, (lb + 1) * LANES)
            win = s_ref[pl.ds(r0, rows + halo), lanes].reshape(n_win, SUBLANES, LANES)
            y_ref[:, lanes] = _conv_tiles(win, cwb_ref, lanes, conv_w, first, rows // SUBLANES)
        y = y_ref[...] + cb_ref[...]
        c_ref[pl.ds(r0, rows), :] = _silu(_layernorm(y, lg_ref[...], lb_ref[...])).astype(c_ref.dtype)
        return 0

    lax.fori_loop(0, tm // rows, chunk, 0)


def prompt_conv_module(kvu, head, conv_w, conv_b, ln_g, ln_b, tm):
    _, b, s, c = kvu.shape
    halo = head.shape[0]
    w = conv_w.shape[0]
    per = tm // halo
    vec = lambda a: a.reshape(1, -1)
    const = lambda bi, i: (0, 0)
    cwb = jnp.broadcast_to(conv_w[:, None, :], (w, SUBLANES, c))
    return pl.pallas_call(
        functools.partial(_prompt_conv_kernel, conv_w=w, halo=halo, rows=CONV_ROWS),
        grid=(b, s // tm),
        in_specs=[pl.BlockSpec((None, None, tm, c), lambda bi, i: (2, bi, i, 0)),
                  pl.BlockSpec((None, None, halo, c), lambda bi, i: (2, bi, jnp.maximum(i * per - 1, 0), 0)),
                  pl.BlockSpec((halo, c), const),
                  pl.BlockSpec((w, SUBLANES, c), lambda bi, i: (0, 0, 0)),
                  pl.BlockSpec((1, c), const), pl.BlockSpec((1, c), const), pl.BlockSpec((1, c), const)],
        out_specs=pl.BlockSpec((None, tm, c), lambda bi, i: (bi, i, 0)),
        out_shape=jax.ShapeDtypeStruct((b, s, c), BF16),
        scratch_shapes=[pltpu.VMEM((tm + halo, c), F32), pltpu.VMEM((CONV_ROWS, c), F32)],
        compiler_params=_cparams("arbitrary", "arbitrary"),
        name="prompt_conv_module",
    )(kvu, kvu, head, cwb, vec(conv_b), vec(ln_g), vec(ln_b))


def _small_conv_kernel(za_ref, zg_ref, st_ref, cw_ref, cb_ref, lg_ref, lb_ref, u_ref, c_ref, s_ref,
                       *, conv_w, n_sample):
    n_meta = za_ref.shape[0] - n_sample
    u = za_ref[...] * jax.nn.sigmoid(zg_ref[...])
    u_ref[...] = u
    y = u[:n_sample] * cw_ref[conv_w - 1:conv_w, :]
    for w in range(conv_w - 1):
        y = y + st_ref[w] * cw_ref[w:w + 1, :]
    y_s = y + cb_ref[...]
    hist = s_ref.shape[0] - n_meta
    s_ref[0:hist, :] = jnp.zeros((hist, s_ref.shape[1]), F32)
    s_ref[hist:, :] = u[n_sample:]
    y_m = _conv_rows(s_ref, hist - (conv_w - 1), n_meta, cw_ref, conv_w) + cb_ref[...]
    yy = jnp.concatenate([y_s, y_m], axis=0)
    c_ref[...] = _silu(_layernorm(yy, lg_ref[...], lb_ref[...])).astype(c_ref.dtype)


def small_conv_module(za, zg, state_t, conv_w, conv_b, ln_g, ln_b, n_sample):
    m, c = za.shape
    w = conv_w.shape[0]
    n_meta = m - n_sample
    hist = SUBLANES * ((w - 1 + SUBLANES - 1) // SUBLANES)
    vec = lambda a: a.reshape(1, -1)
    return pl.pallas_call(
        functools.partial(_small_conv_kernel, conv_w=w, n_sample=n_sample),
        out_shape=[jax.ShapeDtypeStruct((m, c), F32), jax.ShapeDtypeStruct((m, c), BF16)],
        scratch_shapes=[pltpu.VMEM((hist + n_meta, c), F32)],
        name="small_conv_module",
    )(za, zg, state_t, conv_w, vec(conv_b), vec(ln_g), vec(ln_b))


def _merge_kernel(o_ref, c_ref, x_ref, w_ref, y_ref):
    da = o_ref.shape[-1]
    y = x_ref[...] + jnp.dot(o_ref[...], w_ref[0:da, :], preferred_element_type=F32)
    y_ref[...] = y + jnp.dot(c_ref[...], w_ref[da:, :], preferred_element_type=F32)


def merge_out(o_n, c, x, w_o, tm):
    m, d = x.shape
    da = o_n.shape[-1]
    row = lambda i: (i, 0)
    return pl.pallas_call(
        _merge_kernel,
        grid=(m // tm,),
        in_specs=[pl.BlockSpec((tm, da), row), pl.BlockSpec((tm, d - da), row),
                  pl.BlockSpec((tm, d), row), pl.BlockSpec((d, d), lambda i: (0, 0))],
        out_specs=pl.BlockSpec((tm, d), row),
        out_shape=jax.ShapeDtypeStruct((m, d), F32),
        compiler_params=_cparams("arbitrary"),
        name="merge_out",
    )(o_n, c, x, w_o)


def _gelu(x):
    return 0.5 * x * (1.0 + lax.erf(x * math.sqrt(0.5)))


def _ffn_kernel(x_ref, g2_ref, gf_ref, wa_ref, wg_ref, wd_ref, cwa_ref, cwg_ref, cba_ref, cbg_ref,
                ha_ref, hg_ref, y_ref, ta_ref, tg_ref,
                xn_ref, ea_ref, eg_ref, ca_ref, cg_ref, *, blocks_per_seq, conv_w, n_sub):
    i = pl.program_id(0)
    f = pl.program_id(1)
    n_f = pl.num_programs(1)
    tm = x_ref.shape[0]
    tf = wa_ref.shape[1]
    ts = tf // n_sub
    pad = SUBLANES

    @pl.when(f == 0)
    def _():
        x = x_ref[...]
        xn_ref[...] = _rms(x, g2_ref[...]).astype(BF16)
        y_ref[...] = x

    xn = xn_ref[...]
    first = (i % blocks_per_seq) == 0
    base = pad - (conv_w - 1)
    down = None
    for sc in range(n_sub):
        cols = slice(sc * ts, (sc + 1) * ts)
        up_a = jnp.dot(xn, wa_ref[:, cols], preferred_element_type=F32)
        up_g = jnp.dot(xn, wg_ref[:, cols], preferred_element_type=F32)
        ta_ref[:, cols] = up_a[tm - pad:]
        tg_ref[:, cols] = up_g[tm - pad:]
        ea_ref[sc, 0:pad, :] = jnp.where(first, ha_ref[:, cols], ca_ref[f, :, cols])
        eg_ref[sc, 0:pad, :] = jnp.where(first, hg_ref[:, cols], cg_ref[f, :, cols])
        ea_ref[sc, pad:, :] = up_a
        eg_ref[sc, pad:, :] = up_g
        ca_ref[f, :, cols] = up_a[tm - pad:]
        cg_ref[f, :, cols] = up_g[tm - pad:]
        conv_a = _conv_rows(ea_ref.at[sc], base, tm, cwa_ref.at[:, cols], conv_w) + cba_ref[:, cols]
        conv_g = _conv_rows(eg_ref.at[sc], base, tm, cwg_ref.at[:, cols], conv_w) + cbg_ref[:, cols]
        hidden = (_gelu(conv_g) * conv_a).astype(BF16)
        d = jnp.dot(hidden, wd_ref[cols, :], preferred_element_type=F32)
        down = d if down is None else down + d
    y_ref[...] += down

    @pl.when(f == n_f - 1)
    def _():
        y_ref[...] = _rms(y_ref[...], gf_ref[...])


def prompt_ffn(x, norm2_g, final_g, wa, wg, wd, cwa, cwg, cba, cbg, head_a, head_g, rows_per_seq,
               tm, tf):
    m, d = x.shape
    ffp = wa.shape[1]
    n_f = ffp // tf
    bps = rows_per_seq // tm
    w = cwa.shape[0]
    ts = tf // FF_SUB
    row = lambda i, f: (i, 0)
    const = lambda i, f: (0, 0)
    col = lambda i, f: (0, f)
    tail = pl.BlockSpec((None, SUBLANES, tf), lambda i, f: (i, 0, f))
    return pl.pallas_call(
        functools.partial(_ffn_kernel, blocks_per_seq=bps, conv_w=w, n_sub=FF_SUB),
        grid=(m // tm, n_f),
        in_specs=[pl.BlockSpec((tm, d), row, pipeline_mode=pl.Buffered(1)),
                  pl.BlockSpec((1, d), const), pl.BlockSpec((1, d), const),
                  pl.BlockSpec((d, tf), col), pl.BlockSpec((d, tf), col),
                  pl.BlockSpec((tf, d), lambda i, f: (f, 0)),
                  pl.BlockSpec((w, tf), col), pl.BlockSpec((w, tf), col),
                  pl.BlockSpec((1, tf), col), pl.BlockSpec((1, tf), col),
                  pl.BlockSpec((SUBLANES, tf), col), pl.BlockSpec((SUBLANES, tf), col)],
        out_specs=[pl.BlockSpec((tm, d), row), tail, tail],
        out_shape=[jax.ShapeDtypeStruct((m, d), F32),
                   jax.ShapeDtypeStruct((m // tm, SUBLANES, ffp), F32),
                   jax.ShapeDtypeStruct((m // tm, SUBLANES, ffp), F32)],
        scratch_shapes=[pltpu.VMEM((tm, d), BF16),
                        pltpu.VMEM((FF_SUB, tm + SUBLANES, ts), F32),
                        pltpu.VMEM((FF_SUB, tm + SUBLANES, ts), F32),
                        pltpu.VMEM((n_f, SUBLANES, tf), F32), pltpu.VMEM((n_f, SUBLANES, tf), F32)],
        compiler_params=_cparams("arbitrary", "arbitrary"),
        name="prompt_ffn",
    )(x, norm2_g.reshape(1, -1), final_g.reshape(1, -1), wa, wg, wd, cwa, cwg,
      cba.reshape(1, -1), cbg.reshape(1, -1), head_a, head_g)


def _small_gate_kernel(ua_ref, ug_ref, sa_ref, sg_ref, cwa_ref, cwg_ref, cba_ref, cbg_ref, h_ref,
                       *, conv_w):
    def conv(u_ref, s_ref, cw_ref, cb_ref):
        y = u_ref[...] * cw_ref[conv_w - 1:conv_w, :] + cb_ref[...]
        for w in range(conv_w - 1):
            y = y + s_ref[w] * cw_ref[w:w + 1, :]
        return y

    a = conv(ua_ref, sa_ref, cwa_ref, cba_ref)
    g = conv(ug_ref, sg_ref, cwg_ref, cbg_ref)
    h_ref[...] = (_gelu(g) * a).astype(h_ref.dtype)


def small_gate(up_a, up_g, st_a, st_g, cwa, cwg, cba, cbg):
    return pl.pallas_call(
        functools.partial(_small_gate_kernel, conv_w=cwa.shape[0]),
        out_shape=jax.ShapeDtypeStruct(up_a.shape, BF16),
        name="small_gate",
    )(up_a, up_g, st_a, st_g, cwa, cwg, cba.reshape(1, -1), cbg.reshape(1, -1))


def _residual_norm_kernel(x_ref, d_ref, g_ref, y_ref):
    y_ref[...] = _rms(x_ref[...] + d_ref[...], g_ref[...])


def residual_norm(x, delta, g):
    return pl.pallas_call(
        _residual_norm_kernel,
        out_shape=jax.ShapeDtypeStruct(x.shape, F32),
        name="residual_norm",
    )(x, delta, g.reshape(1, -1))


def kernel(x_prompt, x_sample, cache_k, cache_v, page_table, state_conv, state_ffn_conv, meta_tokens,
           rel_bias, norm1_g, w_in, lambda_q1, lambda_k1, lambda_q2, lambda_k2, subln_g, conv_w, conv_b,
           conv_ln_g, conv_ln_b, w_o, norm2_g, ffn_w_up, ffn_conv_w, ffn_conv_b, ffn_w_down,
           final_norm_g):
    depth = w_in.shape[0]
    assert depth == 1, "single-layer step only"
    b, s, d = x_prompt.shape
    bd, t_dec = x_sample.shape[:2]
    assert t_dec == 1
    n_meta = meta_tokens.shape[0]
    n_pool, page, n_heads, e = cache_k.shape[1:]
    dv = cache_v.shape[-1]
    dk = e // 2
    d_attn = n_heads * dv
    d_conv = conv_w.shape[-1]
    cw_len = conv_w.shape[1]
    fw_len = ffn_conv_w.shape[1]
    d_ff = ffn_w_down.shape[1]
    n_buckets = rel_bias.shape[0]
    t = ATTN_T
    tm = ROW_TILE
    tf = FF_TILE
    assert e == dv == LANES and d_conv == d_attn and w_in.shape[-1] == 5 * d_attn
    assert s % tm == 0 and s % t == 0 and n_meta <= t and page_table.shape[1] % DEC_PAGES == 0
    assert s % FF_ROWS == 0 and (b * s) % IN_ROWS == 0
    me = n_buckets // 2
    sat = math.ceil(me * (MAX_DISTANCE / me) ** ((n_buckets - 1 - me) / (n_buckets - me))) + 1
    assert t + 1 >= sat and page + 1 >= sat
    halo = SUBLANES * ((cw_len - 1 + SUBLANES - 1) // SUBLANES)
    assert n_meta <= halo and tm % halo == 0 and tm % CONV_ROWS == 0 and fw_len - 1 <= SUBLANES

    ffp = tf * ((d_ff + tf - 1) // tf)
    w_in_b = w_in[0].astype(BF16)
    w_o_b = w_o[0].astype(BF16)
    pad_c = lambda a: jnp.pad(a, ((0, 0), (0, ffp - d_ff)))
    wa_b = pad_c(ffn_w_up[0, :, :d_ff]).astype(BF16)
    wg_b = pad_c(ffn_w_up[0, :, d_ff:]).astype(BF16)
    wd_b = jnp.pad(ffn_w_down[0], ((0, ffp - d_ff), (0, 0))).astype(BF16)
    cwa, cwg = pad_c(ffn_conv_w[0, :, :d_ff]), pad_c(ffn_conv_w[0, :, d_ff:])
    cba, cbg = pad_c(ffn_conv_b[:, :d_ff])[0], pad_c(ffn_conv_b[:, d_ff:])[0]
    lamv = jnp.stack([lambda_q1[0], lambda_k1[0], lambda_q2[0], lambda_k2[0]]).astype(F32)
    q_scale = dk ** -0.5 * LOG2E

    ar = lambda n: jnp.arange(n, dtype=jnp.int32)
    d0 = _bias_tile(rel_bias, ar(t)[:, None] - ar(t)[None, :])
    d1 = _bias_tile(rel_bias, t + ar(t)[:, None] - ar(t)[None, :])
    dm = _bias_tile(rel_bias, n_meta + ar(t)[:, None] - ar(n_meta)[None, :])
    dmeta = _bias_tile(rel_bias, ar(n_meta)[:, None] - ar(n_meta)[None, :])
    stack2 = lambda a: jnp.concatenate([a, a], axis=0)
    same_head = (ar(n_heads)[:, None, None] == ar(n_heads)[None, None, :])
    blast = _bias_tile(rel_bias, (page - ar(page))[None, :])[:, 0, :]
    mb = stack2(jnp.broadcast_to(jnp.where(same_head, 0.0, NEG), (n_heads, page, n_heads))
                ).reshape(2 * n_heads, page * n_heads)
    mbl = stack2(jnp.where(same_head, blast[:, :, None], NEG)).reshape(2 * n_heads, page * n_heads)
    b0 = stack2(_bias_tile(rel_bias, jnp.zeros((1, 1), jnp.int32))[:, 0, :])

    x_small = jnp.concatenate([x_sample[:, 0, :], meta_tokens.astype(F32)], axis=0)
    xn_small = rmsnorm_cast(x_small, norm1_g[0], BF16)
    z = mm_small(xn_small, w_in_b)
    q_small = (z[:, :d_attn] * q_scale).astype(BF16)
    k_small = z[:, d_attn:2 * d_attn]
    v_small = z[:, 2 * d_attn:3 * d_attn]
    state_t = jnp.transpose(state_conv[0], (1, 0, 2))
    u_small, c_small = small_conv_module(z[:, 3 * d_attn:4 * d_attn], z[:, 4 * d_attn:], state_t,
                                         conv_w[0], conv_b[0], conv_ln_g[0], conv_ln_b[0], bd)
    o_meta = meta_attention(q_small[bd:], k_small[bd:], v_small[bd:], dmeta, lamv, subln_g[0], n_heads)
    hd = lambda a: a.reshape(bd, n_heads, -1)
    o_dec = decode_attention(page_table, hd(q_small[:bd]), hd(k_small[:bd]), hd(v_small[:bd]),
                             cache_k[0], cache_v[0], mb, mbl, b0, lamv, subln_g[0], DEC_PAGES)
    o_small = jnp.concatenate([o_dec.reshape(bd, d_attn), o_meta], axis=0)
    h1_small = merge_out(o_small, c_small, x_small, w_o_b, x_small.shape[0])
    xn2_small = rmsnorm_cast(h1_small, norm2_g[0], BF16)
    up_a_small = mm_small(xn2_small, wa_b)
    up_g_small = mm_small(xn2_small, wg_b)
    st = jnp.transpose(state_ffn_conv[0], (1, 0, 2))
    st_pad = lambda a: pad_c(a.reshape(-1, d_ff)).reshape(fw_len - 1, bd, ffp)
    hid = small_gate(up_a_small[:bd], up_g_small[:bd], st_pad(st[:, :, :d_ff]), st_pad(st[:, :, d_ff:]),
                     cwa, cwg, cba, cbg)
    down = mm_small(hid, wd_b)
    y_sample = residual_norm(h1_small[:bd], down, final_norm_g)

    xp = x_prompt.reshape(b * s, d)
    qkv_b, kvu = prompt_inproj(xp, norm1_g[0], w_in_b, d_attn, q_scale, IN_ROWS)
    qkv_b = qkv_b.reshape(3, b, s, d_attn)
    kvu = kvu.reshape(3, b, s, d_attn)
    o_p = prompt_attention(qkv_b, k_small[bd:].astype(BF16), v_small[bd:].astype(BF16),
                           d0, d1, dm, lamv, subln_g[0], n_heads, t)
    head_u = jnp.concatenate([jnp.zeros((halo - n_meta, d_conv), F32), u_small[bd:]], axis=0)
    c_p = prompt_conv_module(kvu, head_u, conv_w[0], conv_b[0], conv_ln_g[0], conv_ln_b[0], tm)
    h1_p = merge_out(o_p.reshape(b * s, -1), c_p.reshape(b * s, -1), xp, w_o_b, tm)
    n_t = fw_len - 1
    head_rows = lambda a: jnp.concatenate([jnp.zeros((SUBLANES - n_t, ffp), F32), a[-n_t:]], axis=0)
    y_p, tail_a, tail_g = prompt_ffn(h1_p, norm2_g[0], final_norm_g, wa_b, wg_b, wd_b, cwa, cwg, cba, cbg,
                                     head_rows(up_a_small[bd:]), head_rows(up_g_small[bd:]), s, FF_ROWS, tf)

    def with_meta(meta_rows, real):
        m_b = jnp.broadcast_to(meta_rows[None], (b,) + meta_rows.shape)
        return jnp.concatenate([m_b, real], axis=1).reshape(1, b, n_meta + s, n_heads, -1)

    k_prompt = with_meta(k_small[bd:], kvu[0])
    v_prompt = with_meta(v_small[bd:], kvu[1])
    conv_prompt = kvu[2][None, :, s - (cw_len - 1):]
    bps = s // FF_ROWS
    last_blk = lambda a: a[bps - 1::bps, SUBLANES - n_t:, :d_ff]
    ffn_prompt = jnp.concatenate([last_blk(tail_a), last_blk(tail_g)], axis=-1)[None]
    k_sample = k_small[:bd].reshape(1, bd, 1, n_heads, e)
    v_sample = v_small[:bd].reshape(1, bd, 1, n_heads, dv)
    conv_sample = jnp.concatenate([state_conv[0][:, 1:], u_small[:bd, None, :]], axis=1)[None]
    up_s = jnp.concatenate([up_a_small[:bd, :d_ff], up_g_small[:bd, :d_ff]], axis=-1)
    ffn_sample = jnp.concatenate([state_ffn_conv[0][:, 1:], up_s[:, None, :]], axis=1)[None]
    return (y_p.reshape(b, s, d), y_sample.reshape(bd, 1, d), k_prompt, v_prompt, conv_prompt, ffn_prompt,
            k_sample, v_sample, conv_sample, ffn_sample)
```

```python
import functools
import math

import jax
import jax.numpy as jnp
from jax import lax
from jax.experimental import pallas as pl
from jax.experimental.pallas import tpu as pltpu

F32 = jnp.float32
BF16 = jnp.bfloat16

EPS = 1e-6
MAX_DISTANCE = 128
LAMBDA_INIT_0 = 0.8 - 0.6 * math.exp(-0.3 * 0)
NEG = -1e30
LOG2E = math.log2(math.e)

LANES = 128
SUBLANES = 8
VMEM_LIMIT = 56 * 1024 * 1024

ATTN_T = 512
ATTN_HEADS = 2
ROW_TILE = 512
IN_ROWS = 1024
FF_ROWS = 1024
FF_TILE = 512
FF_SUB = 2
CONV_ROWS = 64
DEC_PAGES = 16


def _cparams(*sem):
    return pltpu.CompilerParams(dimension_semantics=sem, vmem_limit_bytes=VMEM_LIMIT)


def _rms(x, g):
    return x * lax.rsqrt(jnp.mean(x * x, axis=-1, keepdims=True) + EPS) * g


def _layernorm(y, g, b):
    mu = jnp.mean(y, axis=-1, keepdims=True)
    d = y - mu
    var = jnp.mean(d * d, axis=-1, keepdims=True)
    return d * lax.rsqrt(var + EPS) * g + b


def _silu(x):
    return x * jax.nn.sigmoid(x)


def _lambda(lamv_ref):
    a = jnp.sum(lamv_ref[0:1, :] * lamv_ref[1:2, :], axis=-1, keepdims=True)
    b = jnp.sum(lamv_ref[2:3, :] * lamv_ref[3:4, :], axis=-1, keepdims=True)
    return jnp.exp(a) - jnp.exp(b) + LAMBDA_INIT_0


def _t5_bucket(rel, n_buckets):
    n = jnp.maximum(rel, 0)
    max_exact = n_buckets // 2
    nf = jnp.maximum(n, 1).astype(F32)
    large = max_exact + (jnp.log(nf / max_exact) / math.log(MAX_DISTANCE / max_exact)
                         * (n_buckets - max_exact)).astype(jnp.int32)
    large = jnp.minimum(large, n_buckets - 1)
    return jnp.where(n < max_exact, n, large)


def _bias_tile(rel_bias, rel):
    nb = rel_bias.shape[0]
    onehot = (_t5_bucket(rel, nb)[..., None] == jnp.arange(nb, dtype=jnp.int32)).astype(F32)
    table = (rel_bias - rel_bias[nb - 1]).astype(F32) * LOG2E
    b = jnp.einsum("rcn,nh->hrc", onehot, table, precision=lax.Precision.HIGHEST)
    return jnp.where((rel >= 0)[None], b, NEG)


def _rmsnorm_cast_kernel(x_ref, g_ref, o_ref):
    o_ref[...] = _rms(x_ref[...], g_ref[...]).astype(o_ref.dtype)


def rmsnorm_cast(x, g, dtype):
    return pl.pallas_call(
        _rmsnorm_cast_kernel,
        out_shape=jax.ShapeDtypeStruct(x.shape, dtype),
        name="rmsnorm_cast",
    )(x, g.reshape(1, -1))


def _mm_cast_kernel(x_ref, w_ref, o_ref, wb_ref):
    k = w_ref.shape[0]
    wb = w_ref[...].astype(BF16)
    wb_ref[0:k, :] = wb
    if wb_ref.shape[0] > k:
        wb_ref[k:, :] = jnp.zeros((wb_ref.shape[0] - k, wb_ref.shape[1]), BF16)
    o_ref[...] = jnp.dot(x_ref[:, 0:k], wb, preferred_element_type=F32)


def mm_cast(x, w, tn, k_pad=None):
    m, kx = x.shape
    k, n = w.shape
    kp = k if k_pad is None else k_pad
    return pl.pallas_call(
        _mm_cast_kernel,
        grid=(n // tn,),
        in_specs=[pl.BlockSpec((m, kx), lambda j: (0, 0)),
                  pl.BlockSpec((k, tn), lambda j: (0, j))],
        out_specs=[pl.BlockSpec((m, tn), lambda j: (0, j)), pl.BlockSpec((kp, tn), lambda j: (0, j))],
        out_shape=[jax.ShapeDtypeStruct((m, n), F32), jax.ShapeDtypeStruct((kp, n), BF16)],
        compiler_params=_cparams("arbitrary"),
        name="mm_cast",
    )(x, w)


def _up_cast_kernel(x_ref, w_ref, o_ref, wb_ref, *, n_real):
    wb = jnp.where(pl.program_id(1) < n_real, w_ref[...], 0.0).astype(BF16)
    wb_ref[...] = wb
    o_ref[...] = jnp.dot(x_ref[...], wb, preferred_element_type=F32)


def up_cast(x, w_up, d_ff, ffp):
    m, d = x.shape
    tn = math.gcd(d_ff, ffp)
    n_real, n_blk = d_ff // tn, ffp // tn
    return pl.pallas_call(
        functools.partial(_up_cast_kernel, n_real=n_real),
        grid=(2, n_blk),
        in_specs=[pl.BlockSpec((m, d), lambda hf, j: (0, 0)),
                  pl.BlockSpec((d, tn), lambda hf, j: (0, hf * n_real + jnp.minimum(j, n_real - 1)))],
        out_specs=[pl.BlockSpec((None, m, tn), lambda hf, j: (hf, 0, j)),
                   pl.BlockSpec((None, d, tn), lambda hf, j: (hf, 0, j))],
        out_shape=[jax.ShapeDtypeStruct((2, m, ffp), F32), jax.ShapeDtypeStruct((2, d, ffp), BF16)],
        compiler_params=_cparams("arbitrary", "arbitrary"),
        name="up_cast",
    )(x, w_up)


def _add_rmsnorm_kernel(x_ref, d_ref, g_ref, h_ref, xn_ref):
    h = x_ref[...] + d_ref[...]
    h_ref[...] = h
    xn_ref[...] = _rms(h, g_ref[...]).astype(xn_ref.dtype)


def add_rmsnorm(x, delta, g):
    return pl.pallas_call(
        _add_rmsnorm_kernel,
        out_shape=[jax.ShapeDtypeStruct(x.shape, F32), jax.ShapeDtypeStruct(x.shape, BF16)],
        name="add_rmsnorm",
    )(x, delta, g.reshape(1, -1))


def _inproj_kernel(x_ref, g_ref, w_ref, wg_ref, ob_ref, of_ref, xn_ref, *, q_scale):
    j = pl.program_id(1)

    @pl.when(j == 0)
    def _():
        xn_ref[...] = _rms(x_ref[...], g_ref[...]).astype(BF16)

    z = jnp.dot(xn_ref[...], w_ref[...], preferred_element_type=F32)

    @pl.when(j == 0)
    def _():
        ob_ref[...] = (z * q_scale).astype(BF16)

    @pl.when((j == 1) | (j == 2))
    def _():
        of_ref[...] = z
        ob_ref[...] = z.astype(BF16)

    @pl.when(j == 3)
    def _():
        gate = jnp.dot(xn_ref[...], wg_ref[...], preferred_element_type=F32)
        of_ref[...] = z * jax.nn.sigmoid(gate)


def prompt_inproj(x, g, w_in, d_attn, q_scale, tm):
    m, d = x.shape
    c = d_attn
    return pl.pallas_call(
        functools.partial(_inproj_kernel, q_scale=q_scale),
        grid=(m // tm, 4),
        in_specs=[pl.BlockSpec((tm, d), lambda i, j: (i, 0)),
                  pl.BlockSpec((1, d), lambda i, j: (0, 0)),
                  pl.BlockSpec((d, c), lambda i, j: (0, j)),
                  pl.BlockSpec((d, c), lambda i, j: (0, 4), pipeline_mode=pl.Buffered(1))],
        out_specs=[pl.BlockSpec((None, tm, c), lambda i, j: (jnp.minimum(j, 2), i, 0)),
                   pl.BlockSpec((None, tm, c), lambda i, j: (jnp.maximum(j - 1, 0), i, 0))],
        out_shape=[jax.ShapeDtypeStruct((3, m, c), BF16), jax.ShapeDtypeStruct((3, m, c), F32)],
        scratch_shapes=[pltpu.VMEM((tm, d), BF16)],
        compiler_params=_cparams("arbitrary", "arbitrary"),
        name="prompt_inproj",
    )(x, g.reshape(1, -1), w_in, w_in)


def _stack_q(q, dk):
    lane = lax.broadcasted_iota(jnp.int32, q.shape, 1)
    zero = jnp.zeros_like(q)
    return jnp.concatenate([jnp.where(lane < dk, q, zero), jnp.where(lane >= dk, q, zero)], axis=0)


def _scores(qq, kblk):
    return lax.dot_general(qq, kblk, (((1,), (1,)), ((), ())), preferred_element_type=F32)


def _add_tile(s, tile):
    t, c = tile.shape
    return (s.reshape(2, t, c) + tile[None]).reshape(2 * t, c)


def _with_ones(v):
    return jnp.concatenate([v, jnp.ones_like(v)], axis=-1)


def _softmax_update(s, vx, carry):
    return _softmax_update_with_max(s, jnp.max(s, axis=-1, keepdims=True), vx, carry)


def _softmax_update_with_max(s, s_max, vx, carry):
    m, acc = carry
    m_new = jnp.maximum(m, s_max)
    alpha = jnp.exp2(m - m_new)
    p = jnp.exp2(s - m_new).astype(BF16)
    return m_new, alpha * acc + jnp.dot(p, vx, preferred_element_type=F32)


def _combine_heads_out(acc, lam, g, t):
    dv = acc.shape[-1] // 2
    o = acc[:t, :dv] / acc[:t, dv:] - lam * (acc[t:, :dv] / acc[t:, dv:])
    return _rms(o, g) * (1.0 - LAMBDA_INIT_0)


def _prompt_attn_kernel(q_ref, k_ref, v_ref, km_ref, vm_ref, d0_ref, d1_ref, dm_ref,
                        lamv_ref, g_ref, o_ref, vx_ref, *, t, dk, n_grp):
    qi = pl.program_id(2)
    r = 2 * t
    e = 2 * dk
    dv = v_ref.shape[-1] // n_grp
    heads = range(n_grp)
    hs = lambda hh, w: slice(hh * w, (hh + 1) * w)

    @pl.when(qi == 0)
    def _():
        for hh in heads:
            vx_ref[hh, :, :dv] = v_ref[:, hs(hh, dv)]
            vx_ref[hh, :, dv:] = jnp.ones((v_ref.shape[0], dv), BF16)

    qq = [_stack_q(q_ref[:, hs(hh, e)], dk) for hh in heads]
    rows = lambda j: pl.ds(pl.multiple_of(j * t, t), t)
    kblk = lambda hh, j: k_ref[rows(j), hs(hh, e)]
    vxblk = lambda hh, j: vx_ref[hh, rows(j), :]

    n_far = jnp.maximum(qi - 1, 0)

    def far_body(j, carry):
        out = []
        for hh in heads:
            s_cur, m, acc = carry[hh]
            s_next = _scores(qq[hh], kblk(hh, j + 1))
            m, acc = _softmax_update(s_cur, vxblk(hh, j), (m, acc))
            out.append((s_next, m, acc))
        return tuple(out)

    init = tuple((_scores(qq[hh], kblk(hh, 0)), jnp.full((r, 1), NEG, F32), jnp.zeros((r, 2 * dv), F32))
                 for hh in heads)
    carry = lax.fori_loop(0, n_far, far_body, init)

    lam = _lambda(lamv_ref)
    for hh in heads:
        s_prev, m, acc = carry[hh]
        s_sub = _add_tile(s_prev, jnp.where(qi >= 1, d1_ref[hh], NEG))
        s_diag = _add_tile(_scores(qq[hh], kblk(hh, qi)), d0_ref[hh])
        s_meta = _add_tile(_scores(qq[hh], km_ref[:, hs(hh, e)]), jnp.where(qi == 0, dm_ref[hh], 0.0))
        c = _softmax_update(s_meta, _with_ones(vm_ref[:, hs(hh, dv)]), (m, acc))
        c = _softmax_update(s_diag, vxblk(hh, qi), c)
        m, acc = _softmax_update(s_sub, vxblk(hh, n_far), c)
        o_ref[:, hs(hh, dv)] = _combine_heads_out(acc, lam, g_ref[...], t).astype(o_ref.dtype)


def prompt_attention(qkv, kmb, vmb, d0, d1, dm, lamv, subln_g, n_heads, t, n_grp):
    _, b, s, width = qkv.shape
    e = dv = width // n_heads
    nm = kmb.shape[0]
    ge, gv = n_grp * e, n_grp * dv
    tile = lambda c: pl.BlockSpec((n_grp, t, c), lambda bi, h, qi: (h, 0, 0))
    return pl.pallas_call(
        functools.partial(_prompt_attn_kernel, t=t, dk=e // 2, n_grp=n_grp),
        grid=(b, n_heads // n_grp, s // t),
        in_specs=[
            pl.BlockSpec((None, None, t, ge), lambda bi, h, qi: (0, bi, qi, h)),
            pl.BlockSpec((None, None, s, ge), lambda bi, h, qi: (1, bi, 0, h)),
            pl.BlockSpec((None, None, s, gv), lambda bi, h, qi: (2, bi, 0, h)),
            pl.BlockSpec((nm, ge), lambda bi, h, qi: (0, h)),
            pl.BlockSpec((nm, gv), lambda bi, h, qi: (0, h)),
            tile(t), tile(t), tile(nm),
            pl.BlockSpec((4, e // 2), lambda bi, h, qi: (0, 0)),
            pl.BlockSpec((1, dv), lambda bi, h, qi: (0, 0)),
        ],
        out_specs=pl.BlockSpec((None, t, gv), lambda bi, h, qi: (bi, qi, h)),
        out_shape=jax.ShapeDtypeStruct((b, s, n_heads * dv), BF16),
        scratch_shapes=[pltpu.VMEM((n_grp, s, 2 * dv), BF16)],
        compiler_params=_cparams("arbitrary", "arbitrary", "arbitrary"),
        name="prompt_attention",
    )(qkv, qkv, qkv, kmb, vmb, d0, d1, dm, lamv, subln_g.reshape(1, -1))


def _meta_attn_kernel(q_ref, k_ref, v_ref, d_ref, lamv_ref, g_ref, o_ref, *, n_heads, dk):
    t = q_ref.shape[0]
    e = 2 * dk
    dv = v_ref.shape[-1] // n_heads
    lam = _lambda(lamv_ref)
    for h in range(n_heads):
        qq = _stack_q(q_ref[:, h * e:(h + 1) * e], dk)
        kblk = k_ref[:, h * e:(h + 1) * e].astype(BF16)
        vx = _with_ones(v_ref[:, h * dv:(h + 1) * dv].astype(BF16))
        carry = (jnp.full((2 * t, 1), NEG, F32), jnp.zeros((2 * t, 2 * dv), F32))
        m, acc = _softmax_update(_add_tile(_scores(qq, kblk), d_ref[h]), vx, carry)
        o_ref[:, h * dv:(h + 1) * dv] = _combine_heads_out(acc, lam, g_ref[...], t).astype(o_ref.dtype)


def meta_attention(q, k, v, dmeta, lamv, subln_g, n_heads):
    t = q.shape[0]
    dv = v.shape[-1] // n_heads
    return pl.pallas_call(
        functools.partial(_meta_attn_kernel, n_heads=n_heads, dk=k.shape[-1] // n_heads // 2),
        out_shape=jax.ShapeDtypeStruct((t, n_heads * dv), BF16),
        name="meta_attention",
    )(q, k, v, dmeta, lamv, subln_g.reshape(1, -1))


def _decode_attn_kernel(pt_ref, q_ref, kn_ref, vn_ref, mb_ref, mbl_ref, b0_ref, lamv_ref, g_ref, *rest,
                        n_pages_step, n_heads, dk):
    k_refs = rest[:n_pages_step]
    v_refs = rest[n_pages_step:2 * n_pages_step]
    o_ref = rest[2 * n_pages_step]
    qq_ref, m_ref, l_ref, acc_ref = rest[2 * n_pages_step + 1:]
    del pt_ref
    g = pl.program_id(1)
    n_g = pl.num_programs(1)
    e = 2 * dk

    @pl.when(g == 0)
    def _():
        qf = q_ref[...].astype(F32)
        lane = lax.broadcasted_iota(jnp.int32, qf.shape, 1)
        qq = jnp.concatenate([jnp.where(lane < dk, qf, 0.0), jnp.where(lane >= dk, qf, 0.0)], axis=0)
        qq_ref[...] = qq.astype(BF16)
        kn = jnp.concatenate([kn_ref[...], kn_ref[...]], axis=0)
        s_new = jnp.sum(qq * kn, axis=-1, keepdims=True) + b0_ref[...]
        m_ref[...] = s_new
        l_ref[...] = jnp.ones_like(s_new)
        acc_ref[...] = jnp.concatenate([vn_ref[...], vn_ref[...]], axis=0)

    qq = qq_ref[...]
    s_pages = []
    for p in range(n_pages_step):
        bias = mb_ref[...]
        if p == n_pages_step - 1:
            bias = jnp.where(g == n_g - 1, mbl_ref[...], bias)
        rows = k_refs[p].shape[0] * n_heads
        s_pages.append(_scores(qq, k_refs[p][...].reshape(rows, e).astype(BF16)) + bias)
    m_old = m_ref[...]
    m_loc = s_pages[0]
    for s in s_pages[1:]:
        m_loc = jnp.maximum(m_loc, s)
    m_new = jnp.maximum(m_old, jnp.max(m_loc, axis=-1, keepdims=True))
    alpha = jnp.exp2(m_old - m_new)
    l_sum = None
    pv = None
    for p in range(n_pages_step):
        w = jnp.exp2(s_pages[p] - m_new)
        rows = v_refs[p].shape[0] * n_heads
        vblk = v_refs[p][...].reshape(rows, v_refs[p].shape[-1]).astype(BF16)
        d = jnp.dot(w.astype(BF16), vblk, preferred_element_type=F32)
        l_sum = w if l_sum is None else l_sum + w
        pv = d if pv is None else pv + d
    l_new = alpha * l_ref[...] + jnp.sum(l_sum, axis=-1, keepdims=True)
    acc_new = alpha * acc_ref[...] + pv
    m_ref[...] = m_new
    l_ref[...] = l_new
    acc_ref[...] = acc_new

    @pl.when(g == n_g - 1)
    def _():
        o = (acc_new[:n_heads] / l_new[:n_heads]
             - _lambda(lamv_ref) * (acc_new[n_heads:] / l_new[n_heads:]))
        o_ref[...] = (_rms(o, g_ref[...]) * (1.0 - LAMBDA_INIT_0)).astype(o_ref.dtype)


def decode_attention(page_table, q, k_new, v_new, cache_k, cache_v, mb, mbl, b0, lamv, subln_g,
                     pages_per_step):
    bd, n_pages = page_table.shape
    _, page, n_heads, e = cache_k.shape
    dv = cache_v.shape[-1]
    p_step = pages_per_step
    r = 2 * n_heads
    seq_blk = lambda w: pl.BlockSpec((None, n_heads, w), lambda b, g, pt: (b, 0, 0))
    const2 = lambda b, g, pt: (0, 0)

    def page_spec(p, w):
        return pl.BlockSpec((None, page, n_heads, w), lambda b, g, pt: (pt[b, g * p_step + p], 0, 0, 0))

    grid_spec = pltpu.PrefetchScalarGridSpec(
        num_scalar_prefetch=1,
        grid=(bd, n_pages // p_step),
        in_specs=[seq_blk(e), seq_blk(e), seq_blk(dv),
                  pl.BlockSpec((r, page * n_heads), const2),
                  pl.BlockSpec((r, page * n_heads), const2),
                  pl.BlockSpec((r, 1), const2),
                  pl.BlockSpec((4, e // 2), const2),
                  pl.BlockSpec((1, dv), const2)]
                 + [page_spec(p, e) for p in range(p_step)]
                 + [page_spec(p, dv) for p in range(p_step)],
        out_specs=seq_blk(dv),
        scratch_shapes=[pltpu.VMEM((r, e), BF16), pltpu.VMEM((r, 1), F32),
                        pltpu.VMEM((r, 1), F32), pltpu.VMEM((r, dv), F32)],
    )
    return pl.pallas_call(
        functools.partial(_decode_attn_kernel, n_pages_step=p_step, n_heads=n_heads, dk=e // 2),
        grid_spec=grid_spec,
        out_shape=jax.ShapeDtypeStruct((bd, n_heads, dv), BF16),
        compiler_params=_cparams("arbitrary", "arbitrary"),
        name="decode_attention",
    )(page_table, q, k_new, v_new, mb, mbl, b0, lamv, subln_g.reshape(1, -1),
      *([cache_k] * p_step), *([cache_v] * p_step))


def _conv_rows(src_ref, base, n_rows, cw_ref, conv_w):
    acc = src_ref[pl.ds(base, n_rows), :] * cw_ref[0:1, :]
    for w in range(1, conv_w):
        acc = acc + src_ref[pl.ds(base + w, n_rows), :] * cw_ref[w:w + 1, :]
    return acc


def _conv_tiles(win, cwb_ref, lanes, conv_w, first, n_out):
    y = None
    for k in range(SUBLANES):
        taps = [w for w in range(conv_w) if (first + w) % SUBLANES == k]
        if not taps:
            continue
        n_p = n_out if k == 0 else n_out + 1
        part = None
        for w in taps:
            j = (first + w) // SUBLANES
            term = win[j:j + n_p] * cwb_ref[w, :, lanes][None]
            part = term if part is None else part + term
        part = part.reshape(n_p * SUBLANES, part.shape[-1])
        part = part if k == 0 else part[k:k + n_out * SUBLANES]
        y = part if y is None else y + part
    return y


def _prompt_conv_kernel(u_ref, uh_ref, head_ref, cwb_ref, cb_ref, lg_ref, lb_ref, c_ref, s_ref, y_ref,
                        *, conv_w, halo, rows):
    i = pl.program_id(1)
    tm, c = u_ref.shape
    s_ref[0:halo, :] = jnp.where(i == 0, head_ref[...], uh_ref[...])
    s_ref[halo:, :] = u_ref[...]
    first = halo - (conv_w - 1)
    n_win = (rows + halo) // SUBLANES

    def chunk(ci, _):
        r0 = pl.multiple_of(ci * rows, rows)
        for lb in range(c // LANES):
            lanes = slice(lb * LANES, (lb + 1) * LANES)
            win = s_ref[pl.ds(r0, rows + halo), lanes].reshape(n_win, SUBLANES, LANES)
            y_ref[:, lanes] = _conv_tiles(win, cwb_ref, lanes, conv_w, first, rows // SUBLANES)
        y = y_ref[...] + cb_ref[...]
        c_ref[pl.ds(r0, rows), :] = _silu(_layernorm(y, lg_ref[...], lb_ref[...])).astype(c_ref.dtype)
        return 0

    lax.fori_loop(0, tm // rows, chunk, 0)


def prompt_conv_module(kvu, head, conv_w, conv_b, ln_g, ln_b, tm):
    _, b, s, c = kvu.shape
    halo = head.shape[0]
    w = conv_w.shape[0]
    per = tm // halo
    vec = lambda a: a.reshape(1, -1)
    const = lambda bi, i: (0, 0)
    cwb = jnp.broadcast_to(conv_w[:, None, :], (w, SUBLANES, c))
    return pl.pallas_call(
        functools.partial(_prompt_conv_kernel, conv_w=w, halo=halo, rows=CONV_ROWS),
        grid=(b, s // tm),
        in_specs=[pl.BlockSpec((None, None, tm, c), lambda bi, i: (2, bi, i, 0)),
                  pl.BlockSpec((None, None, halo, c), lambda bi, i: (2, bi, jnp.maximum(i * per - 1, 0), 0)),
                  pl.BlockSpec((halo, c), const),
                  pl.BlockSpec((w, SUBLANES, c), lambda bi, i: (0, 0, 0)),
                  pl.BlockSpec((1, c), const), pl.BlockSpec((1, c), const), pl.BlockSpec((1, c), const)],
        out_specs=pl.BlockSpec((None, tm, c), lambda bi, i: (bi, i, 0)),
        out_shape=jax.ShapeDtypeStruct((b, s, c), BF16),
        scratch_shapes=[pltpu.VMEM((tm + halo, c), F32), pltpu.VMEM((CONV_ROWS, c), F32)],
        compiler_params=_cparams("arbitrary", "arbitrary"),
        name="prompt_conv_module",
    )(kvu, kvu, head, cwb, vec(conv_b), vec(ln_g), vec(ln_b))


def _small_conv_kernel(za_ref, zg_ref, st_ref, cw_ref, cb_ref, lg_ref, lb_ref, u_ref, c_ref, s_ref,
                       *, conv_w, n_sample):
    n_meta = za_ref.shape[0] - n_sample
    u = za_ref[...] * jax.nn.sigmoid(zg_ref[...])
    u_ref[...] = u
    y = u[:n_sample] * cw_ref[conv_w - 1:conv_w, :]
    for w in range(conv_w - 1):
        y = y + st_ref[w] * cw_ref[w:w + 1, :]
    y_s = y + cb_ref[...]
    hist = s_ref.shape[0] - n_meta
    s_ref[0:hist, :] = jnp.zeros((hist, s_ref.shape[1]), F32)
    s_ref[hist:, :] = u[n_sample:]
    y_m = _conv_rows(s_ref, hist - (conv_w - 1), n_meta, cw_ref, conv_w) + cb_ref[...]
    yy = jnp.concatenate([y_s, y_m], axis=0)
    c_ref[...] = _silu(_layernorm(yy, lg_ref[...], lb_ref[...])).astype(c_ref.dtype)


def small_conv_module(za, zg, state_t, conv_w, conv_b, ln_g, ln_b, n_sample):
    m, c = za.shape
    w = conv_w.shape[0]
    n_meta = m - n_sample
    hist = SUBLANES * ((w - 1 + SUBLANES - 1) // SUBLANES)
    vec = lambda a: a.reshape(1, -1)
    return pl.pallas_call(
        functools.partial(_small_conv_kernel, conv_w=w, n_sample=n_sample),
        out_shape=[jax.ShapeDtypeStruct((m, c), F32), jax.ShapeDtypeStruct((m, c), BF16)],
        scratch_shapes=[pltpu.VMEM((hist + n_meta, c), F32)],
        name="small_conv_module",
    )(za, zg, state_t, conv_w, vec(conv_b), vec(ln_g), vec(ln_b))


def _merge_kernel(o_ref, c_ref, x_ref, w_ref, y_ref):
    da = o_ref.shape[-1]
    y = x_ref[...] + jnp.dot(o_ref[...], w_ref[0:da, :], preferred_element_type=F32)
    y_ref[...] = y + jnp.dot(c_ref[...], w_ref[da:, :], preferred_element_type=F32)


def merge_out(o_n, c, x, w_o, tm):
    m, d = x.shape
    da = o_n.shape[-1]
    row = lambda i: (i, 0)
    return pl.pallas_call(
        _merge_kernel,
        grid=(m // tm,),
        in_specs=[pl.BlockSpec((tm, da), row), pl.BlockSpec((tm, d - da), row),
                  pl.BlockSpec((tm, d), row), pl.BlockSpec((d, d), lambda i: (0, 0))],
        out_specs=pl.BlockSpec((tm, d), row),
        out_shape=jax.ShapeDtypeStruct((m, d), F32),
        compiler_params=_cparams("arbitrary"),
        name="merge_out",
    )(o_n, c, x, w_o)


def _gelu(x):
    return 0.5 * x * (1.0 + lax.erf(x * math.sqrt(0.5)))


def _ffn_kernel(x_ref, g2_ref, gf_ref, wa_ref, wg_ref, wd_ref, cwa_ref, cwg_ref, cba_ref, cbg_ref,
                ha_ref, hg_ref, y_ref, ta_ref, tg_ref,
                xn_ref, ea_ref, eg_ref, ca_ref, cg_ref, *, blocks_per_seq, conv_w, n_sub):
    i = pl.program_id(0)
    f = pl.program_id(1)
    n_f = pl.num_programs(1)
    tm = x_ref.shape[0]
    tf = wa_ref.shape[1]
    ts = tf // n_sub
    pad = SUBLANES

    @pl.when(f == 0)
    def _():
        x = x_ref[...]
        xn_ref[...] = _rms(x, g2_ref[...]).astype(BF16)
        y_ref[...] = x

    xn = xn_ref[...]
    first = (i % blocks_per_seq) == 0
    base = pad - (conv_w - 1)
    down = None
    for sc in range(n_sub):
        cols = slice(sc * ts, (sc + 1) * ts)
        up_a = jnp.dot(xn, wa_ref[:, cols], preferred_element_type=F32)
        up_g = jnp.dot(xn, wg_ref[:, cols], preferred_element_type=F32)
        ta_ref[:, cols] = up_a[tm - pad:]
        tg_ref[:, cols] = up_g[tm - pad:]
        ea_ref[sc, 0:pad, :] = jnp.where(first, ha_ref[:, cols], ca_ref[f, :, cols])
        eg_ref[sc, 0:pad, :] = jnp.where(first, hg_ref[:, cols], cg_ref[f, :, cols])
        ea_ref[sc, pad:, :] = up_a
        eg_ref[sc, pad:, :] = up_g
        ca_ref[f, :, cols] = up_a[tm - pad:]
        cg_ref[f, :, cols] = up_g[tm - pad:]
        conv_a = _conv_rows(ea_ref.at[sc], base, tm, cwa_ref.at[:, cols], conv_w) + cba_ref[:, cols]
        conv_g = _conv_rows(eg_ref.at[sc], base, tm, cwg_ref.at[:, cols], conv_w) + cbg_ref[:, cols]
        hidden = (_gelu(conv_g) * conv_a).astype(BF16)
        d = jnp.dot(hidden, wd_ref[cols, :], preferred_element_type=F32)
        down = d if down is None else down + d
    y_ref[...] += down

    @pl.when(f == n_f - 1)
    def _():
        y_ref[...] = _rms(y_ref[...], gf_ref[...])


def prompt_ffn(x, norm2_g, final_g, wab, wd, cwa, cwg, cba, cbg, head_a, head_g, rows_per_seq,
               tm, tf):
    m, d = x.shape
    ffp = wab.shape[-1]
    n_f = ffp // tf
    bps = rows_per_seq // tm
    w = cwa.shape[0]
    ts = tf // FF_SUB
    row = lambda i, f: (i, 0)
    const = lambda i, f: (0, 0)
    col = lambda i, f: (0, f)
    tail = pl.BlockSpec((None, SUBLANES, tf), lambda i, f: (i, 0, f))
    return pl.pallas_call(
        functools.partial(_ffn_kernel, blocks_per_seq=bps, conv_w=w, n_sub=FF_SUB),
        grid=(m // tm, n_f),
        in_specs=[pl.BlockSpec((tm, d), row, pipeline_mode=pl.Buffered(1)),
                  pl.BlockSpec((1, d), const), pl.BlockSpec((1, d), const),
                  pl.BlockSpec((None, d, tf), lambda i, f: (0, 0, f)),
                  pl.BlockSpec((None, d, tf), lambda i, f: (1, 0, f)),
                  pl.BlockSpec((tf, d), lambda i, f: (f, 0)),
                  pl.BlockSpec((w, tf), col), pl.BlockSpec((w, tf), col),
                  pl.BlockSpec((1, tf), col), pl.BlockSpec((1, tf), col),
                  pl.BlockSpec((SUBLANES, tf), col), pl.BlockSpec((SUBLANES, tf), col)],
        out_specs=[pl.BlockSpec((tm, d), row), tail, tail],
        out_shape=[jax.ShapeDtypeStruct((m, d), F32),
                   jax.ShapeDtypeStruct((m // tm, SUBLANES, ffp), F32),
                   jax.ShapeDtypeStruct((m // tm, SUBLANES, ffp), F32)],
        scratch_shapes=[pltpu.VMEM((tm, d), BF16),
                        pltpu.VMEM((FF_SUB, tm + SUBLANES, ts), F32),
                        pltpu.VMEM((FF_SUB, tm + SUBLANES, ts), F32),
                        pltpu.VMEM((n_f, SUBLANES, tf), F32), pltpu.VMEM((n_f, SUBLANES, tf), F32)],
        compiler_params=_cparams("arbitrary", "arbitrary"),
        name="prompt_ffn",
    )(x, norm2_g.reshape(1, -1), final_g.reshape(1, -1), wab, wab, wd, cwa, cwg,
      cba.reshape(1, -1), cbg.reshape(1, -1), head_a, head_g)


def _small_gate_kernel(ua_ref, ug_ref, sa_ref, sg_ref, cwa_ref, cwg_ref, cba_ref, cbg_ref, h_ref,
                       *, conv_w):
    def conv(u_ref, s_ref, cw_ref, cb_ref):
        y = u_ref[...] * cw_ref[conv_w - 1:conv_w, :] + cb_ref[...]
        for w in range(conv_w - 1):
            y = y + s_ref[w] * cw_ref[w:w + 1, :]
        return y

    a = conv(ua_ref, sa_ref, cwa_ref, cba_ref)
    g = conv(ug_ref, sg_ref, cwg_ref, cbg_ref)
    h_ref[...] = (_gelu(g) * a).astype(h_ref.dtype)


def small_gate(up_a, up_g, st_a, st_g, cwa, cwg, cba, cbg):
    return pl.pallas_call(
        functools.partial(_small_gate_kernel, conv_w=cwa.shape[0]),
        out_shape=jax.ShapeDtypeStruct(up_a.shape, BF16),
        name="small_gate",
    )(up_a, up_g, st_a, st_g, cwa, cwg, cba.reshape(1, -1), cbg.reshape(1, -1))


def _residual_norm_kernel(x_ref, d_ref, g_ref, y_ref):
    y_ref[...] = _rms(x_ref[...] + d_ref[...], g_ref[...])


def residual_norm(x, delta, g):
    return pl.pallas_call(
        _residual_norm_kernel,
        out_shape=jax.ShapeDtypeStruct(x.shape, F32),
        name="residual_norm",
    )(x, delta, g.reshape(1, -1))


def kernel(x_prompt, x_sample, cache_k, cache_v, page_table, state_conv, state_ffn_conv, meta_tokens,
           rel_bias, norm1_g, w_in, lambda_q1, lambda_k1, lambda_q2, lambda_k2, subln_g, conv_w, conv_b,
           conv_ln_g, conv_ln_b, w_o, norm2_g, ffn_w_up, ffn_conv_w, ffn_conv_b, ffn_w_down,
           final_norm_g):
    depth = w_in.shape[0]
    assert depth == 1, "single-layer step only"
    b, s, d = x_prompt.shape
    bd, t_dec = x_sample.shape[:2]
    assert t_dec == 1
    n_meta = meta_tokens.shape[0]
    n_pool, page, n_heads, e = cache_k.shape[1:]
    dv = cache_v.shape[-1]
    dk = e // 2
    d_attn = n_heads * dv
    d_conv = conv_w.shape[-1]
    cw_len = conv_w.shape[1]
    fw_len = ffn_conv_w.shape[1]
    d_ff = ffn_w_down.shape[1]
    n_buckets = rel_bias.shape[0]
    t = ATTN_T
    tm = ROW_TILE
    tf = FF_TILE
    assert e == dv == LANES and d_conv == d_attn and w_in.shape[-1] == 5 * d_attn
    assert s % tm == 0 and s % t == 0 and n_meta <= t and page_table.shape[1] % DEC_PAGES == 0
    assert s % FF_ROWS == 0 and (b * s) % IN_ROWS == 0
    me = n_buckets // 2
    sat = math.ceil(me * (MAX_DISTANCE / me) ** ((n_buckets - 1 - me) / (n_buckets - me))) + 1
    assert t + 1 >= sat and page + 1 >= sat
    halo = SUBLANES * ((cw_len - 1 + SUBLANES - 1) // SUBLANES)
    assert n_meta <= halo and tm % halo == 0 and tm % CONV_ROWS == 0 and fw_len - 1 <= SUBLANES

    ffp = tf * ((d_ff + tf - 1) // tf)
    pad_c = lambda a: jnp.pad(a, ((0, 0), (0, ffp - d_ff)))
    cwa, cwg = pad_c(ffn_conv_w[0, :, :d_ff]), pad_c(ffn_conv_w[0, :, d_ff:])
    cba, cbg = pad_c(ffn_conv_b[:, :d_ff])[0], pad_c(ffn_conv_b[:, d_ff:])[0]
    lamv = jnp.stack([lambda_q1[0], lambda_k1[0], lambda_q2[0], lambda_k2[0]]).astype(F32)
    q_scale = dk ** -0.5 * LOG2E

    ar = lambda n: jnp.arange(n, dtype=jnp.int32)
    d0 = _bias_tile(rel_bias, ar(t)[:, None] - ar(t)[None, :])
    d1 = _bias_tile(rel_bias, t + ar(t)[:, None] - ar(t)[None, :])
    dm = _bias_tile(rel_bias, n_meta + ar(t)[:, None] - ar(n_meta)[None, :])
    dmeta = _bias_tile(rel_bias, ar(n_meta)[:, None] - ar(n_meta)[None, :])
    stack2 = lambda a: jnp.concatenate([a, a], axis=0)
    same_head = (ar(n_heads)[:, None, None] == ar(n_heads)[None, None, :])
    blast = _bias_tile(rel_bias, (page - ar(page))[None, :])[:, 0, :]
    mb = stack2(jnp.broadcast_to(jnp.where(same_head, 0.0, NEG), (n_heads, page, n_heads))
                ).reshape(2 * n_heads, page * n_heads)
    mbl = stack2(jnp.where(same_head, blast[:, :, None], NEG)).reshape(2 * n_heads, page * n_heads)
    b0 = stack2(_bias_tile(rel_bias, jnp.zeros((1, 1), jnp.int32))[:, 0, :])

    x_small = jnp.concatenate([x_sample[:, 0, :], meta_tokens.astype(F32)], axis=0)
    xn_small = rmsnorm_cast(x_small, norm1_g[0], BF16)
    z, w_in_b = mm_cast(xn_small, w_in[0], 512)
    q_small = (z[:, :d_attn] * q_scale).astype(BF16)
    k_small = z[:, d_attn:2 * d_attn]
    v_small = z[:, 2 * d_attn:3 * d_attn]
    state_t = jnp.transpose(state_conv[0], (1, 0, 2))
    u_small, c_small = small_conv_module(z[:, 3 * d_attn:4 * d_attn], z[:, 4 * d_attn:], state_t,
                                         conv_w[0], conv_b[0], conv_ln_g[0], conv_ln_b[0], bd)
    o_meta = meta_attention(q_small[bd:], k_small[bd:], v_small[bd:], dmeta, lamv, subln_g[0], n_heads)
    hd = lambda a: a.reshape(bd, n_heads, -1)
    o_dec = decode_attention(page_table, hd(q_small[:bd]), hd(k_small[:bd]), hd(v_small[:bd]),
                             cache_k[0], cache_v[0], mb, mbl, b0, lamv, subln_g[0], DEC_PAGES)
    o_small = jnp.concatenate([o_dec.reshape(bd, d_attn), o_meta], axis=0)
    h_delta, w_o_b = mm_cast(jnp.concatenate([o_small, c_small], axis=1), w_o[0], 512)
    h1_small, xn2_small = add_rmsnorm(x_small, h_delta, norm2_g[0])
    up_small, wab_b = up_cast(xn2_small, ffn_w_up[0], d_ff, ffp)
    up_a_small, up_g_small = up_small[0], up_small[1]
    st = jnp.transpose(state_ffn_conv[0], (1, 0, 2))
    st_pad = lambda a: pad_c(a.reshape(-1, d_ff)).reshape(fw_len - 1, bd, ffp)
    hid = small_gate(up_a_small[:bd], up_g_small[:bd], st_pad(st[:, :, :d_ff]), st_pad(st[:, :, d_ff:]),
                     cwa, cwg, cba, cbg)
    down, wd_b = mm_cast(hid, ffn_w_down[0], 256, k_pad=ffp)
    y_sample = residual_norm(h1_small[:bd], down, final_norm_g)

    xp = x_prompt.reshape(b * s, d)
    qkv_b, kvu = prompt_inproj(xp, norm1_g[0], w_in_b, d_attn, q_scale, IN_ROWS)
    qkv_b = qkv_b.reshape(3, b, s, d_attn)
    kvu = kvu.reshape(3, b, s, d_attn)
    o_p = prompt_attention(qkv_b, k_small[bd:].astype(BF16), v_small[bd:].astype(BF16),
                           d0, d1, dm, lamv, subln_g[0], n_heads, t, ATTN_HEADS)
    head_u = jnp.concatenate([jnp.zeros((halo - n_meta, d_conv), F32), u_small[bd:]], axis=0)
    c_p = prompt_conv_module(kvu, head_u, conv_w[0], conv_b[0], conv_ln_g[0], conv_ln_b[0], tm)
    h1_p = merge_out(o_p.reshape(b * s, -1), c_p.reshape(b * s, -1), xp, w_o_b, tm)
    n_t = fw_len - 1
    head_rows = lambda a: jnp.concatenate([jnp.zeros((SUBLANES - n_t, ffp), F32), a[-n_t:]], axis=0)
    y_p, tail_a, tail_g = prompt_ffn(h1_p, norm2_g[0], final_norm_g, wab_b, wd_b, cwa, cwg, cba, cbg,
                                     head_rows(up_a_small[bd:]), head_rows(up_g_small[bd:]), s, FF_ROWS, tf)

    def with_meta(meta_rows, real):
        m_b = jnp.broadcast_to(meta_rows[None], (b,) + meta_rows.shape)
        return jnp.concatenate([m_b, real], axis=1).reshape(1, b, n_meta + s, n_heads, -1)

    k_prompt = with_meta(k_small[bd:], kvu[0])
    v_prompt = with_meta(v_small[bd:], kvu[1])
    conv_prompt = kvu[2][None, :, s - (cw_len - 1):]
    bps = s // FF_ROWS
    last_blk = lambda a: a[bps - 1::bps, SUBLANES - n_t:, :d_ff]
    ffn_prompt = jnp.concatenate([last_blk(tail_a), last_blk(tail_g)], axis=-1)[None]
    k_sample = k_small[:bd].reshape(1, bd, 1, n_heads, e)
    v_sample = v_small[:bd].reshape(1, bd, 1, n_heads, dv)
    conv_sample = jnp.concatenate([state_conv[0][:, 1:], u_small[:bd, None, :]], axis=1)[None]
    up_s = jnp.concatenate([up_a_small[:bd, :d_ff], up_g_small[:bd, :d_ff]], axis=-1)
    ffn_sample = jnp.concatenate([state_ffn_conv[0][:, 1:], up_s[:, None, :]], axis=1)[None]
    return (y_p.reshape(b, s, d), y_sample.reshape(bd, 1, d), k_prompt, v_prompt, conv_prompt, ffn_prompt,
            k_sample, v_sample, conv_sample, ffn_sample)
```

```python
import functools
import math

import jax
import jax.numpy as jnp
from jax import lax
from jax.experimental import pallas as pl
from jax.experimental.pallas import tpu as pltpu

F32 = jnp.float32
BF16 = jnp.bfloat16

EPS = 1e-6
MAX_DISTANCE = 128
LAMBDA_INIT_0 = 0.8 - 0.6 * math.exp(-0.3 * 0)
NEG = -1e30
LOG2E = math.log2(math.e)

LANES = 128
SUBLANES = 8
VMEM_LIMIT = 56 * 1024 * 1024

ATTN_T = 512
ATTN_HEADS = 2
ROW_TILE = 512
IN_ROWS = 1024
FF_ROWS = 1024
FF_TILE = 512
FF_SUB = 2
CONV_ROWS = 64
DEC_PAGES = 16


def _cparams(*sem):
    return pltpu.CompilerParams(dimension_semantics=sem, vmem_limit_bytes=VMEM_LIMIT)


def _rms(x, g):
    return x * lax.rsqrt(jnp.mean(x * x, axis=-1, keepdims=True) + EPS) * g


def _layernorm(y, g, b):
    mu = jnp.mean(y, axis=-1, keepdims=True)
    d = y - mu
    var = jnp.mean(d * d, axis=-1, keepdims=True)
    return d * lax.rsqrt(var + EPS) * g + b


def _silu(x):
    return x * jax.nn.sigmoid(x)


def _lambda(lamv_ref):
    a = jnp.sum(lamv_ref[0:1, :] * lamv_ref[1:2, :], axis=-1, keepdims=True)
    b = jnp.sum(lamv_ref[2:3, :] * lamv_ref[3:4, :], axis=-1, keepdims=True)
    return jnp.exp(a) - jnp.exp(b) + LAMBDA_INIT_0


def _t5_bucket(rel, n_buckets):
    n = jnp.maximum(rel, 0)
    max_exact = n_buckets // 2
    nf = jnp.maximum(n, 1).astype(F32)
    large = max_exact + (jnp.log(nf / max_exact) / math.log(MAX_DISTANCE / max_exact)
                         * (n_buckets - max_exact)).astype(jnp.int32)
    large = jnp.minimum(large, n_buckets - 1)
    return jnp.where(n < max_exact, n, large)


def _bias_tile(rel_bias, rel):
    nb = rel_bias.shape[0]
    onehot = (_t5_bucket(rel, nb)[..., None] == jnp.arange(nb, dtype=jnp.int32)).astype(F32)
    table = (rel_bias - rel_bias[nb - 1]).astype(F32) * LOG2E
    b = jnp.einsum("rcn,nh->hrc", onehot, table, precision=lax.Precision.HIGHEST)
    return jnp.where((rel >= 0)[None], b, NEG)


def _toeplitz(w, t):
    h = w.shape[0]
    u = jnp.pad(w[:, ::-1], ((0, 0), (0, 1)))
    a = jnp.tile(u, (1, t))[:, :t * (2 * t - 1)].reshape(h, t, 2 * t - 1)
    return a[:, :, t - 1:]


def _rmsnorm_cast_kernel(x_ref, g_ref, o_ref):
    o_ref[...] = _rms(x_ref[...], g_ref[...]).astype(o_ref.dtype)


def rmsnorm_cast(x, g, dtype):
    return pl.pallas_call(
        _rmsnorm_cast_kernel,
        out_shape=jax.ShapeDtypeStruct(x.shape, dtype),
        name="rmsnorm_cast",
    )(x, g.reshape(1, -1))


def _mm_cast_kernel(x_ref, w_ref, o_ref, wb_ref):
    k = w_ref.shape[0]
    wb = w_ref[...].astype(BF16)
    wb_ref[0:k, :] = wb
    if wb_ref.shape[0] > k:
        wb_ref[k:, :] = jnp.zeros((wb_ref.shape[0] - k, wb_ref.shape[1]), BF16)
    o_ref[...] = jnp.dot(x_ref[:, 0:k], wb, preferred_element_type=F32)


def mm_cast(x, w, tn, k_pad=None):
    m, kx = x.shape
    k, n = w.shape
    kp = k if k_pad is None else k_pad
    return pl.pallas_call(
        _mm_cast_kernel,
        grid=(n // tn,),
        in_specs=[pl.BlockSpec((m, kx), lambda j: (0, 0)),
                  pl.BlockSpec((k, tn), lambda j: (0, j))],
        out_specs=[pl.BlockSpec((m, tn), lambda j: (0, j)), pl.BlockSpec((kp, tn), lambda j: (0, j))],
        out_shape=[jax.ShapeDtypeStruct((m, n), F32), jax.ShapeDtypeStruct((kp, n), BF16)],
        compiler_params=_cparams("arbitrary"),
        name="mm_cast",
    )(x, w)


def _up_cast_kernel(x_ref, w_ref, o_ref, wb_ref, *, rem):
    hf = pl.program_id(0)
    j = pl.program_id(1)
    last = pl.num_programs(1) - 1
    d, tn = w_ref.shape

    def emit(w):
        wb = w.astype(BF16)
        wb_ref[...] = wb
        o_ref[...] = jnp.dot(x_ref[...], wb, preferred_element_type=F32)

    @pl.when(j < last)
    def _():
        emit(w_ref[...])

    @pl.when((j == last) & (hf == 0))
    def _():
        lane = lax.broadcasted_iota(jnp.int32, (d, tn), 1)
        emit(jnp.where(lane < rem, w_ref[...], 0.0))

    @pl.when((j == last) & (hf == 1))
    def _():
        emit(jnp.concatenate([w_ref[:, tn - rem:], jnp.zeros((d, tn - rem), F32)], axis=1))


def up_cast(x, w_up, d_ff, ffp, tn):
    m, d = x.shape
    n_blk = ffp // tn
    rem = d_ff - (n_blk - 1) * tn
    assert 0 < rem <= tn and rem % LANES == 0 and d_ff >= tn
    el = pl.Element
    return pl.pallas_call(
        functools.partial(_up_cast_kernel, rem=rem),
        grid=(2, n_blk),
        in_specs=[pl.BlockSpec((m, d), lambda hf, j: (0, 0)),
                  pl.BlockSpec((el(d), el(tn)),
                               lambda hf, j: (0, pl.multiple_of(
                                   jnp.minimum(hf * d_ff + j * tn, 2 * d_ff - tn), LANES)))],
        out_specs=[pl.BlockSpec((None, m, tn), lambda hf, j: (hf, 0, j)),
                   pl.BlockSpec((None, d, tn), lambda hf, j: (hf, 0, j))],
        out_shape=[jax.ShapeDtypeStruct((2, m, ffp), F32), jax.ShapeDtypeStruct((2, d, ffp), BF16)],
        compiler_params=_cparams("arbitrary", "arbitrary"),
        name="up_cast",
    )(x, w_up)


def _add_rmsnorm_kernel(x_ref, d_ref, g_ref, h_ref, xn_ref):
    h = x_ref[...] + d_ref[...]
    h_ref[...] = h
    xn_ref[...] = _rms(h, g_ref[...]).astype(xn_ref.dtype)


def add_rmsnorm(x, delta, g):
    return pl.pallas_call(
        _add_rmsnorm_kernel,
        out_shape=[jax.ShapeDtypeStruct(x.shape, F32), jax.ShapeDtypeStruct(x.shape, BF16)],
        name="add_rmsnorm",
    )(x, delta, g.reshape(1, -1))


def _inproj_kernel(x_ref, g_ref, w_ref, wg_ref, ob_ref, of_ref, xn_ref, *, q_scale):
    j = pl.program_id(1)

    @pl.when(j == 0)
    def _():
        xn_ref[...] = _rms(x_ref[...], g_ref[...]).astype(BF16)

    z = jnp.dot(xn_ref[...], w_ref[...], preferred_element_type=F32)

    @pl.when(j == 0)
    def _():
        ob_ref[...] = (z * q_scale).astype(BF16)

    @pl.when((j == 1) | (j == 2))
    def _():
        of_ref[...] = z
        ob_ref[...] = z.astype(BF16)

    @pl.when(j == 3)
    def _():
        gate = jnp.dot(xn_ref[...], wg_ref[...], preferred_element_type=F32)
        of_ref[...] = z * jax.nn.sigmoid(gate)


def prompt_inproj(x, g, w_in, d_attn, q_scale, tm):
    m, d = x.shape
    c = d_attn
    return pl.pallas_call(
        functools.partial(_inproj_kernel, q_scale=q_scale),
        grid=(m // tm, 4),
        in_specs=[pl.BlockSpec((tm, d), lambda i, j: (i, 0)),
                  pl.BlockSpec((1, d), lambda i, j: (0, 0)),
                  pl.BlockSpec((d, c), lambda i, j: (0, j)),
                  pl.BlockSpec((d, c), lambda i, j: (0, 4), pipeline_mode=pl.Buffered(1))],
        out_specs=[pl.BlockSpec((None, tm, c), lambda i, j: (jnp.minimum(j, 2), i, 0)),
                   pl.BlockSpec((None, tm, c), lambda i, j: (jnp.maximum(j - 1, 0), i, 0))],
        out_shape=[jax.ShapeDtypeStruct((3, m, c), BF16), jax.ShapeDtypeStruct((3, m, c), F32)],
        scratch_shapes=[pltpu.VMEM((tm, d), BF16)],
        compiler_params=_cparams("arbitrary", "arbitrary"),
        name="prompt_inproj",
    )(x, g.reshape(1, -1), w_in, w_in)


def _kv_rows_kernel(x_ref, meta_ref, o_ref, *, n_meta, n_heads):
    i = pl.program_id(1)
    n_blk = pl.num_programs(1)
    tm = o_ref.shape[0]
    heads = lambda a: a.reshape(a.shape[0], n_heads, a.shape[1] // n_heads)

    @pl.when(i == 0)
    def _():
        o_ref[0:n_meta] = heads(meta_ref[...])
        o_ref[n_meta:] = heads(x_ref[0, 0, 0:tm - n_meta, :])

    @pl.when((i > 0) & (i < n_blk - 1))
    def _():
        o_ref[...] = heads(x_ref[0, 0])

    @pl.when(i == n_blk - 1)
    def _():
        o_ref[0:n_meta] = heads(x_ref[0, 0, tm - n_meta:tm, :])
        o_ref[n_meta:] = jnp.zeros((tm - n_meta,) + o_ref.shape[1:], o_ref.dtype)


def kv_rows_with_meta(kvu, slab, meta_rows, n_heads, tm):
    _, b, s, c = kvu.shape
    n_meta = meta_rows.shape[0]
    n_blk = s // tm + 1
    assert n_meta % SUBLANES == 0 and n_meta < tm
    el = pl.Element
    return pl.pallas_call(
        functools.partial(_kv_rows_kernel, n_meta=n_meta, n_heads=n_heads),
        grid=(b, n_blk),
        in_specs=[pl.BlockSpec((el(1), el(1), el(tm), el(c)),
                               lambda bi, i: (slab, bi, pl.multiple_of(
                                   jnp.clip(i * tm - n_meta, 0, s - tm), SUBLANES), 0)),
                  pl.BlockSpec((n_meta, c), lambda bi, i: (0, 0))],
        out_specs=pl.BlockSpec((None, tm, n_heads, c // n_heads), lambda bi, i: (bi, i, 0, 0)),
        out_shape=jax.ShapeDtypeStruct((b, n_meta + s, n_heads, c // n_heads), F32),
        compiler_params=_cparams("arbitrary", "arbitrary"),
        name="kv_rows_with_meta",
    )(kvu, meta_rows)


def _stack_q(q, dk):
    lane = lax.broadcasted_iota(jnp.int32, q.shape, 1)
    zero = jnp.zeros_like(q)
    return jnp.concatenate([jnp.where(lane < dk, q, zero), jnp.where(lane >= dk, q, zero)], axis=0)


def _scores(qq, kblk):
    return lax.dot_general(qq, kblk, (((1,), (1,)), ((), ())), preferred_element_type=F32)


def _add_tile(s, tile):
    t, c = tile.shape
    return (s.reshape(2, t, c) + tile[None]).reshape(2 * t, c)


def _with_ones(v):
    return jnp.concatenate([v, jnp.ones_like(v)], axis=-1)


def _softmax_update(s, vx, carry):
    return _softmax_update_with_max(s, jnp.max(s, axis=-1, keepdims=True), vx, carry)


def _softmax_update_with_max(s, s_max, vx, carry):
    m, acc = carry
    m_new = jnp.maximum(m, s_max)
    alpha = jnp.exp2(m - m_new)
    p = jnp.exp2(s - m_new).astype(BF16)
    return m_new, alpha * acc + jnp.dot(p, vx, preferred_element_type=F32)


def _combine_heads_out(acc, lam, g, t):
    dv = acc.shape[-1] // 2
    o = acc[:t, :dv] / acc[:t, dv:] - lam * (acc[t:, :dv] / acc[t:, dv:])
    return _rms(o, g) * (1.0 - LAMBDA_INIT_0)


def _prompt_attn_kernel(q_ref, k_ref, v_ref, km_ref, vm_ref, d0_ref, d1_ref, dm_ref,
                        lamv_ref, g_ref, o_ref, vx_ref, *, t, dk, n_grp):
    qi = pl.program_id(2)
    r = 2 * t
    e = 2 * dk
    dv = v_ref.shape[-1] // n_grp
    heads = range(n_grp)
    hs = lambda hh, w: slice(hh * w, (hh + 1) * w)

    @pl.when(qi == 0)
    def _():
        for hh in heads:
            vx_ref[hh, :, :dv] = v_ref[:, hs(hh, dv)]
            vx_ref[hh, :, dv:] = jnp.ones((v_ref.shape[0], dv), BF16)

    qq = [_stack_q(q_ref[:, hs(hh, e)], dk) for hh in heads]
    rows = lambda j: pl.ds(pl.multiple_of(j * t, t), t)
    kblk = lambda hh, j: k_ref[rows(j), hs(hh, e)]
    vxblk = lambda hh, j: vx_ref[hh, rows(j), :]

    n_far = jnp.maximum(qi - 1, 0)

    def far_body(j, carry):
        out = []
        for hh in heads:
            s_cur, m, acc = carry[hh]
            s_next = _scores(qq[hh], kblk(hh, j + 1))
            m, acc = _softmax_update(s_cur, vxblk(hh, j), (m, acc))
            out.append((s_next, m, acc))
        return tuple(out)

    init = tuple((_scores(qq[hh], kblk(hh, 0)), jnp.full((r, 1), NEG, F32), jnp.zeros((r, 2 * dv), F32))
                 for hh in heads)
    carry = lax.fori_loop(0, n_far, far_body, init)

    lam = _lambda(lamv_ref)
    for hh in heads:
        s_prev, m, acc = carry[hh]
        s_sub = _add_tile(s_prev, jnp.where(qi >= 1, d1_ref[hh], NEG))
        s_diag = _add_tile(_scores(qq[hh], kblk(hh, qi)), d0_ref[hh])
        s_meta = _add_tile(_scores(qq[hh], km_ref[:, hs(hh, e)]), jnp.where(qi == 0, dm_ref[hh], 0.0))
        c = _softmax_update(s_meta, _with_ones(vm_ref[:, hs(hh, dv)]), (m, acc))
        c = _softmax_update(s_diag, vxblk(hh, qi), c)
        m, acc = _softmax_update(s_sub, vxblk(hh, n_far), c)
        o_ref[:, hs(hh, dv)] = _combine_heads_out(acc, lam, g_ref[...], t).astype(o_ref.dtype)


def prompt_attention(qkv, kmb, vmb, d0, d1, dm, lamv, subln_g, n_heads, t, n_grp):
    _, b, s, width = qkv.shape
    e = dv = width // n_heads
    nm = kmb.shape[0]
    ge, gv = n_grp * e, n_grp * dv
    tile = lambda c: pl.BlockSpec((n_grp, t, c), lambda bi, h, qi: (h, 0, 0))
    return pl.pallas_call(
        functools.partial(_prompt_attn_kernel, t=t, dk=e // 2, n_grp=n_grp),
        grid=(b, n_heads // n_grp, s // t),
        in_specs=[
            pl.BlockSpec((None, None, t, ge), lambda bi, h, qi: (0, bi, qi, h)),
            pl.BlockSpec((None, None, s, ge), lambda bi, h, qi: (1, bi, 0, h)),
            pl.BlockSpec((None, None, s, gv), lambda bi, h, qi: (2, bi, 0, h)),
            pl.BlockSpec((nm, ge), lambda bi, h, qi: (0, h)),
            pl.BlockSpec((nm, gv), lambda bi, h, qi: (0, h)),
            tile(t), tile(t), tile(nm),
            pl.BlockSpec((4, e // 2), lambda bi, h, qi: (0, 0)),
            pl.BlockSpec((1, dv), lambda bi, h, qi: (0, 0)),
        ],
        out_specs=pl.BlockSpec((None, t, gv), lambda bi, h, qi: (bi, qi, h)),
        out_shape=jax.ShapeDtypeStruct((b, s, n_heads * dv), BF16),
        scratch_shapes=[pltpu.VMEM((n_grp, s, 2 * dv), BF16)],
        compiler_params=_cparams("arbitrary", "arbitrary", "arbitrary"),
        name="prompt_attention",
    )(qkv, qkv, qkv, kmb, vmb, d0, d1, dm, lamv, subln_g.reshape(1, -1))


def _meta_attn_kernel(q_ref, k_ref, v_ref, d_ref, lamv_ref, g_ref, o_ref, *, n_heads, dk):
    t = q_ref.shape[0]
    e = 2 * dk
    dv = v_ref.shape[-1] // n_heads
    lam = _lambda(lamv_ref)
    for h in range(n_heads):
        qq = _stack_q(q_ref[:, h * e:(h + 1) * e], dk)
        kblk = k_ref[:, h * e:(h + 1) * e].astype(BF16)
        vx = _with_ones(v_ref[:, h * dv:(h + 1) * dv].astype(BF16))
        carry = (jnp.full((2 * t, 1), NEG, F32), jnp.zeros((2 * t, 2 * dv), F32))
        m, acc = _softmax_update(_add_tile(_scores(qq, kblk), d_ref[h]), vx, carry)
        o_ref[:, h * dv:(h + 1) * dv] = _combine_heads_out(acc, lam, g_ref[...], t).astype(o_ref.dtype)


def meta_attention(q, k, v, dmeta, lamv, subln_g, n_heads):
    t = q.shape[0]
    dv = v.shape[-1] // n_heads
    return pl.pallas_call(
        functools.partial(_meta_attn_kernel, n_heads=n_heads, dk=k.shape[-1] // n_heads // 2),
        out_shape=jax.ShapeDtypeStruct((t, n_heads * dv), BF16),
        name="meta_attention",
    )(q, k, v, dmeta, lamv, subln_g.reshape(1, -1))


def _decode_attn_kernel(pt_ref, q_ref, kn_ref, vn_ref, mb_ref, mbl_ref, b0_ref, lamv_ref, g_ref, *rest,
                        n_pages_step, n_heads, dk):
    k_refs = rest[:n_pages_step]
    v_refs = rest[n_pages_step:2 * n_pages_step]
    o_ref = rest[2 * n_pages_step]
    qq_ref, m_ref, l_ref, acc_ref = rest[2 * n_pages_step + 1:]
    del pt_ref
    g = pl.program_id(1)
    n_g = pl.num_programs(1)
    e = 2 * dk

    @pl.when(g == 0)
    def _():
        qf = q_ref[...].astype(F32)
        lane = lax.broadcasted_iota(jnp.int32, qf.shape, 1)
        qq = jnp.concatenate([jnp.where(lane < dk, qf, 0.0), jnp.where(lane >= dk, qf, 0.0)], axis=0)
        qq_ref[...] = qq.astype(BF16)
        kn = jnp.concatenate([kn_ref[...], kn_ref[...]], axis=0)
        s_new = jnp.sum(qq * kn, axis=-1, keepdims=True) + b0_ref[...]
        m_ref[...] = s_new
        l_ref[...] = jnp.ones_like(s_new)
        acc_ref[...] = jnp.concatenate([vn_ref[...], vn_ref[...]], axis=0)

    qq = qq_ref[...]
    s_pages = []
    for p in range(n_pages_step):
        bias = mb_ref[...]
        if p == n_pages_step - 1:
            bias = jnp.where(g == n_g - 1, mbl_ref[...], bias)
        rows = k_refs[p].shape[0] * n_heads
        s_pages.append(_scores(qq, k_refs[p][...].reshape(rows, e).astype(BF16)) + bias)
    m_old = m_ref[...]
    m_loc = s_pages[0]
    for s in s_pages[1:]:
        m_loc = jnp.maximum(m_loc, s)
    m_new = jnp.maximum(m_old, jnp.max(m_loc, axis=-1, keepdims=True))
    alpha = jnp.exp2(m_old - m_new)
    l_sum = None
    pv = None
    for p in range(n_pages_step):
        w = jnp.exp2(s_pages[p] - m_new)
        rows = v_refs[p].shape[0] * n_heads
        vblk = v_refs[p][...].reshape(rows, v_refs[p].shape[-1]).astype(BF16)
        d = jnp.dot(w.astype(BF16), vblk, preferred_element_type=F32)
        l_sum = w if l_sum is None else l_sum + w
        pv = d if pv is None else pv + d
    l_new = alpha * l_ref[...] + jnp.sum(l_sum, axis=-1, keepdims=True)
    acc_new = alpha * acc_ref[...] + pv
    m_ref[...] = m_new
    l_ref[...] = l_new
    acc_ref[...] = acc_new

    @pl.when(g == n_g - 1)
    def _():
        o = (acc_new[:n_heads] / l_new[:n_heads]
             - _lambda(lamv_ref) * (acc_new[n_heads:] / l_new[n_heads:]))
        o_ref[...] = (_rms(o, g_ref[...]) * (1.0 - LAMBDA_INIT_0)).astype(o_ref.dtype)


def decode_attention(page_table, q, k_new, v_new, cache_k, cache_v, mb, mbl, b0, lamv, subln_g,
                     pages_per_step):
    bd, n_pages = page_table.shape
    _, page, n_heads, e = cache_k.shape
    dv = cache_v.shape[-1]
    p_step = pages_per_step
    r = 2 * n_heads
    seq_blk = lambda w: pl.BlockSpec((None, n_heads, w), lambda b, g, pt: (b, 0, 0))
    const2 = lambda b, g, pt: (0, 0)

    def page_spec(p, w):
        return pl.BlockSpec((None, page, n_heads, w), lambda b, g, pt: (pt[b, g * p_step + p], 0, 0, 0))

    grid_spec = pltpu.PrefetchScalarGridSpec(
        num_scalar_prefetch=1,
        grid=(bd, n_pages // p_step),
        in_specs=[seq_blk(e), seq_blk(e), seq_blk(dv),
                  pl.BlockSpec((r, page * n_heads), const2),
                  pl.BlockSpec((r, page * n_heads), const2),
                  pl.BlockSpec((r, 1), const2),
                  pl.BlockSpec((4, e // 2), const2),
                  pl.BlockSpec((1, dv), const2)]
                 + [page_spec(p, e) for p in range(p_step)]
                 + [page_spec(p, dv) for p in range(p_step)],
        out_specs=seq_blk(dv),
        scratch_shapes=[pltpu.VMEM((r, e), BF16), pltpu.VMEM((r, 1), F32),
                        pltpu.VMEM((r, 1), F32), pltpu.VMEM((r, dv), F32)],
    )
    return pl.pallas_call(
        functools.partial(_decode_attn_kernel, n_pages_step=p_step, n_heads=n_heads, dk=e // 2),
        grid_spec=grid_spec,
        out_shape=jax.ShapeDtypeStruct((bd, n_heads, dv), BF16),
        compiler_params=_cparams("arbitrary", "arbitrary"),
        name="decode_attention",
    )(page_table, q, k_new, v_new, mb, mbl, b0, lamv, subln_g.reshape(1, -1),
      *([cache_k] * p_step), *([cache_v] * p_step))


def _conv_rows(src_ref, base, n_rows, cw_ref, conv_w):
    acc = src_ref[pl.ds(base, n_rows), :] * cw_ref[0:1, :]
    for w in range(1, conv_w):
        acc = acc + src_ref[pl.ds(base + w, n_rows), :] * cw_ref[w:w + 1, :]
    return acc


def _conv_tiles(win, cwb_ref, lanes, conv_w, first, n_out):
    y = None
    for k in range(SUBLANES):
        taps = [w for w in range(conv_w) if (first + w) % SUBLANES == k]
        if not taps:
            continue
        n_p = n_out if k == 0 else n_out + 1
        part = None
        for w in taps:
            j = (first + w) // SUBLANES
            term = win[j:j + n_p] * cwb_ref[w, :, lanes][None]
            part = term if part is None else part + term
        part = part.reshape(n_p * SUBLANES, part.shape[-1])
        part = part if k == 0 else part[k:k + n_out * SUBLANES]
        y = part if y is None else y + part
    return y


def _prompt_conv_kernel(u_ref, uh_ref, head_ref, cwb_ref, cb_ref, lg_ref, lb_ref, c_ref, s_ref, y_ref,
                        *, conv_w, halo, rows):
    i = pl.program_id(1)
    tm, c = u_ref.shape
    s_ref[0:halo, :] = jnp.where(i == 0, head_ref[...], uh_ref[...])
    s_ref[halo:, :] = u_ref[...]
    first = halo - (conv_w - 1)
    n_win = (rows + halo) // SUBLANES

    def chunk(ci, _):
        r0 = pl.multiple_of(ci * rows, rows)
        for lb in range(c // LANES):
            lanes = slice(lb * LANES, (lb + 1) * LANES)
            win = s_ref[pl.ds(r0, rows + halo), lanes].reshape(n_win, SUBLANES, LANES)
            y_ref[:, lanes] = _conv_tiles(win, cwb_ref, lanes, conv_w, first, rows // SUBLANES)
        y = y_ref[...] + cb_ref[...]
        c_ref[pl.ds(r0, rows), :] = _silu(_layernorm(y, lg_ref[...], lb_ref[...])).astype(c_ref.dtype)
        return 0

    lax.fori_loop(0, tm // rows, chunk, 0)


def prompt_conv_module(kvu, head, conv_w, conv_b, ln_g, ln_b, tm):
    _, b, s, c = kvu.shape
    halo = head.shape[0]
    w = conv_w.shape[0]
    per = tm // halo
    vec = lambda a: a.reshape(1, -1)
    const = lambda bi, i: (0, 0)
    cwb = jnp.broadcast_to(conv_w[:, None, :], (w, SUBLANES, c))
    return pl.pallas_call(
        functools.partial(_prompt_conv_kernel, conv_w=w, halo=halo, rows=CONV_ROWS),
        grid=(b, s // tm),
        in_specs=[pl.BlockSpec((None, None, tm, c), lambda bi, i: (2, bi, i, 0)),
                  pl.BlockSpec((None, None, halo, c), lambda bi, i: (2, bi, jnp.maximum(i * per - 1, 0), 0)),
                  pl.BlockSpec((halo, c), const),
                  pl.BlockSpec((w, SUBLANES, c), lambda bi, i: (0, 0, 0)),
                  pl.BlockSpec((1, c), const), pl.BlockSpec((1, c), const), pl.BlockSpec((1, c), const)],
        out_specs=pl.BlockSpec((None, tm, c), lambda bi, i: (bi, i, 0)),
        out_shape=jax.ShapeDtypeStruct((b, s, c), BF16),
        scratch_shapes=[pltpu.VMEM((tm + halo, c), F32), pltpu.VMEM((CONV_ROWS, c), F32)],
        compiler_params=_cparams("arbitrary", "arbitrary"),
        name="prompt_conv_module",
    )(kvu, kvu, head, cwb, vec(conv_b), vec(ln_g), vec(ln_b))


def _small_conv_kernel(za_ref, zg_ref, st_ref, cw_ref, cb_ref, lg_ref, lb_ref, u_ref, c_ref, s_ref,
                       *, conv_w, n_sample):
    n_meta = za_ref.shape[0] - n_sample
    u = za_ref[...] * jax.nn.sigmoid(zg_ref[...])
    u_ref[...] = u
    y = u[:n_sample] * cw_ref[conv_w - 1:conv_w, :]
    for w in range(conv_w - 1):
        y = y + st_ref[w] * cw_ref[w:w + 1, :]
    y_s = y + cb_ref[...]
    hist = s_ref.shape[0] - n_meta
    s_ref[0:hist, :] = jnp.zeros((hist, s_ref.shape[1]), F32)
    s_ref[hist:, :] = u[n_sample:]
    y_m = _conv_rows(s_ref, hist - (conv_w - 1), n_meta, cw_ref, conv_w) + cb_ref[...]
    yy = jnp.concatenate([y_s, y_m], axis=0)
    c_ref[...] = _silu(_layernorm(yy, lg_ref[...], lb_ref[...])).astype(c_ref.dtype)


def small_conv_module(za, zg, state_t, conv_w, conv_b, ln_g, ln_b, n_sample):
    m, c = za.shape
    w = conv_w.shape[0]
    n_meta = m - n_sample
    hist = SUBLANES * ((w - 1 + SUBLANES - 1) // SUBLANES)
    vec = lambda a: a.reshape(1, -1)
    return pl.pallas_call(
        functools.partial(_small_conv_kernel, conv_w=w, n_sample=n_sample),
        out_shape=[jax.ShapeDtypeStruct((m, c), F32), jax.ShapeDtypeStruct((m, c), BF16)],
        scratch_shapes=[pltpu.VMEM((hist + n_meta, c), F32)],
        name="small_conv_module",
    )(za, zg, state_t, conv_w, vec(conv_b), vec(ln_g), vec(ln_b))


def _merge_kernel(o_ref, c_ref, x_ref, w_ref, y_ref):
    da = o_ref.shape[-1]
    y = x_ref[...] + jnp.dot(o_ref[...], w_ref[0:da, :], preferred_element_type=F32)
    y_ref[...] = y + jnp.dot(c_ref[...], w_ref[da:, :], preferred_element_type=F32)


def merge_out(o_n, c, x, w_o, tm):
    m, d = x.shape
    da = o_n.shape[-1]
    row = lambda i: (i, 0)
    return pl.pallas_call(
        _merge_kernel,
        grid=(m // tm,),
        in_specs=[pl.BlockSpec((tm, da), row), pl.BlockSpec((tm, d - da), row),
                  pl.BlockSpec((tm, d), row), pl.BlockSpec((d, d), lambda i: (0, 0))],
        out_specs=pl.BlockSpec((tm, d), row),
        out_shape=jax.ShapeDtypeStruct((m, d), F32),
        compiler_params=_cparams("arbitrary"),
        name="merge_out",
    )(o_n, c, x, w_o)


def _gelu(x):
    return 0.5 * x * (1.0 + lax.erf(x * math.sqrt(0.5)))


def _ffn_kernel(x_ref, g2_ref, gf_ref, wa_ref, wg_ref, wd_ref, cwa_ref, cwg_ref, cba_ref, cbg_ref,
                ha_ref, hg_ref, y_ref, ta_ref, tg_ref,
                xn_ref, ea_ref, eg_ref, ca_ref, cg_ref, *, blocks_per_seq, conv_w, n_sub):
    i = pl.program_id(0)
    f = pl.program_id(1)
    n_f = pl.num_programs(1)
    tm = x_ref.shape[0]
    tf = wa_ref.shape[1]
    ts = tf // n_sub
    pad = SUBLANES

    @pl.when(f == 0)
    def _():
        x = x_ref[...]
        xn_ref[...] = _rms(x, g2_ref[...]).astype(BF16)
        y_ref[...] = x

    xn = xn_ref[...]
    first = (i % blocks_per_seq) == 0
    base = pad - (conv_w - 1)
    down = None
    for sc in range(n_sub):
        cols = slice(sc * ts, (sc + 1) * ts)
        up_a = jnp.dot(xn, wa_ref[:, cols], preferred_element_type=F32)
        up_g = jnp.dot(xn, wg_ref[:, cols], preferred_element_type=F32)
        ta_ref[:, cols] = up_a[tm - pad:]
        tg_ref[:, cols] = up_g[tm - pad:]
        ea_ref[sc, 0:pad, :] = jnp.where(first, ha_ref[:, cols], ca_ref[f, :, cols])
        eg_ref[sc, 0:pad, :] = jnp.where(first, hg_ref[:, cols], cg_ref[f, :, cols])
        ea_ref[sc, pad:, :] = up_a
        eg_ref[sc, pad:, :] = up_g
        ca_ref[f, :, cols] = up_a[tm - pad:]
        cg_ref[f, :, cols] = up_g[tm - pad:]
        conv_a = _conv_rows(ea_ref.at[sc], base, tm, cwa_ref.at[:, cols], conv_w) + cba_ref[:, cols]
        conv_g = _conv_rows(eg_ref.at[sc], base, tm, cwg_ref.at[:, cols], conv_w) + cbg_ref[:, cols]
        hidden = (_gelu(conv_g) * conv_a).astype(BF16)
        d = jnp.dot(hidden, wd_ref[cols, :], preferred_element_type=F32)
        down = d if down is None else down + d
    y_ref[...] += down

    @pl.when(f == n_f - 1)
    def _():
        y_ref[...] = _rms(y_ref[...], gf_ref[...])


def prompt_ffn(x, norm2_g, final_g, wab, wd, cwa, cwg, cba, cbg, head_a, head_g, rows_per_seq,
               tm, tf):
    m, d = x.shape
    ffp = wab.shape[-1]
    n_f = ffp // tf
    bps = rows_per_seq // tm
    w = cwa.shape[0]
    ts = tf // FF_SUB
    row = lambda i, f: (i, 0)
    const = lambda i, f: (0, 0)
    col = lambda i, f: (0, f)
    tail = pl.BlockSpec((None, SUBLANES, tf), lambda i, f: (i, 0, f))
    return pl.pallas_call(
        functools.partial(_ffn_kernel, blocks_per_seq=bps, conv_w=w, n_sub=FF_SUB),
        grid=(m // tm, n_f),
        in_specs=[pl.BlockSpec((tm, d), row, pipeline_mode=pl.Buffered(1)),
                  pl.BlockSpec((1, d), const), pl.BlockSpec((1, d), const),
                  pl.BlockSpec((None, d, tf), lambda i, f: (0, 0, f)),
                  pl.BlockSpec((None, d, tf), lambda i, f: (1, 0, f)),
                  pl.BlockSpec((tf, d), lambda i, f: (f, 0)),
                  pl.BlockSpec((w, tf), col), pl.BlockSpec((w, tf), col),
                  pl.BlockSpec((1, tf), col), pl.BlockSpec((1, tf), col),
                  pl.BlockSpec((SUBLANES, tf), col), pl.BlockSpec((SUBLANES, tf), col)],
        out_specs=[pl.BlockSpec((tm, d), row), tail, tail],
        out_shape=[jax.ShapeDtypeStruct((m, d), F32),
                   jax.ShapeDtypeStruct((m // tm, SUBLANES, ffp), F32),
                   jax.ShapeDtypeStruct((m // tm, SUBLANES, ffp), F32)],
        scratch_shapes=[pltpu.VMEM((tm, d), BF16),
                        pltpu.VMEM((FF_SUB, tm + SUBLANES, ts), F32),
                        pltpu.VMEM((FF_SUB, tm + SUBLANES, ts), F32),
                        pltpu.VMEM((n_f, SUBLANES, tf), F32), pltpu.VMEM((n_f, SUBLANES, tf), F32)],
        compiler_params=_cparams("arbitrary", "arbitrary"),
        name="prompt_ffn",
    )(x, norm2_g.reshape(1, -1), final_g.reshape(1, -1), wab, wab, wd, cwa, cwg,
      cba.reshape(1, -1), cbg.reshape(1, -1), head_a, head_g)


def _small_gate_kernel(ua_ref, ug_ref, sa_ref, sg_ref, cwa_ref, cwg_ref, cba_ref, cbg_ref, h_ref,
                       *, conv_w):
    def conv(u_ref, s_ref, cw_ref, cb_ref):
        y = u_ref[...] * cw_ref[conv_w - 1:conv_w, :] + cb_ref[...]
        for w in range(conv_w - 1):
            y = y + s_ref[w] * cw_ref[w:w + 1, :]
        return y

    a = conv(ua_ref, sa_ref, cwa_ref, cba_ref)
    g = conv(ug_ref, sg_ref, cwg_ref, cbg_ref)
    h_ref[...] = (_gelu(g) * a).astype(h_ref.dtype)


def small_gate(up_a, up_g, st_a, st_g, cwa, cwg, cba, cbg):
    return pl.pallas_call(
        functools.partial(_small_gate_kernel, conv_w=cwa.shape[0]),
        out_shape=jax.ShapeDtypeStruct(up_a.shape, BF16),
        name="small_gate",
    )(up_a, up_g, st_a, st_g, cwa, cwg, cba.reshape(1, -1), cbg.reshape(1, -1))


def _residual_norm_kernel(x_ref, d_ref, g_ref, y_ref):
    y_ref[...] = _rms(x_ref[...] + d_ref[...], g_ref[...])


def residual_norm(x, delta, g):
    return pl.pallas_call(
        _residual_norm_kernel,
        out_shape=jax.ShapeDtypeStruct(x.shape, F32),
        name="residual_norm",
    )(x, delta, g.reshape(1, -1))


def kernel(x_prompt, x_sample, cache_k, cache_v, page_table, state_conv, state_ffn_conv, meta_tokens,
           rel_bias, norm1_g, w_in, lambda_q1, lambda_k1, lambda_q2, lambda_k2, subln_g, conv_w, conv_b,
           conv_ln_g, conv_ln_b, w_o, norm2_g, ffn_w_up, ffn_conv_w, ffn_conv_b, ffn_w_down,
           final_norm_g):
    depth = w_in.shape[0]
    assert depth == 1, "single-layer step only"
    b, s, d = x_prompt.shape
    bd, t_dec = x_sample.shape[:2]
    assert t_dec == 1
    n_meta = meta_tokens.shape[0]
    n_pool, page, n_heads, e = cache_k.shape[1:]
    dv = cache_v.shape[-1]
    dk = e // 2
    d_attn = n_heads * dv
    d_conv = conv_w.shape[-1]
    cw_len = conv_w.shape[1]
    fw_len = ffn_conv_w.shape[1]
    d_ff = ffn_w_down.shape[1]
    n_buckets = rel_bias.shape[0]
    t = ATTN_T
    tm = ROW_TILE
    tf = FF_TILE
    assert e == dv == LANES and d_conv == d_attn and w_in.shape[-1] == 5 * d_attn
    assert s % tm == 0 and s % t == 0 and n_meta <= t and page_table.shape[1] % DEC_PAGES == 0
    assert s % FF_ROWS == 0 and (b * s) % IN_ROWS == 0
    me = n_buckets // 2
    sat = math.ceil(me * (MAX_DISTANCE / me) ** ((n_buckets - 1 - me) / (n_buckets - me))) + 1
    assert t + 1 >= sat and page + 1 >= sat
    halo = SUBLANES * ((cw_len - 1 + SUBLANES - 1) // SUBLANES)
    assert n_meta <= halo and tm % halo == 0 and tm % CONV_ROWS == 0 and fw_len - 1 <= SUBLANES

    ffp = tf * ((d_ff + tf - 1) // tf)
    pad_c = lambda a: jnp.pad(a, ((0, 0), (0, ffp - d_ff)))
    cwa, cwg = pad_c(ffn_conv_w[0, :, :d_ff]), pad_c(ffn_conv_w[0, :, d_ff:])
    cba, cbg = pad_c(ffn_conv_b[:, :d_ff])[0], pad_c(ffn_conv_b[:, d_ff:])[0]
    lamv = jnp.stack([lambda_q1[0], lambda_k1[0], lambda_q2[0], lambda_k2[0]]).astype(F32)
    q_scale = dk ** -0.5 * LOG2E

    ar = lambda n: jnp.arange(n, dtype=jnp.int32)
    by_rel = _bias_tile(rel_bias, ar(2 * t)[None, :])[:, 0, :]
    d0 = _toeplitz(jnp.concatenate([jnp.full((n_heads, t - 1), NEG, F32), by_rel[:, :t]], axis=1), t)
    d1 = _toeplitz(by_rel[:, 1:], t)
    dm = _bias_tile(rel_bias, n_meta + ar(t)[:, None] - ar(n_meta)[None, :])
    dmeta = _bias_tile(rel_bias, ar(n_meta)[:, None] - ar(n_meta)[None, :])
    stack2 = lambda a: jnp.concatenate([a, a], axis=0)
    same_head = (ar(n_heads)[:, None, None] == ar(n_heads)[None, None, :])
    blast = _bias_tile(rel_bias, (page - ar(page))[None, :])[:, 0, :]
    mb = stack2(jnp.broadcast_to(jnp.where(same_head, 0.0, NEG), (n_heads, page, n_heads))
                ).reshape(2 * n_heads, page * n_heads)
    mbl = stack2(jnp.where(same_head, blast[:, :, None], NEG)).reshape(2 * n_heads, page * n_heads)
    b0 = stack2(_bias_tile(rel_bias, jnp.zeros((1, 1), jnp.int32))[:, 0, :])

    x_small = jnp.concatenate([x_sample[:, 0, :], meta_tokens.astype(F32)], axis=0)
    xn_small = rmsnorm_cast(x_small, norm1_g[0], BF16)
    z, w_in_b = mm_cast(xn_small, w_in[0], 512)
    q_small = (z[:, :d_attn] * q_scale).astype(BF16)
    k_small = z[:, d_attn:2 * d_attn]
    v_small = z[:, 2 * d_attn:3 * d_attn]
    state_t = jnp.transpose(state_conv[0], (1, 0, 2))
    u_small, c_small = small_conv_module(z[:, 3 * d_attn:4 * d_attn], z[:, 4 * d_attn:], state_t,
                                         conv_w[0], conv_b[0], conv_ln_g[0], conv_ln_b[0], bd)
    o_meta = meta_attention(q_small[bd:], k_small[bd:], v_small[bd:], dmeta, lamv, subln_g[0], n_heads)
    hd = lambda a: a.reshape(bd, n_heads, -1)
    o_dec = decode_attention(page_table, hd(q_small[:bd]), hd(k_small[:bd]), hd(v_small[:bd]),
                             cache_k[0], cache_v[0], mb, mbl, b0, lamv, subln_g[0], DEC_PAGES)
    o_small = jnp.concatenate([o_dec.reshape(bd, d_attn), o_meta], axis=0)
    h_delta, w_o_b = mm_cast(jnp.concatenate([o_small, c_small], axis=1), w_o[0], 512)
    h1_small, xn2_small = add_rmsnorm(x_small, h_delta, norm2_g[0])
    up_small, wab_b = up_cast(xn2_small, ffn_w_up[0], d_ff, ffp, tf)
    up_a_small, up_g_small = up_small[0], up_small[1]
    st = jnp.transpose(state_ffn_conv[0], (1, 0, 2))
    st_pad = lambda a: pad_c(a.reshape(-1, d_ff)).reshape(fw_len - 1, bd, ffp)
    hid = small_gate(up_a_small[:bd], up_g_small[:bd], st_pad(st[:, :, :d_ff]), st_pad(st[:, :, d_ff:]),
                     cwa, cwg, cba, cbg)
    down, wd_b = mm_cast(hid, ffn_w_down[0], 256, k_pad=ffp)
    y_sample = residual_norm(h1_small[:bd], down, final_norm_g)

    xp = x_prompt.reshape(b * s, d)
    qkv_b, kvu = prompt_inproj(xp, norm1_g[0], w_in_b, d_attn, q_scale, IN_ROWS)
    qkv_b = qkv_b.reshape(3, b, s, d_attn)
    kvu = kvu.reshape(3, b, s, d_attn)
    o_p = prompt_attention(qkv_b, k_small[bd:].astype(BF16), v_small[bd:].astype(BF16),
                           d0, d1, dm, lamv, subln_g[0], n_heads, t, ATTN_HEADS)
    head_u = jnp.concatenate([jnp.zeros((halo - n_meta, d_conv), F32), u_small[bd:]], axis=0)
    c_p = prompt_conv_module(kvu, head_u, conv_w[0], conv_b[0], conv_ln_g[0], conv_ln_b[0], tm)
    h1_p = merge_out(o_p.reshape(b * s, -1), c_p.reshape(b * s, -1), xp, w_o_b, tm)
    n_t = fw_len - 1
    head_rows = lambda a: jnp.concatenate([jnp.zeros((SUBLANES - n_t, ffp), F32), a[-n_t:]], axis=0)
    y_p, tail_a, tail_g = prompt_ffn(h1_p, norm2_g[0], final_norm_g, wab_b, wd_b, cwa, cwg, cba, cbg,
                                     head_rows(up_a_small[bd:]), head_rows(up_g_small[bd:]), s, FF_ROWS, tf)

    k_prompt = kv_rows_with_meta(kvu, 0, k_small[bd:], n_heads, tm)[None]
    v_prompt = kv_rows_with_meta(kvu, 1, v_small[bd:], n_heads, tm)[None]
    conv_prompt = kvu[2][None, :, s - (cw_len - 1):]
    bps = s // FF_ROWS
    last_blk = lambda a: a[bps - 1::bps, SUBLANES - n_t:, :d_ff]
    ffn_prompt = jnp.concatenate([last_blk(tail_a), last_blk(tail_g)], axis=-1)[None]
    k_sample = k_small[:bd].reshape(1, bd, 1, n_heads, e)
    v_sample = v_small[:bd].reshape(1, bd, 1, n_heads, dv)
    conv_sample = jnp.concatenate([state_conv[0][:, 1:], u_small[:bd, None, :]], axis=1)[None]
    up_s = jnp.concatenate([up_a_small[:bd, :d_ff], up_g_small[:bd, :d_ff]], axis=-1)
    ffn_sample = jnp.concatenate([state_ffn_conv[0][:, 1:], up_s[:, None, :]], axis=1)[None]
    return (y_p.reshape(b, s, d), y_sample.reshape(bd, 1, d), k_prompt, v_prompt, conv_prompt, ffn_prompt,
            k_sample, v_sample, conv_sample, ffn_sample)
```

```python
import functools
import math

import jax
import jax.numpy as jnp
from jax import lax
from jax.experimental import pallas as pl
from jax.experimental.pallas import tpu as pltpu

F32 = jnp.float32
BF16 = jnp.bfloat16

EPS = 1e-6
MAX_DISTANCE = 128
LAMBDA_INIT_0 = 0.8 - 0.6 * math.exp(-0.3 * 0)
NEG = -1e30
LOG2E = math.log2(math.e)

LANES = 128
SUBLANES = 8
VMEM_LIMIT = 56 * 1024 * 1024

ATTN_T = 512
ATTN_HEADS = 2
ROW_TILE = 512
IN_ROWS = 1024
FF_ROWS = 1024
FF_TILE = 512
FF_SUB = 2
CONV_ROWS = 64
DEC_PAGES = 16


def _cparams(*sem):
    return pltpu.CompilerParams(dimension_semantics=sem, vmem_limit_bytes=VMEM_LIMIT)


def _rms(x, g):
    return x * lax.rsqrt(jnp.mean(x * x, axis=-1, keepdims=True) + EPS) * g


def _layernorm(y, g, b):
    mu = jnp.mean(y, axis=-1, keepdims=True)
    d = y - mu
    var = jnp.mean(d * d, axis=-1, keepdims=True)
    return d * lax.rsqrt(var + EPS) * g + b


def _silu(x):
    return x * jax.nn.sigmoid(x)


def _lambda(lamv_ref):
    a = jnp.sum(lamv_ref[0:1, :] * lamv_ref[1:2, :], axis=-1, keepdims=True)
    b = jnp.sum(lamv_ref[2:3, :] * lamv_ref[3:4, :], axis=-1, keepdims=True)
    return jnp.exp(a) - jnp.exp(b) + LAMBDA_INIT_0


def _t5_bucket(rel, n_buckets):
    n = jnp.maximum(rel, 0)
    max_exact = n_buckets // 2
    nf = jnp.maximum(n, 1).astype(F32)
    large = max_exact + (jnp.log(nf / max_exact) / math.log(MAX_DISTANCE / max_exact)
                         * (n_buckets - max_exact)).astype(jnp.int32)
    large = jnp.minimum(large, n_buckets - 1)
    return jnp.where(n < max_exact, n, large)


def _bias_tile(rel_bias, rel):
    nb = rel_bias.shape[0]
    onehot = (_t5_bucket(rel, nb)[..., None] == jnp.arange(nb, dtype=jnp.int32)).astype(F32)
    table = (rel_bias - rel_bias[nb - 1]).astype(F32) * LOG2E
    b = jnp.einsum("rcn,nh->hrc", onehot, table, precision=lax.Precision.HIGHEST)
    return jnp.where((rel >= 0)[None], b, NEG)


def _toeplitz_kernel(u0_ref, u1_ref, d0_ref, d1_ref):
    t = d0_ref.shape[0]
    for u_ref, d_ref in ((u0_ref, d0_ref), (u1_ref, d1_ref)):
        x = jnp.broadcast_to(u_ref[...], (t, 2 * t))
        d_ref[...] = pltpu.roll(x, 0, 1, stride=1, stride_axis=0)[:, :t]


def toeplitz_tiles(w0, w1, t):
    h = w0.shape[0]

    def wrap(w):
        return jnp.concatenate([w[:, :t][:, ::-1], jnp.zeros((h, 1), F32), w[:, t:][:, ::-1]],
                               axis=1).reshape(h, 1, 2 * t)

    vec = pl.BlockSpec((None, 1, 2 * t), lambda hh: (hh, 0, 0))
    tile = pl.BlockSpec((None, t, t), lambda hh: (hh, 0, 0))
    shape = jax.ShapeDtypeStruct((h, t, t), F32)
    return pl.pallas_call(
        _toeplitz_kernel, grid=(h,), in_specs=[vec, vec], out_specs=[tile, tile],
        out_shape=[shape, shape], name="toeplitz_tiles",
    )(wrap(w0), wrap(w1))


def _rmsnorm_cast_kernel(x_ref, g_ref, o_ref):
    o_ref[...] = _rms(x_ref[...], g_ref[...]).astype(o_ref.dtype)


def rmsnorm_cast(x, g, dtype):
    return pl.pallas_call(
        _rmsnorm_cast_kernel,
        out_shape=jax.ShapeDtypeStruct(x.shape, dtype),
        name="rmsnorm_cast",
    )(x, g.reshape(1, -1))


def _mm_cast_kernel(x_ref, w_ref, o_ref, wb_ref):
    k = w_ref.shape[0]
    wb = w_ref[...].astype(BF16)
    wb_ref[0:k, :] = wb
    if wb_ref.shape[0] > k:
        wb_ref[k:, :] = jnp.zeros((wb_ref.shape[0] - k, wb_ref.shape[1]), BF16)
    o_ref[...] = jnp.dot(x_ref[:, 0:k], wb, preferred_element_type=F32)


def mm_cast(x, w, tn, k_pad=None):
    m, kx = x.shape
    k, n = w.shape
    kp = k if k_pad is None else k_pad
    return pl.pallas_call(
        _mm_cast_kernel,
        grid=(n // tn,),
        in_specs=[pl.BlockSpec((m, kx), lambda j: (0, 0)),
                  pl.BlockSpec((k, tn), lambda j: (0, j))],
        out_specs=[pl.BlockSpec((m, tn), lambda j: (0, j)), pl.BlockSpec((kp, tn), lambda j: (0, j))],
        out_shape=[jax.ShapeDtypeStruct((m, n), F32), jax.ShapeDtypeStruct((kp, n), BF16)],
        compiler_params=_cparams("arbitrary"),
        name="mm_cast",
    )(x, w)


def _up_cast_kernel(x_ref, w_ref, o_ref, wb_ref, *, rem):
    hf = pl.program_id(0)
    j = pl.program_id(1)
    last = pl.num_programs(1) - 1
    d, tn = w_ref.shape

    def emit(w):
        wb = w.astype(BF16)
        wb_ref[...] = wb
        o_ref[...] = jnp.dot(x_ref[...], wb, preferred_element_type=F32)

    @pl.when(j < last)
    def _():
        emit(w_ref[...])

    @pl.when((j == last) & (hf == 0))
    def _():
        lane = lax.broadcasted_iota(jnp.int32, (d, tn), 1)
        emit(jnp.where(lane < rem, w_ref[...], 0.0))

    @pl.when((j == last) & (hf == 1))
    def _():
        emit(jnp.concatenate([w_ref[:, tn - rem:], jnp.zeros((d, tn - rem), F32)], axis=1))


def up_cast(x, w_up, d_ff, ffp, tn):
    m, d = x.shape
    n_blk = ffp // tn
    rem = d_ff - (n_blk - 1) * tn
    assert 0 < rem <= tn and rem % LANES == 0 and d_ff >= tn
    el = pl.Element
    return pl.pallas_call(
        functools.partial(_up_cast_kernel, rem=rem),
        grid=(2, n_blk),
        in_specs=[pl.BlockSpec((m, d), lambda hf, j: (0, 0)),
                  pl.BlockSpec((el(d), el(tn)),
                               lambda hf, j: (0, pl.multiple_of(
                                   jnp.minimum(hf * d_ff + j * tn, 2 * d_ff - tn), LANES)))],
        out_specs=[pl.BlockSpec((None, m, tn), lambda hf, j: (hf, 0, j)),
                   pl.BlockSpec((None, d, tn), lambda hf, j: (hf, 0, j))],
        out_shape=[jax.ShapeDtypeStruct((2, m, ffp), F32), jax.ShapeDtypeStruct((2, d, ffp), BF16)],
        compiler_params=_cparams("arbitrary", "arbitrary"),
        name="up_cast",
    )(x, w_up)


def _add_rmsnorm_kernel(x_ref, d_ref, g_ref, h_ref, xn_ref):
    h = x_ref[...] + d_ref[...]
    h_ref[...] = h
    xn_ref[...] = _rms(h, g_ref[...]).astype(xn_ref.dtype)


def add_rmsnorm(x, delta, g):
    return pl.pallas_call(
        _add_rmsnorm_kernel,
        out_shape=[jax.ShapeDtypeStruct(x.shape, F32), jax.ShapeDtypeStruct(x.shape, BF16)],
        name="add_rmsnorm",
    )(x, delta, g.reshape(1, -1))


def _inproj_kernel(x_ref, g_ref, w_ref, wg_ref, ob_ref, of_ref, xn_ref, *, q_scale):
    j = pl.program_id(1)

    @pl.when(j == 0)
    def _():
        xn_ref[...] = _rms(x_ref[...], g_ref[...]).astype(BF16)

    z = jnp.dot(xn_ref[...], w_ref[...], preferred_element_type=F32)

    @pl.when(j == 0)
    def _():
        ob_ref[...] = (z * q_scale).astype(BF16)

    @pl.when((j == 1) | (j == 2))
    def _():
        of_ref[...] = z
        ob_ref[...] = z.astype(BF16)

    @pl.when(j == 3)
    def _():
        gate = jnp.dot(xn_ref[...], wg_ref[...], preferred_element_type=F32)
        of_ref[...] = z * jax.nn.sigmoid(gate)


def prompt_inproj(x, g, w_in, d_attn, q_scale, tm):
    m, d = x.shape
    c = d_attn
    return pl.pallas_call(
        functools.partial(_inproj_kernel, q_scale=q_scale),
        grid=(m // tm, 4),
        in_specs=[pl.BlockSpec((tm, d), lambda i, j: (i, 0)),
                  pl.BlockSpec((1, d), lambda i, j: (0, 0)),
                  pl.BlockSpec((d, c), lambda i, j: (0, j)),
                  pl.BlockSpec((d, c), lambda i, j: (0, 4), pipeline_mode=pl.Buffered(1))],
        out_specs=[pl.BlockSpec((None, tm, c), lambda i, j: (jnp.minimum(j, 2), i, 0)),
                   pl.BlockSpec((None, tm, c), lambda i, j: (jnp.maximum(j - 1, 0), i, 0))],
        out_shape=[jax.ShapeDtypeStruct((3, m, c), BF16), jax.ShapeDtypeStruct((3, m, c), F32)],
        scratch_shapes=[pltpu.VMEM((tm, d), BF16)],
        compiler_params=_cparams("arbitrary", "arbitrary"),
        name="prompt_inproj",
    )(x, g.reshape(1, -1), w_in, w_in)


def _kv_rows_kernel(x_ref, meta_ref, o_ref, *, n_meta, n_heads):
    i = pl.program_id(1)
    n_blk = pl.num_programs(1)
    tm = o_ref.shape[0]
    heads = lambda a: a.reshape(a.shape[0], n_heads, a.shape[1] // n_heads)

    @pl.when(i == 0)
    def _():
        o_ref[0:n_meta] = heads(meta_ref[...])
        o_ref[n_meta:] = heads(x_ref[0, 0, 0:tm - n_meta, :])

    @pl.when((i > 0) & (i < n_blk - 1))
    def _():
        o_ref[...] = heads(x_ref[0, 0])

    @pl.when(i == n_blk - 1)
    def _():
        o_ref[0:n_meta] = heads(x_ref[0, 0, tm - n_meta:tm, :])
        o_ref[n_meta:] = jnp.zeros((tm - n_meta,) + o_ref.shape[1:], o_ref.dtype)


def kv_rows_with_meta(kvu, slab, meta_rows, n_heads, tm):
    _, b, s, c = kvu.shape
    n_meta = meta_rows.shape[0]
    n_blk = s // tm + 1
    assert n_meta % SUBLANES == 0 and n_meta < tm
    el = pl.Element
    return pl.pallas_call(
        functools.partial(_kv_rows_kernel, n_meta=n_meta, n_heads=n_heads),
        grid=(b, n_blk),
        in_specs=[pl.BlockSpec((el(1), el(1), el(tm), el(c)),
                               lambda bi, i: (slab, bi, pl.multiple_of(
                                   jnp.clip(i * tm - n_meta, 0, s - tm), SUBLANES), 0)),
                  pl.BlockSpec((n_meta, c), lambda bi, i: (0, 0))],
        out_specs=pl.BlockSpec((None, tm, n_heads, c // n_heads), lambda bi, i: (bi, i, 0, 0)),
        out_shape=jax.ShapeDtypeStruct((b, n_meta + s, n_heads, c // n_heads), F32),
        compiler_params=_cparams("arbitrary", "arbitrary"),
        name="kv_rows_with_meta",
    )(kvu, meta_rows)


def _stack_q(q, dk):
    lane = lax.broadcasted_iota(jnp.int32, q.shape, 1)
    zero = jnp.zeros_like(q)
    return jnp.concatenate([jnp.where(lane < dk, q, zero), jnp.where(lane >= dk, q, zero)], axis=0)


def _scores(qq, kblk):
    return lax.dot_general(qq, kblk, (((1,), (1,)), ((), ())), preferred_element_type=F32)


def _add_tile(s, tile):
    t, c = tile.shape
    return (s.reshape(2, t, c) + tile[None]).reshape(2 * t, c)


def _with_ones(v):
    return jnp.concatenate([v, jnp.ones_like(v)], axis=-1)


def _softmax_update(s, vx, carry):
    return _softmax_update_with_max(s, jnp.max(s, axis=-1, keepdims=True), vx, carry)


def _softmax_update_with_max(s, s_max, vx, carry):
    m, acc = carry
    m_new = jnp.maximum(m, s_max)
    alpha = jnp.exp2(m - m_new)
    p = jnp.exp2(s - m_new).astype(BF16)
    return m_new, alpha * acc + jnp.dot(p, vx, preferred_element_type=F32)


def _combine_heads_out(acc, lam, g, t):
    dv = acc.shape[-1] // 2
    o = acc[:t, :dv] / acc[:t, dv:] - lam * (acc[t:, :dv] / acc[t:, dv:])
    return _rms(o, g) * (1.0 - LAMBDA_INIT_0)


def _prompt_attn_kernel(q_ref, k_ref, v_ref, km_ref, vm_ref, d0_ref, d1_ref, dm_ref,
                        lamv_ref, g_ref, o_ref, vx_ref, qa_ref, qb_ref, *, t, dk, n_grp):
    qi = pl.program_id(2)
    r = 2 * t
    e = 2 * dk
    dv = v_ref.shape[-1] // n_grp
    heads = range(n_grp)
    hs = lambda hh, w: slice(hh * w, (hh + 1) * w)

    @pl.when(qi == 0)
    def _():
        for hh in heads:
            vx_ref[hh, :, :dv] = v_ref[:, hs(hh, dv)]
            vx_ref[hh, :, dv:] = jnp.ones((v_ref.shape[0], dv), BF16)

    qq = [_stack_q(q_ref[:, hs(hh, e)], dk) for hh in heads]
    rows = lambda j: pl.ds(pl.multiple_of(j * t, t), t)
    kblk = lambda hh, j: k_ref[rows(j), hs(hh, e)]
    vxblk = lambda hh, j: vx_ref[hh, rows(j), :]

    n_far = jnp.maximum(qi - 1, 0)

    for hh in heads:
        qa_ref[hh] = qq[hh]
        qb_ref[hh] = qq[hh]

    def far_body(j, carry):
        out = []
        for hh in heads:
            m, acc = carry[hh]
            kb = kblk(hh, j)
            m_new = jnp.maximum(m, jnp.max(_scores(qa_ref[hh], kb), axis=-1, keepdims=True))
            alpha = jnp.exp2(m - m_new)
            p = jnp.exp2(_scores(qb_ref[hh], kb) - m_new).astype(BF16)
            out.append((m_new, alpha * acc + jnp.dot(p, vxblk(hh, j), preferred_element_type=F32)))
        return tuple(out)

    init = tuple((jnp.full((r, 1), NEG, F32), jnp.zeros((r, 2 * dv), F32)) for hh in heads)
    carry = lax.fori_loop(0, n_far, far_body, init)

    lam = _lambda(lamv_ref)
    for hh in heads:
        m, acc = carry[hh]
        s_prev = _scores(qq[hh], kblk(hh, n_far))
        s_sub = _add_tile(s_prev, jnp.where(qi >= 1, d1_ref[hh], NEG))
        s_diag = _add_tile(_scores(qq[hh], kblk(hh, qi)), d0_ref[hh])
        s_meta = _add_tile(_scores(qq[hh], km_ref[:, hs(hh, e)]), jnp.where(qi == 0, dm_ref[hh], 0.0))
        c = _softmax_update(s_meta, _with_ones(vm_ref[:, hs(hh, dv)]), (m, acc))
        c = _softmax_update(s_diag, vxblk(hh, qi), c)
        m, acc = _softmax_update(s_sub, vxblk(hh, n_far), c)
        o_ref[:, hs(hh, dv)] = _combine_heads_out(acc, lam, g_ref[...], t).astype(o_ref.dtype)


def prompt_attention(qkv, kmb, vmb, d0, d1, dm, lamv, subln_g, n_heads, t, n_grp):
    _, b, s, width = qkv.shape
    e = dv = width // n_heads
    nm = kmb.shape[0]
    ge, gv = n_grp * e, n_grp * dv
    tile = lambda c: pl.BlockSpec((n_grp, t, c), lambda bi, h, qi: (h, 0, 0))
    return pl.pallas_call(
        functools.partial(_prompt_attn_kernel, t=t, dk=e // 2, n_grp=n_grp),
        grid=(b, n_heads // n_grp, s // t),
        in_specs=[
            pl.BlockSpec((None, None, t, ge), lambda bi, h, qi: (0, bi, qi, h)),
            pl.BlockSpec((None, None, s, ge), lambda bi, h, qi: (1, bi, 0, h)),
            pl.BlockSpec((None, None, s, gv), lambda bi, h, qi: (2, bi, 0, h)),
            pl.BlockSpec((nm, ge), lambda bi, h, qi: (0, h)),
            pl.BlockSpec((nm, gv), lambda bi, h, qi: (0, h)),
            tile(t), tile(t), tile(nm),
            pl.BlockSpec((4, e // 2), lambda bi, h, qi: (0, 0)),
            pl.BlockSpec((1, dv), lambda bi, h, qi: (0, 0)),
        ],
        out_specs=pl.BlockSpec((None, t, gv), lambda bi, h, qi: (bi, qi, h)),
        out_shape=jax.ShapeDtypeStruct((b, s, n_heads * dv), BF16),
        scratch_shapes=[pltpu.VMEM((n_grp, s, 2 * dv), BF16),
                        pltpu.VMEM((n_grp, 2 * t, e), BF16), pltpu.VMEM((n_grp, 2 * t, e), BF16)],
        compiler_params=_cparams("arbitrary", "arbitrary", "arbitrary"),
        name="prompt_attention",
    )(qkv, qkv, qkv, kmb, vmb, d0, d1, dm, lamv, subln_g.reshape(1, -1))


def _meta_attn_kernel(q_ref, k_ref, v_ref, d_ref, lamv_ref, g_ref, o_ref, *, n_heads, dk):
    t = q_ref.shape[0]
    e = 2 * dk
    dv = v_ref.shape[-1] // n_heads
    lam = _lambda(lamv_ref)
    for h in range(n_heads):
        qq = _stack_q(q_ref[:, h * e:(h + 1) * e], dk)
        kblk = k_ref[:, h * e:(h + 1) * e].astype(BF16)
        vx = _with_ones(v_ref[:, h * dv:(h + 1) * dv].astype(BF16))
        carry = (jnp.full((2 * t, 1), NEG, F32), jnp.zeros((2 * t, 2 * dv), F32))
        m, acc = _softmax_update(_add_tile(_scores(qq, kblk), d_ref[h]), vx, carry)
        o_ref[:, h * dv:(h + 1) * dv] = _combine_heads_out(acc, lam, g_ref[...], t).astype(o_ref.dtype)


def meta_attention(q, k, v, dmeta, lamv, subln_g, n_heads):
    t = q.shape[0]
    dv = v.shape[-1] // n_heads
    return pl.pallas_call(
        functools.partial(_meta_attn_kernel, n_heads=n_heads, dk=k.shape[-1] // n_heads // 2),
        out_shape=jax.ShapeDtypeStruct((t, n_heads * dv), BF16),
        name="meta_attention",
    )(q, k, v, dmeta, lamv, subln_g.reshape(1, -1))


def _decode_attn_kernel(pt_ref, q_ref, kn_ref, vn_ref, mb_ref, mbl_ref, b0_ref, lamv_ref, g_ref, *rest,
                        n_pages_step, n_heads, dk):
    k_refs = rest[:n_pages_step]
    v_refs = rest[n_pages_step:2 * n_pages_step]
    o_ref = rest[2 * n_pages_step]
    qq_ref, m_ref, l_ref, acc_ref = rest[2 * n_pages_step + 1:]
    del pt_ref
    g = pl.program_id(1)
    n_g = pl.num_programs(1)
    e = 2 * dk

    @pl.when(g == 0)
    def _():
        qf = q_ref[...].astype(F32)
        lane = lax.broadcasted_iota(jnp.int32, qf.shape, 1)
        qq = jnp.concatenate([jnp.where(lane < dk, qf, 0.0), jnp.where(lane >= dk, qf, 0.0)], axis=0)
        qq_ref[...] = qq.astype(BF16)
        kn = jnp.concatenate([kn_ref[...], kn_ref[...]], axis=0)
        s_new = jnp.sum(qq * kn, axis=-1, keepdims=True) + b0_ref[...]
        m_ref[...] = s_new
        l_ref[...] = jnp.ones_like(s_new)
        acc_ref[...] = jnp.concatenate([vn_ref[...], vn_ref[...]], axis=0)

    qq = qq_ref[...]
    s_pages = []
    for p in range(n_pages_step):
        bias = mb_ref[...]
        if p == n_pages_step - 1:
            bias = jnp.where(g == n_g - 1, mbl_ref[...], bias)
        rows = k_refs[p].shape[0] * n_heads
        s_pages.append(_scores(qq, k_refs[p][...].reshape(rows, e).astype(BF16)) + bias)
    m_old = m_ref[...]
    m_loc = s_pages[0]
    for s in s_pages[1:]:
        m_loc = jnp.maximum(m_loc, s)
    m_new = jnp.maximum(m_old, jnp.max(m_loc, axis=-1, keepdims=True))
    alpha = jnp.exp2(m_old - m_new)
    l_sum = None
    pv = None
    for p in range(n_pages_step):
        w = jnp.exp2(s_pages[p] - m_new)
        rows = v_refs[p].shape[0] * n_heads
        vblk = v_refs[p][...].reshape(rows, v_refs[p].shape[-1]).astype(BF16)
        d = jnp.dot(w.astype(BF16), vblk, preferred_element_type=F32)
        l_sum = w if l_sum is None else l_sum + w
        pv = d if pv is None else pv + d
    l_new = alpha * l_ref[...] + jnp.sum(l_sum, axis=-1, keepdims=True)
    acc_new = alpha * acc_ref[...] + pv
    m_ref[...] = m_new
    l_ref[...] = l_new
    acc_ref[...] = acc_new

    @pl.when(g == n_g - 1)
    def _():
        o = (acc_new[:n_heads] / l_new[:n_heads]
             - _lambda(lamv_ref) * (acc_new[n_heads:] / l_new[n_heads:]))
        o_ref[...] = (_rms(o, g_ref[...]) * (1.0 - LAMBDA_INIT_0)).astype(o_ref.dtype)


def decode_attention(page_table, q, k_new, v_new, cache_k, cache_v, mb, mbl, b0, lamv, subln_g,
                     pages_per_step):
    bd, n_pages = page_table.shape
    _, page, n_heads, e = cache_k.shape
    dv = cache_v.shape[-1]
    p_step = pages_per_step
    r = 2 * n_heads
    seq_blk = lambda w: pl.BlockSpec((None, n_heads, w), lambda b, g, pt: (b, 0, 0))
    const2 = lambda b, g, pt: (0, 0)

    def page_spec(p, w):
        return pl.BlockSpec((None, page, n_heads, w), lambda b, g, pt: (pt[b, g * p_step + p], 0, 0, 0))

    grid_spec = pltpu.PrefetchScalarGridSpec(
        num_scalar_prefetch=1,
        grid=(bd, n_pages // p_step),
        in_specs=[seq_blk(e), seq_blk(e), seq_blk(dv),
                  pl.BlockSpec((r, page * n_heads), const2),
                  pl.BlockSpec((r, page * n_heads), const2),
                  pl.BlockSpec((r, 1), const2),
                  pl.BlockSpec((4, e // 2), const2),
                  pl.BlockSpec((1, dv), const2)]
                 + [page_spec(p, e) for p in range(p_step)]
                 + [page_spec(p, dv) for p in range(p_step)],
        out_specs=seq_blk(dv),
        scratch_shapes=[pltpu.VMEM((r, e), BF16), pltpu.VMEM((r, 1), F32),
                        pltpu.VMEM((r, 1), F32), pltpu.VMEM((r, dv), F32)],
    )
    return pl.pallas_call(
        functools.partial(_decode_attn_kernel, n_pages_step=p_step, n_heads=n_heads, dk=e // 2),
        grid_spec=grid_spec,
        out_shape=jax.ShapeDtypeStruct((bd, n_heads, dv), BF16),
        compiler_params=_cparams("arbitrary", "arbitrary"),
        name="decode_attention",
    )(page_table, q, k_new, v_new, mb, mbl, b0, lamv, subln_g.reshape(1, -1),
      *([cache_k] * p_step), *([cache_v] * p_step))


def _conv_rows(src_ref, base, n_rows, cw_ref, conv_w):
    acc = src_ref[pl.ds(base, n_rows), :] * cw_ref[0:1, :]
    for w in range(1, conv_w):
        acc = acc + src_ref[pl.ds(base + w, n_rows), :] * cw_ref[w:w + 1, :]
    return acc


def _conv_tiles(win, cwb_ref, lanes, conv_w, first, n_out):
    y = None
    for k in range(SUBLANES):
        taps = [w for w in range(conv_w) if (first + w) % SUBLANES == k]
        if not taps:
            continue
        n_p = n_out if k == 0 else n_out + 1
        part = None
        for w in taps:
            j = (first + w) // SUBLANES
            term = win[j:j + n_p] * cwb_ref[w, :, lanes][None]
            part = term if part is None else part + term
        part = part.reshape(n_p * SUBLANES, part.shape[-1])
        part = part if k == 0 else part[k:k + n_out * SUBLANES]
        y = part if y is None else y + part
    return y


def _conv_merge_kernel(u_ref, uh_ref, head_ref, cwb_ref, cb_ref, lg_ref, lb_ref, o_ref, x_ref, w_ref,
                       y_ref, s_ref, yc_ref, c_ref, *, conv_w, halo, rows):
    i = pl.program_id(1)
    tm, c = u_ref.shape
    da = o_ref.shape[-1]
    s_ref[0:halo, :] = jnp.where(i == 0, head_ref[...], uh_ref[...])
    s_ref[halo:, :] = u_ref[...]
    first = halo - (conv_w - 1)
    n_win = (rows + halo) // SUBLANES

    y_ref[...] = x_ref[...] + jnp.dot(o_ref[...], w_ref[0:da, :], preferred_element_type=F32)
    for ci in range(tm // rows):
        r0 = ci * rows
        for lb in range(c // LANES):
            lanes = slice(lb * LANES, (lb + 1) * LANES)
            win = s_ref[r0:r0 + rows + halo, lanes].reshape(n_win, SUBLANES, LANES)
            yc_ref[r0:r0 + rows, lanes] = _conv_tiles(win, cwb_ref, lanes, conv_w, first, rows // SUBLANES)
        y = yc_ref[r0:r0 + rows, :] + cb_ref[...]
        c_ref[r0:r0 + rows, :] = _silu(_layernorm(y, lg_ref[...], lb_ref[...])).astype(c_ref.dtype)
    y_ref[...] += jnp.dot(c_ref[...], w_ref[da:, :], preferred_element_type=F32)


def conv_merge(kvu, head, conv_w, conv_b, ln_g, ln_b, o_n, x, w_o, tm):
    _, b, s, c = kvu.shape
    d = x.shape[-1]
    halo = head.shape[0]
    w = conv_w.shape[0]
    per = tm // halo
    vec = lambda a: a.reshape(1, -1)
    const = lambda bi, i: (0, 0)
    blk = lambda width: pl.BlockSpec((None, tm, width), lambda bi, i: (bi, i, 0))
    cwb = jnp.broadcast_to(conv_w[:, None, :], (w, SUBLANES, c))
    return pl.pallas_call(
        functools.partial(_conv_merge_kernel, conv_w=w, halo=halo, rows=CONV_ROWS),
        grid=(b, s // tm),
        in_specs=[pl.BlockSpec((None, None, tm, c), lambda bi, i: (2, bi, i, 0)),
                  pl.BlockSpec((None, None, halo, c), lambda bi, i: (2, bi, jnp.maximum(i * per - 1, 0), 0)),
                  pl.BlockSpec((halo, c), const),
                  pl.BlockSpec((w, SUBLANES, c), lambda bi, i: (0, 0, 0)),
                  pl.BlockSpec((1, c), const), pl.BlockSpec((1, c), const), pl.BlockSpec((1, c), const),
                  blk(c), blk(d),
                  pl.BlockSpec((d, d), const, pipeline_mode=pl.Buffered(1))],
        out_specs=blk(d),
        out_shape=jax.ShapeDtypeStruct((b, s, d), F32),
        scratch_shapes=[pltpu.VMEM((tm + halo, c), F32), pltpu.VMEM((tm, c), F32), pltpu.VMEM((tm, c), BF16)],
        compiler_params=_cparams("arbitrary", "arbitrary"),
        name="conv_merge",
    )(kvu, kvu, head, cwb, vec(conv_b), vec(ln_g), vec(ln_b), o_n, x, w_o)


def _small_conv_kernel(za_ref, zg_ref, st_ref, cw_ref, cb_ref, lg_ref, lb_ref, u_ref, c_ref, s_ref,
                       *, conv_w, n_sample):
    n_meta = za_ref.shape[0] - n_sample
    u = za_ref[...] * jax.nn.sigmoid(zg_ref[...])
    u_ref[...] = u
    y = u[:n_sample] * cw_ref[conv_w - 1:conv_w, :]
    for w in range(conv_w - 1):
        y = y + st_ref[w] * cw_ref[w:w + 1, :]
    y_s = y + cb_ref[...]
    hist = s_ref.shape[0] - n_meta
    s_ref[0:hist, :] = jnp.zeros((hist, s_ref.shape[1]), F32)
    s_ref[hist:, :] = u[n_sample:]
    y_m = _conv_rows(s_ref, hist - (conv_w - 1), n_meta, cw_ref, conv_w) + cb_ref[...]
    yy = jnp.concatenate([y_s, y_m], axis=0)
    c_ref[...] = _silu(_layernorm(yy, lg_ref[...], lb_ref[...])).astype(c_ref.dtype)


def small_conv_module(za, zg, state_t, conv_w, conv_b, ln_g, ln_b, n_sample):
    m, c = za.shape
    w = conv_w.shape[0]
    n_meta = m - n_sample
    hist = SUBLANES * ((w - 1 + SUBLANES - 1) // SUBLANES)
    vec = lambda a: a.reshape(1, -1)
    return pl.pallas_call(
        functools.partial(_small_conv_kernel, conv_w=w, n_sample=n_sample),
        out_shape=[jax.ShapeDtypeStruct((m, c), F32), jax.ShapeDtypeStruct((m, c), BF16)],
        scratch_shapes=[pltpu.VMEM((hist + n_meta, c), F32)],
        name="small_conv_module",
    )(za, zg, state_t, conv_w, vec(conv_b), vec(ln_g), vec(ln_b))


def _gelu(x):
    return 0.5 * x * (1.0 + lax.erf(x * math.sqrt(0.5)))


def _ffn_kernel(x_ref, g2_ref, gf_ref, wa_ref, wg_ref, wd_ref, cwa_ref, cwg_ref, cba_ref, cbg_ref,
                ha_ref, hg_ref, y_ref, ta_ref, tg_ref,
                xn_ref, ea_ref, eg_ref, ca_ref, cg_ref, *, blocks_per_seq, conv_w, n_sub):
    i = pl.program_id(0)
    f = pl.program_id(1)
    n_f = pl.num_programs(1)
    tm = x_ref.shape[0]
    tf = wa_ref.shape[1]
    ts = tf // n_sub
    pad = SUBLANES

    @pl.when(f == 0)
    def _():
        x = x_ref[...]
        xn_ref[...] = _rms(x, g2_ref[...]).astype(BF16)
        y_ref[...] = x

    xn = xn_ref[...]
    first = (i % blocks_per_seq) == 0
    base = pad - (conv_w - 1)
    down = None
    for sc in range(n_sub):
        cols = slice(sc * ts, (sc + 1) * ts)
        up_a = jnp.dot(xn, wa_ref[:, cols], preferred_element_type=F32)
        up_g = jnp.dot(xn, wg_ref[:, cols], preferred_element_type=F32)
        ta_ref[:, cols] = up_a[tm - pad:]
        tg_ref[:, cols] = up_g[tm - pad:]
        ea_ref[sc, 0:pad, :] = jnp.where(first, ha_ref[:, cols], ca_ref[f, :, cols])
        eg_ref[sc, 0:pad, :] = jnp.where(first, hg_ref[:, cols], cg_ref[f, :, cols])
        ea_ref[sc, pad:, :] = up_a
        eg_ref[sc, pad:, :] = up_g
        ca_ref[f, :, cols] = up_a[tm - pad:]
        cg_ref[f, :, cols] = up_g[tm - pad:]
        conv_a = _conv_rows(ea_ref.at[sc], base, tm, cwa_ref.at[:, cols], conv_w) + cba_ref[:, cols]
        conv_g = _conv_rows(eg_ref.at[sc], base, tm, cwg_ref.at[:, cols], conv_w) + cbg_ref[:, cols]
        hidden = (_gelu(conv_g) * conv_a).astype(BF16)
        d = jnp.dot(hidden, wd_ref[cols, :], preferred_element_type=F32)
        down = d if down is None else down + d
    y_ref[...] += down

    @pl.when(f == n_f - 1)
    def _():
        y_ref[...] = _rms(y_ref[...], gf_ref[...])


def prompt_ffn(x, norm2_g, final_g, wab, wd, cwa, cwg, cba, cbg, head_a, head_g, rows_per_seq,
               tm, tf):
    m, d = x.shape
    ffp = wab.shape[-1]
    n_f = ffp // tf
    bps = rows_per_seq // tm
    w = cwa.shape[0]
    ts = tf // FF_SUB
    row = lambda i, f: (i, 0)
    const = lambda i, f: (0, 0)
    col = lambda i, f: (0, f)
    tail = pl.BlockSpec((None, SUBLANES, tf), lambda i, f: (i, 0, f))
    return pl.pallas_call(
        functools.partial(_ffn_kernel, blocks_per_seq=bps, conv_w=w, n_sub=FF_SUB),
        grid=(m // tm, n_f),
        in_specs=[pl.BlockSpec((tm, d), row, pipeline_mode=pl.Buffered(1)),
                  pl.BlockSpec((1, d), const), pl.BlockSpec((1, d), const),
                  pl.BlockSpec((None, d, tf), lambda i, f: (0, 0, f)),
                  pl.BlockSpec((None, d, tf), lambda i, f: (1, 0, f)),
                  pl.BlockSpec((tf, d), lambda i, f: (f, 0)),
                  pl.BlockSpec((w, tf), col), pl.BlockSpec((w, tf), col),
                  pl.BlockSpec((1, tf), col), pl.BlockSpec((1, tf), col),
                  pl.BlockSpec((SUBLANES, tf), col), pl.BlockSpec((SUBLANES, tf), col)],
        out_specs=[pl.BlockSpec((tm, d), row), tail, tail],
        out_shape=[jax.ShapeDtypeStruct((m, d), F32),
                   jax.ShapeDtypeStruct((m // tm, SUBLANES, ffp), F32),
                   jax.ShapeDtypeStruct((m // tm, SUBLANES, ffp), F32)],
        scratch_shapes=[pltpu.VMEM((tm, d), BF16),
                        pltpu.VMEM((FF_SUB, tm + SUBLANES, ts), F32),
                        pltpu.VMEM((FF_SUB, tm + SUBLANES, ts), F32),
                        pltpu.VMEM((n_f, SUBLANES, tf), F32), pltpu.VMEM((n_f, SUBLANES, tf), F32)],
        compiler_params=_cparams("arbitrary", "arbitrary"),
        name="prompt_ffn",
    )(x, norm2_g.reshape(1, -1), final_g.reshape(1, -1), wab, wab, wd, cwa, cwg,
      cba.reshape(1, -1), cbg.reshape(1, -1), head_a, head_g)


def _small_gate_kernel(ua_ref, ug_ref, sa_ref, sg_ref, cwa_ref, cwg_ref, cba_ref, cbg_ref, h_ref,
                       *, conv_w):
    def conv(u_ref, s_ref, cw_ref, cb_ref):
        y = u_ref[...] * cw_ref[conv_w - 1:conv_w, :] + cb_ref[...]
        for w in range(conv_w - 1):
            y = y + s_ref[w] * cw_ref[w:w + 1, :]
        return y

    a = conv(ua_ref, sa_ref, cwa_ref, cba_ref)
    g = conv(ug_ref, sg_ref, cwg_ref, cbg_ref)
    h_ref[...] = (_gelu(g) * a).astype(h_ref.dtype)


def small_gate(up_a, up_g, st_a, st_g, cwa, cwg, cba, cbg):
    return pl.pallas_call(
        functools.partial(_small_gate_kernel, conv_w=cwa.shape[0]),
        out_shape=jax.ShapeDtypeStruct(up_a.shape, BF16),
        name="small_gate",
    )(up_a, up_g, st_a, st_g, cwa, cwg, cba.reshape(1, -1), cbg.reshape(1, -1))


def _residual_norm_kernel(x_ref, d_ref, g_ref, y_ref):
    y_ref[...] = _rms(x_ref[...] + d_ref[...], g_ref[...])


def residual_norm(x, delta, g):
    return pl.pallas_call(
        _residual_norm_kernel,
        out_shape=jax.ShapeDtypeStruct(x.shape, F32),
        name="residual_norm",
    )(x, delta, g.reshape(1, -1))


def kernel(x_prompt, x_sample, cache_k, cache_v, page_table, state_conv, state_ffn_conv, meta_tokens,
           rel_bias, norm1_g, w_in, lambda_q1, lambda_k1, lambda_q2, lambda_k2, subln_g, conv_w, conv_b,
           conv_ln_g, conv_ln_b, w_o, norm2_g, ffn_w_up, ffn_conv_w, ffn_conv_b, ffn_w_down,
           final_norm_g):
    depth = w_in.shape[0]
    assert depth == 1, "single-layer step only"
    b, s, d = x_prompt.shape
    bd, t_dec = x_sample.shape[:2]
    assert t_dec == 1
    n_meta = meta_tokens.shape[0]
    n_pool, page, n_heads, e = cache_k.shape[1:]
    dv = cache_v.shape[-1]
    dk = e // 2
    d_attn = n_heads * dv
    d_conv = conv_w.shape[-1]
    cw_len = conv_w.shape[1]
    fw_len = ffn_conv_w.shape[1]
    d_ff = ffn_w_down.shape[1]
    n_buckets = rel_bias.shape[0]
    t = ATTN_T
    tm = ROW_TILE
    tf = FF_TILE
    assert e == dv == LANES and d_conv == d_attn and w_in.shape[-1] == 5 * d_attn
    assert s % tm == 0 and s % t == 0 and n_meta <= t and page_table.shape[1] % DEC_PAGES == 0
    assert s % FF_ROWS == 0 and (b * s) % IN_ROWS == 0
    me = n_buckets // 2
    sat = math.ceil(me * (MAX_DISTANCE / me) ** ((n_buckets - 1 - me) / (n_buckets - me))) + 1
    assert t + 1 >= sat and page + 1 >= sat
    halo = SUBLANES * ((cw_len - 1 + SUBLANES - 1) // SUBLANES)
    assert n_meta <= halo and tm % halo == 0 and tm % CONV_ROWS == 0 and fw_len - 1 <= SUBLANES

    ffp = tf * ((d_ff + tf - 1) // tf)
    pad_c = lambda a: jnp.pad(a, ((0, 0), (0, ffp - d_ff)))
    cwa, cwg = pad_c(ffn_conv_w[0, :, :d_ff]), pad_c(ffn_conv_w[0, :, d_ff:])
    cba, cbg = pad_c(ffn_conv_b[:, :d_ff])[0], pad_c(ffn_conv_b[:, d_ff:])[0]
    lamv = jnp.stack([lambda_q1[0], lambda_k1[0], lambda_q2[0], lambda_k2[0]]).astype(F32)
    q_scale = dk ** -0.5 * LOG2E

    ar = lambda n: jnp.arange(n, dtype=jnp.int32)
    by_rel = _bias_tile(rel_bias, ar(2 * t)[None, :])[:, 0, :]
    d0, d1 = toeplitz_tiles(jnp.concatenate([jnp.full((n_heads, t - 1), NEG, F32), by_rel[:, :t]], axis=1),
                            by_rel[:, 1:], t)
    dm = _bias_tile(rel_bias, n_meta + ar(t)[:, None] - ar(n_meta)[None, :])
    dmeta = _bias_tile(rel_bias, ar(n_meta)[:, None] - ar(n_meta)[None, :])
    stack2 = lambda a: jnp.concatenate([a, a], axis=0)
    same_head = (ar(n_heads)[:, None, None] == ar(n_heads)[None, None, :])
    blast = _bias_tile(rel_bias, (page - ar(page))[None, :])[:, 0, :]
    mb = stack2(jnp.broadcast_to(jnp.where(same_head, 0.0, NEG), (n_heads, page, n_heads))
                ).reshape(2 * n_heads, page * n_heads)
    mbl = stack2(jnp.where(same_head, blast[:, :, None], NEG)).reshape(2 * n_heads, page * n_heads)
    b0 = stack2(_bias_tile(rel_bias, jnp.zeros((1, 1), jnp.int32))[:, 0, :])

    x_small = jnp.concatenate([x_sample[:, 0, :], meta_tokens.astype(F32)], axis=0)
    xn_small = rmsnorm_cast(x_small, norm1_g[0], BF16)
    z, w_in_b = mm_cast(xn_small, w_in[0], 512)
    q_small = (z[:, :d_attn] * q_scale).astype(BF16)
    k_small = z[:, d_attn:2 * d_attn]
    v_small = z[:, 2 * d_attn:3 * d_attn]
    state_t = jnp.transpose(state_conv[0], (1, 0, 2))
    u_small, c_small = small_conv_module(z[:, 3 * d_attn:4 * d_attn], z[:, 4 * d_attn:], state_t,
                                         conv_w[0], conv_b[0], conv_ln_g[0], conv_ln_b[0], bd)
    o_meta = meta_attention(q_small[bd:], k_small[bd:], v_small[bd:], dmeta, lamv, subln_g[0], n_heads)
    hd = lambda a: a.reshape(bd, n_heads, -1)
    o_dec = decode_attention(page_table, hd(q_small[:bd]), hd(k_small[:bd]), hd(v_small[:bd]),
                             cache_k[0], cache_v[0], mb, mbl, b0, lamv, subln_g[0], DEC_PAGES)
    o_small = jnp.concatenate([o_dec.reshape(bd, d_attn), o_meta], axis=0)
    h_delta, w_o_b = mm_cast(jnp.concatenate([o_small, c_small], axis=1), w_o[0], 512)
    h1_small, xn2_small = add_rmsnorm(x_small, h_delta, norm2_g[0])
    up_small, wab_b = up_cast(xn2_small, ffn_w_up[0], d_ff, ffp, tf)
    up_a_small, up_g_small = up_small[0], up_small[1]
    st = jnp.transpose(state_ffn_conv[0], (1, 0, 2))
    st_pad = lambda a: pad_c(a.reshape(-1, d_ff)).reshape(fw_len - 1, bd, ffp)
    hid = small_gate(up_a_small[:bd], up_g_small[:bd], st_pad(st[:, :, :d_ff]), st_pad(st[:, :, d_ff:]),
                     cwa, cwg, cba, cbg)
    down, wd_b = mm_cast(hid, ffn_w_down[0], 256, k_pad=ffp)
    y_sample = residual_norm(h1_small[:bd], down, final_norm_g)

    xp = x_prompt.reshape(b * s, d)
    qkv_b, kvu = prompt_inproj(xp, norm1_g[0], w_in_b, d_attn, q_scale, IN_ROWS)
    qkv_b = qkv_b.reshape(3, b, s, d_attn)
    kvu = kvu.reshape(3, b, s, d_attn)
    o_p = prompt_attention(qkv_b, k_small[bd:].astype(BF16), v_small[bd:].astype(BF16),
                           d0, d1, dm, lamv, subln_g[0], n_heads, t, ATTN_HEADS)
    head_u = jnp.concatenate([jnp.zeros((halo - n_meta, d_conv), F32), u_small[bd:]], axis=0)
    h1_p = conv_merge(kvu, head_u, conv_w[0], conv_b[0], conv_ln_g[0], conv_ln_b[0], o_p, x_prompt, w_o_b,
                      tm).reshape(b * s, d)
    n_t = fw_len - 1
    head_rows = lambda a: jnp.concatenate([jnp.zeros((SUBLANES - n_t, ffp), F32), a[-n_t:]], axis=0)
    y_p, tail_a, tail_g = prompt_ffn(h1_p, norm2_g[0], final_norm_g, wab_b, wd_b, cwa, cwg, cba, cbg,
                                     head_rows(up_a_small[bd:]), head_rows(up_g_small[bd:]), s, FF_ROWS, tf)

    k_prompt = kv_rows_with_meta(kvu, 0, k_small[bd:], n_heads, tm)[None]
    v_prompt = kv_rows_with_meta(kvu, 1, v_small[bd:], n_heads, tm)[None]
    conv_prompt = kvu[2][None, :, s - (cw_len - 1):]
    bps = s // FF_ROWS
    last_blk = lambda a: a[bps - 1::bps, SUBLANES - n_t:, :d_ff]
    ffn_prompt = jnp.concatenate([last_blk(tail_a), last_blk(tail_g)], axis=-1)[None]
    k_sample = k_small[:bd].reshape(1, bd, 1, n_heads, e)
    v_sample = v_small[:bd].reshape(1, bd, 1, n_heads, dv)
    conv_sample = jnp.concatenate([state_conv[0][:, 1:], u_small[:bd, None, :]], axis=1)[None]
    up_s = jnp.concatenate([up_a_small[:bd, :d_ff], up_g_small[:bd, :d_ff]], axis=-1)
    ffn_sample = jnp.concatenate([state_ffn_conv[0][:, 1:], up_s[:, None, :]], axis=1)[None]
    return (y_p.reshape(b, s, d), y_sample.reshape(bd, 1, d), k_prompt, v_prompt, conv_prompt, ffn_prompt,
            k_sample, v_sample, conv_sample, ffn_sample)
```

```python
import functools
import math

import jax
import jax.numpy as jnp
from jax import lax
from jax.experimental import pallas as pl
from jax.experimental.pallas import tpu as pltpu

F32 = jnp.float32
BF16 = jnp.bfloat16

EPS = 1e-6
MAX_DISTANCE = 128
LAMBDA_INIT_0 = 0.8 - 0.6 * math.exp(-0.3 * 0)
NEG = -1e30
LOG2E = math.log2(math.e)

LANES = 128
SUBLANES = 8
VMEM_LIMIT = 56 * 1024 * 1024

ATTN_T = 512
ATTN_HEADS = 2
ROW_TILE = 512
IN_ROWS = 1024
FF_ROWS = 512
FF_DEC_PAGES = 8
FF_TILE = 512
FF_SUB = 2
CONV_ROWS = 64
DEC_PAGES = 16


def _cparams(*sem):
    return pltpu.CompilerParams(dimension_semantics=sem, vmem_limit_bytes=VMEM_LIMIT)


def _rms(x, g):
    return x * lax.rsqrt(jnp.mean(x * x, axis=-1, keepdims=True) + EPS) * g


def _layernorm(y, g, b):
    mu = jnp.mean(y, axis=-1, keepdims=True)
    d = y - mu
    var = jnp.mean(d * d, axis=-1, keepdims=True)
    return d * lax.rsqrt(var + EPS) * g + b


def _silu(x):
    return x * jax.nn.sigmoid(x)


def _lambda(lamv_ref):
    a = jnp.sum(lamv_ref[0:1, :] * lamv_ref[1:2, :], axis=-1, keepdims=True)
    b = jnp.sum(lamv_ref[2:3, :] * lamv_ref[3:4, :], axis=-1, keepdims=True)
    return jnp.exp(a) - jnp.exp(b) + LAMBDA_INIT_0


def _t5_bucket(rel, n_buckets):
    n = jnp.maximum(rel, 0)
    max_exact = n_buckets // 2
    nf = jnp.maximum(n, 1).astype(F32)
    large = max_exact + (jnp.log(nf / max_exact) / math.log(MAX_DISTANCE / max_exact)
                         * (n_buckets - max_exact)).astype(jnp.int32)
    large = jnp.minimum(large, n_buckets - 1)
    return jnp.where(n < max_exact, n, large)


def _bias_tile(rel_bias, rel):
    nb = rel_bias.shape[0]
    onehot = (_t5_bucket(rel, nb)[..., None] == jnp.arange(nb, dtype=jnp.int32)).astype(F32)
    table = (rel_bias - rel_bias[nb - 1]).astype(F32) * LOG2E
    b = jnp.einsum("rcn,nh->hrc", onehot, table, precision=lax.Precision.HIGHEST)
    return jnp.where((rel >= 0)[None], b, NEG)


def _toeplitz_kernel(u0_ref, u1_ref, d0_ref, d1_ref):
    t = d0_ref.shape[0]
    for u_ref, d_ref in ((u0_ref, d0_ref), (u1_ref, d1_ref)):
        x = jnp.broadcast_to(u_ref[...], (t, 2 * t))
        d_ref[...] = pltpu.roll(x, 0, 1, stride=1, stride_axis=0)[:, :t]


def toeplitz_tiles(w0, w1, t):
    h = w0.shape[0]

    def wrap(w):
        return jnp.concatenate([w[:, :t][:, ::-1], jnp.zeros((h, 1), F32), w[:, t:][:, ::-1]],
                               axis=1).reshape(h, 1, 2 * t)

    vec = pl.BlockSpec((None, 1, 2 * t), lambda hh: (hh, 0, 0))
    tile = pl.BlockSpec((None, t, t), lambda hh: (hh, 0, 0))
    shape = jax.ShapeDtypeStruct((h, t, t), F32)
    return pl.pallas_call(
        _toeplitz_kernel, grid=(h,), in_specs=[vec, vec], out_specs=[tile, tile],
        out_shape=[shape, shape], name="toeplitz_tiles",
    )(wrap(w0), wrap(w1))


def _rmsnorm_cast_kernel(x_ref, g_ref, o_ref):
    o_ref[...] = _rms(x_ref[...], g_ref[...]).astype(o_ref.dtype)


def rmsnorm_cast(x, g, dtype):
    return pl.pallas_call(
        _rmsnorm_cast_kernel,
        out_shape=jax.ShapeDtypeStruct(x.shape, dtype),
        name="rmsnorm_cast",
    )(x, g.reshape(1, -1))


def _mm_cast_kernel(x_ref, w_ref, o_ref, wb_ref):
    k = w_ref.shape[0]
    wb = w_ref[...].astype(BF16)
    wb_ref[0:k, :] = wb
    if wb_ref.shape[0] > k:
        wb_ref[k:, :] = jnp.zeros((wb_ref.shape[0] - k, wb_ref.shape[1]), BF16)
    o_ref[...] = jnp.dot(x_ref[:, 0:k], wb, preferred_element_type=F32)


def mm_cast(x, w, tn, k_pad=None):
    m, kx = x.shape
    k, n = w.shape
    kp = k if k_pad is None else k_pad
    return pl.pallas_call(
        _mm_cast_kernel,
        grid=(n // tn,),
        in_specs=[pl.BlockSpec((m, kx), lambda j: (0, 0)),
                  pl.BlockSpec((k, tn), lambda j: (0, j))],
        out_specs=[pl.BlockSpec((m, tn), lambda j: (0, j)), pl.BlockSpec((kp, tn), lambda j: (0, j))],
        out_shape=[jax.ShapeDtypeStruct((m, n), F32), jax.ShapeDtypeStruct((kp, n), BF16)],
        compiler_params=_cparams("arbitrary"),
        name="mm_cast",
    )(x, w)


def _up_cast_kernel(x_ref, w_ref, o_ref, wb_ref, *, rem):
    hf = pl.program_id(0)
    j = pl.program_id(1)
    last = pl.num_programs(1) - 1
    d, tn = w_ref.shape

    def emit(w):
        wb = w.astype(BF16)
        wb_ref[...] = wb
        o_ref[...] = jnp.dot(x_ref[...], wb, preferred_element_type=F32)

    @pl.when(j < last)
    def _():
        emit(w_ref[...])

    @pl.when((j == last) & (hf == 0))
    def _():
        lane = lax.broadcasted_iota(jnp.int32, (d, tn), 1)
        emit(jnp.where(lane < rem, w_ref[...], 0.0))

    @pl.when((j == last) & (hf == 1))
    def _():
        emit(jnp.concatenate([w_ref[:, tn - rem:], jnp.zeros((d, tn - rem), F32)], axis=1))


def up_cast(x, w_up, d_ff, ffp, tn):
    m, d = x.shape
    n_blk = ffp // tn
    rem = d_ff - (n_blk - 1) * tn
    assert 0 < rem <= tn and rem % LANES == 0 and d_ff >= tn
    el = pl.Element
    return pl.pallas_call(
        functools.partial(_up_cast_kernel, rem=rem),
        grid=(2, n_blk),
        in_specs=[pl.BlockSpec((m, d), lambda hf, j: (0, 0)),
                  pl.BlockSpec((el(d), el(tn)),
                               lambda hf, j: (0, pl.multiple_of(
                                   jnp.minimum(hf * d_ff + j * tn, 2 * d_ff - tn), LANES)))],
        out_specs=[pl.BlockSpec((None, m, tn), lambda hf, j: (hf, 0, j)),
                   pl.BlockSpec((None, d, tn), lambda hf, j: (hf, 0, j))],
        out_shape=[jax.ShapeDtypeStruct((2, m, ffp), F32), jax.ShapeDtypeStruct((2, d, ffp), BF16)],
        compiler_params=_cparams("arbitrary", "arbitrary"),
        name="up_cast",
    )(x, w_up)


def _cast_pad_kernel(w_ref, wb_ref):
    k = w_ref.shape[0]
    wb_ref[0:k, :] = w_ref[...].astype(BF16)
    if wb_ref.shape[0] > k:
        wb_ref[k:, :] = jnp.zeros((wb_ref.shape[0] - k, wb_ref.shape[1]), BF16)


def cast_pad(w, k_pad, tn):
    k, n = w.shape
    return pl.pallas_call(
        _cast_pad_kernel,
        grid=(n // tn,),
        in_specs=[pl.BlockSpec((k, tn), lambda j: (0, j))],
        out_specs=pl.BlockSpec((k_pad, tn), lambda j: (0, j)),
        out_shape=jax.ShapeDtypeStruct((k_pad, n), BF16),
        compiler_params=_cparams("arbitrary"),
        name="cast_pad",
    )(w)


def _mm_bf16_kernel(x_ref, w_ref, o_ref):
    o_ref[...] = jnp.dot(x_ref[...], w_ref[...], preferred_element_type=F32)


def mm_bf16(x, w3, slab, tn):
    m, k = x.shape
    n = w3.shape[-1]
    return pl.pallas_call(
        _mm_bf16_kernel,
        grid=(n // tn,),
        in_specs=[pl.BlockSpec((m, k), lambda j: (0, 0)),
                  pl.BlockSpec((None, k, tn), lambda j: (slab, 0, j))],
        out_specs=pl.BlockSpec((m, tn), lambda j: (0, j)),
        out_shape=jax.ShapeDtypeStruct((m, n), F32),
        compiler_params=_cparams("arbitrary"),
        name="mm_bf16",
    )(x, w3)


def _add_rmsnorm_kernel(x_ref, d_ref, g_ref, h_ref, xn_ref):
    h = x_ref[...] + d_ref[...]
    h_ref[...] = h
    xn_ref[...] = _rms(h, g_ref[...]).astype(xn_ref.dtype)


def add_rmsnorm(x, delta, g):
    return pl.pallas_call(
        _add_rmsnorm_kernel,
        out_shape=[jax.ShapeDtypeStruct(x.shape, F32), jax.ShapeDtypeStruct(x.shape, BF16)],
        name="add_rmsnorm",
    )(x, delta, g.reshape(1, -1))


def _inproj_kernel(x_ref, g_ref, w_ref, wg_ref, ob_ref, of_ref, xn_ref, *, q_scale):
    j = pl.program_id(1)

    @pl.when(j == 0)
    def _():
        xn_ref[...] = _rms(x_ref[...], g_ref[...]).astype(BF16)

    z = jnp.dot(xn_ref[...], w_ref[...], preferred_element_type=F32)

    @pl.when(j == 0)
    def _():
        ob_ref[...] = (z * q_scale).astype(BF16)

    @pl.when((j == 1) | (j == 2))
    def _():
        of_ref[...] = z
        ob_ref[...] = z.astype(BF16)

    @pl.when(j == 3)
    def _():
        gate = jnp.dot(xn_ref[...], wg_ref[...], preferred_element_type=F32)
        of_ref[...] = z * jax.nn.sigmoid(gate)


def prompt_inproj(x, g, w_in, d_attn, q_scale, tm):
    m, d = x.shape
    c = d_attn
    return pl.pallas_call(
        functools.partial(_inproj_kernel, q_scale=q_scale),
        grid=(m // tm, 4),
        in_specs=[pl.BlockSpec((tm, d), lambda i, j: (i, 0)),
                  pl.BlockSpec((1, d), lambda i, j: (0, 0)),
                  pl.BlockSpec((d, c), lambda i, j: (0, j)),
                  pl.BlockSpec((d, c), lambda i, j: (0, 4), pipeline_mode=pl.Buffered(1))],
        out_specs=[pl.BlockSpec((None, tm, c), lambda i, j: (jnp.minimum(j, 2), i, 0)),
                   pl.BlockSpec((None, tm, c), lambda i, j: (jnp.maximum(j - 1, 0), i, 0))],
        out_shape=[jax.ShapeDtypeStruct((3, m, c), BF16), jax.ShapeDtypeStruct((3, m, c), F32)],
        scratch_shapes=[pltpu.VMEM((tm, d), BF16)],
        compiler_params=_cparams("arbitrary", "arbitrary"),
        name="prompt_inproj",
    )(x, g.reshape(1, -1), w_in, w_in)


def _kv_rows_kernel(x_ref, meta_ref, o_ref, *, n_meta, n_heads):
    i = pl.program_id(1)
    n_blk = pl.num_programs(1)
    tm = o_ref.shape[0]
    heads = lambda a: a.reshape(a.shape[0], n_heads, a.shape[1] // n_heads)

    @pl.when(i == 0)
    def _():
        o_ref[0:n_meta] = heads(meta_ref[...])
        o_ref[n_meta:] = heads(x_ref[0, 0, 0:tm - n_meta, :])

    @pl.when((i > 0) & (i < n_blk - 1))
    def _():
        o_ref[...] = heads(x_ref[0, 0])

    @pl.when(i == n_blk - 1)
    def _():
        o_ref[0:n_meta] = heads(x_ref[0, 0, tm - n_meta:tm, :])
        o_ref[n_meta:] = jnp.zeros((tm - n_meta,) + o_ref.shape[1:], o_ref.dtype)


def kv_rows_with_meta(kvu, slab, meta_rows, n_heads, tm):
    _, b, s, c = kvu.shape
    n_meta = meta_rows.shape[0]
    n_blk = s // tm + 1
    assert n_meta % SUBLANES == 0 and n_meta < tm
    el = pl.Element
    return pl.pallas_call(
        functools.partial(_kv_rows_kernel, n_meta=n_meta, n_heads=n_heads),
        grid=(b, n_blk),
        in_specs=[pl.BlockSpec((el(1), el(1), el(tm), el(c)),
                               lambda bi, i: (slab, bi, pl.multiple_of(
                                   jnp.clip(i * tm - n_meta, 0, s - tm), SUBLANES), 0)),
                  pl.BlockSpec((n_meta, c), lambda bi, i: (0, 0))],
        out_specs=pl.BlockSpec((None, tm, n_heads, c // n_heads), lambda bi, i: (bi, i, 0, 0)),
        out_shape=jax.ShapeDtypeStruct((b, n_meta + s, n_heads, c // n_heads), F32),
        compiler_params=_cparams("arbitrary", "arbitrary"),
        name="kv_rows_with_meta",
    )(kvu, meta_rows)


def _stack_q(q, dk):
    lane = lax.broadcasted_iota(jnp.int32, q.shape, 1)
    zero = jnp.zeros_like(q)
    return jnp.concatenate([jnp.where(lane < dk, q, zero), jnp.where(lane >= dk, q, zero)], axis=0)


def _scores(qq, kblk):
    return lax.dot_general(qq, kblk, (((1,), (1,)), ((), ())), preferred_element_type=F32)


def _add_tile(s, tile):
    t, c = tile.shape
    return (s.reshape(2, t, c) + tile[None]).reshape(2 * t, c)


def _with_ones(v):
    return jnp.concatenate([v, jnp.ones_like(v)], axis=-1)


def _softmax_update(s, vx, carry):
    return _softmax_update_with_max(s, jnp.max(s, axis=-1, keepdims=True), vx, carry)


def _softmax_update_with_max(s, s_max, vx, carry):
    m, acc = carry
    m_new = jnp.maximum(m, s_max)
    alpha = jnp.exp2(m - m_new)
    p = jnp.exp2(s - m_new).astype(BF16)
    return m_new, alpha * acc + jnp.dot(p, vx, preferred_element_type=F32)


def _combine_heads_out(acc, lam, g, t):
    dv = acc.shape[-1] // 2
    o = acc[:t, :dv] / acc[:t, dv:] - lam * (acc[t:, :dv] / acc[t:, dv:])
    return _rms(o, g) * (1.0 - LAMBDA_INIT_0)


def _prompt_attn_kernel(q_ref, k_ref, v_ref, km_ref, vm_ref, d0_ref, d1_ref, dm_ref,
                        lamv_ref, g_ref, o_ref, vx_ref, qa_ref, qb_ref, *, t, dk, n_grp):
    qi = pl.program_id(2)
    r = 2 * t
    e = 2 * dk
    dv = v_ref.shape[-1] // n_grp
    heads = range(n_grp)
    hs = lambda hh, w: slice(hh * w, (hh + 1) * w)

    @pl.when(qi == 0)
    def _():
        for hh in heads:
            vx_ref[hh, :, :dv] = v_ref[:, hs(hh, dv)]
            vx_ref[hh, :, dv:] = jnp.ones((v_ref.shape[0], dv), BF16)

    qq = [_stack_q(q_ref[:, hs(hh, e)], dk) for hh in heads]
    rows = lambda j: pl.ds(pl.multiple_of(j * t, t), t)
    kblk = lambda hh, j: k_ref[rows(j), hs(hh, e)]
    vxblk = lambda hh, j: vx_ref[hh, rows(j), :]

    n_far = jnp.maximum(qi - 1, 0)

    for hh in heads:
        qa_ref[hh] = qq[hh]
        qb_ref[hh] = qq[hh]

    def far_body(j, carry):
        out = []
        for hh in heads:
            m, acc = carry[hh]
            kb = kblk(hh, j)
            m_new = jnp.maximum(m, jnp.max(_scores(qa_ref[hh], kb), axis=-1, keepdims=True))
            alpha = jnp.exp2(m - m_new)
            p = jnp.exp2(_scores(qb_ref[hh], kb) - m_new).astype(BF16)
            out.append((m_new, alpha * acc + jnp.dot(p, vxblk(hh, j), preferred_element_type=F32)))
        return tuple(out)

    init = tuple((jnp.full((r, 1), NEG, F32), jnp.zeros((r, 2 * dv), F32)) for hh in heads)
    carry = lax.fori_loop(0, n_far, far_body, init)

    lam = _lambda(lamv_ref)
    for hh in heads:
        m, acc = carry[hh]
        s_prev = _scores(qq[hh], kblk(hh, n_far))
        s_sub = _add_tile(s_prev, jnp.where(qi >= 1, d1_ref[hh], NEG))
        s_diag = _add_tile(_scores(qq[hh], kblk(hh, qi)), d0_ref[hh])
        s_meta = _add_tile(_scores(qq[hh], km_ref[:, hs(hh, e)]), jnp.where(qi == 0, dm_ref[hh], 0.0))
        c = _softmax_update(s_meta, _with_ones(vm_ref[:, hs(hh, dv)]), (m, acc))
        c = _softmax_update(s_diag, vxblk(hh, qi), c)
        m, acc = _softmax_update(s_sub, vxblk(hh, n_far), c)
        o_ref[:, hs(hh, dv)] = _combine_heads_out(acc, lam, g_ref[...], t).astype(o_ref.dtype)


def prompt_attention(qkv, kmb, vmb, d0, d1, dm, lamv, subln_g, n_heads, t, n_grp):
    _, b, s, width = qkv.shape
    e = dv = width // n_heads
    nm = kmb.shape[0]
    ge, gv = n_grp * e, n_grp * dv
    tile = lambda c: pl.BlockSpec((n_grp, t, c), lambda bi, h, qi: (h, 0, 0))
    return pl.pallas_call(
        functools.partial(_prompt_attn_kernel, t=t, dk=e // 2, n_grp=n_grp),
        grid=(b, n_heads // n_grp, s // t),
        in_specs=[
            pl.BlockSpec((None, None, t, ge), lambda bi, h, qi: (0, bi, qi, h)),
            pl.BlockSpec((None, None, s, ge), lambda bi, h, qi: (1, bi, 0, h)),
            pl.BlockSpec((None, None, s, gv), lambda bi, h, qi: (2, bi, 0, h)),
            pl.BlockSpec((nm, ge), lambda bi, h, qi: (0, h)),
            pl.BlockSpec((nm, gv), lambda bi, h, qi: (0, h)),
            tile(t), tile(t), tile(nm),
            pl.BlockSpec((4, e // 2), lambda bi, h, qi: (0, 0)),
            pl.BlockSpec((1, dv), lambda bi, h, qi: (0, 0)),
        ],
        out_specs=pl.BlockSpec((None, t, gv), lambda bi, h, qi: (bi, qi, h)),
        out_shape=jax.ShapeDtypeStruct((b, s, n_heads * dv), BF16),
        scratch_shapes=[pltpu.VMEM((n_grp, s, 2 * dv), BF16),
                        pltpu.VMEM((n_grp, 2 * t, e), BF16), pltpu.VMEM((n_grp, 2 * t, e), BF16)],
        compiler_params=_cparams("arbitrary", "arbitrary", "arbitrary"),
        name="prompt_attention",
    )(qkv, qkv, qkv, kmb, vmb, d0, d1, dm, lamv, subln_g.reshape(1, -1))


def _meta_attn_kernel(q_ref, k_ref, v_ref, d_ref, lamv_ref, g_ref, o_ref, *, n_heads, dk):
    t = q_ref.shape[0]
    e = 2 * dk
    dv = v_ref.shape[-1] // n_heads
    lam = _lambda(lamv_ref)
    for h in range(n_heads):
        qq = _stack_q(q_ref[:, h * e:(h + 1) * e], dk)
        kblk = k_ref[:, h * e:(h + 1) * e].astype(BF16)
        vx = _with_ones(v_ref[:, h * dv:(h + 1) * dv].astype(BF16))
        carry = (jnp.full((2 * t, 1), NEG, F32), jnp.zeros((2 * t, 2 * dv), F32))
        m, acc = _softmax_update(_add_tile(_scores(qq, kblk), d_ref[h]), vx, carry)
        o_ref[:, h * dv:(h + 1) * dv] = _combine_heads_out(acc, lam, g_ref[...], t).astype(o_ref.dtype)


def meta_attention(q, k, v, dmeta, lamv, subln_g, n_heads):
    t = q.shape[0]
    dv = v.shape[-1] // n_heads
    return pl.pallas_call(
        functools.partial(_meta_attn_kernel, n_heads=n_heads, dk=k.shape[-1] // n_heads // 2),
        out_shape=jax.ShapeDtypeStruct((t, n_heads * dv), BF16),
        name="meta_attention",
    )(q, k, v, dmeta, lamv, subln_g.reshape(1, -1))


def _decode_init(q_ref, kn_ref, vn_ref, b0_ref, qq_ref, m_ref, l_ref, acc_ref, dk):
    qf = q_ref[...].astype(F32)
    lane = lax.broadcasted_iota(jnp.int32, qf.shape, 1)
    qq = jnp.concatenate([jnp.where(lane < dk, qf, 0.0), jnp.where(lane >= dk, qf, 0.0)], axis=0)
    qq_ref[...] = qq.astype(BF16)
    kn = jnp.concatenate([kn_ref[...], kn_ref[...]], axis=0)
    s_new = jnp.sum(qq * kn, axis=-1, keepdims=True) + b0_ref[...]
    m_ref[...] = s_new
    l_ref[...] = jnp.ones_like(s_new)
    acc_ref[...] = jnp.concatenate([vn_ref[...], vn_ref[...]], axis=0)


def _decode_pages(*args):
    for _ in _decode_pages_phases(*args):
        pass


def _decode_pages_phases(is_last_group, mb_ref, mbl_ref, k_refs, v_refs, qq_ref, m_ref, l_ref, acc_ref,
                         n_heads):
    qq = qq_ref[...]
    e = qq.shape[-1]
    n_p = len(k_refs)
    s_pages = []
    for p in range(n_p):
        bias = mb_ref[...]
        if p == n_p - 1:
            bias = jnp.where(is_last_group, mbl_ref[...], bias)
        rows = k_refs[p].shape[0] * n_heads
        s_pages.append(_scores(qq, k_refs[p][...].reshape(rows, e).astype(BF16)) + bias)
    yield
    m_old = m_ref[...]
    m_loc = s_pages[0]
    for s in s_pages[1:]:
        m_loc = jnp.maximum(m_loc, s)
    m_new = jnp.maximum(m_old, jnp.max(m_loc, axis=-1, keepdims=True))
    alpha = jnp.exp2(m_old - m_new)
    weights = [jnp.exp2(s - m_new) for s in s_pages]
    yield
    l_sum = None
    pv = None
    for p in range(n_p):
        w = weights[p]
        rows = v_refs[p].shape[0] * n_heads
        vblk = v_refs[p][...].reshape(rows, v_refs[p].shape[-1]).astype(BF16)
        d = jnp.dot(w.astype(BF16), vblk, preferred_element_type=F32)
        l_sum = w if l_sum is None else l_sum + w
        pv = d if pv is None else pv + d
    l_ref[...] = alpha * l_ref[...] + jnp.sum(l_sum, axis=-1, keepdims=True)
    acc_ref[...] = alpha * acc_ref[...] + pv
    m_ref[...] = m_new


def _decode_final(lamv_ref, g_ref, o_ref, l_ref, acc_ref, n_heads):
    acc, l = acc_ref[...], l_ref[...]
    o = acc[:n_heads] / l[:n_heads] - _lambda(lamv_ref) * (acc[n_heads:] / l[n_heads:])
    o_ref[...] = (_rms(o, g_ref[...]) * (1.0 - LAMBDA_INIT_0)).astype(o_ref.dtype)


def _decode_attn_kernel(pt_ref, q_ref, kn_ref, vn_ref, mb_ref, mbl_ref, b0_ref, lamv_ref, g_ref, *rest,
                        n_pages_step, n_heads, dk):
    k_refs = rest[:n_pages_step]
    v_refs = rest[n_pages_step:2 * n_pages_step]
    o_ref = rest[2 * n_pages_step]
    state = rest[2 * n_pages_step + 1:]
    del pt_ref
    g = pl.program_id(1)
    last = g == pl.num_programs(1) - 1

    @pl.when(g == 0)
    def _():
        _decode_init(q_ref, kn_ref, vn_ref, b0_ref, *state, dk)

    _decode_pages(last, mb_ref, mbl_ref, k_refs, v_refs, *state, n_heads)

    @pl.when(last)
    def _():
        _decode_final(lamv_ref, g_ref, o_ref, state[2], state[3], n_heads)


def decode_attention(page_table, q, k_new, v_new, cache_k, cache_v, mb, mbl, b0, lamv, subln_g,
                     pages_per_step):
    bd, n_pages = page_table.shape
    _, page, n_heads, e = cache_k.shape
    dv = cache_v.shape[-1]
    p_step = pages_per_step
    r = 2 * n_heads
    seq_blk = lambda w: pl.BlockSpec((None, n_heads, w), lambda b, g, pt: (b, 0, 0))
    const2 = lambda b, g, pt: (0, 0)

    def page_spec(p, w):
        return pl.BlockSpec((None, page, n_heads, w), lambda b, g, pt: (pt[b, g * p_step + p], 0, 0, 0))

    grid_spec = pltpu.PrefetchScalarGridSpec(
        num_scalar_prefetch=1,
        grid=(bd, n_pages // p_step),
        in_specs=[seq_blk(e), seq_blk(e), seq_blk(dv),
                  pl.BlockSpec((r, page * n_heads), const2),
                  pl.BlockSpec((r, page * n_heads), const2),
                  pl.BlockSpec((r, 1), const2),
                  pl.BlockSpec((4, e // 2), const2),
                  pl.BlockSpec((1, dv), const2)]
                 + [page_spec(p, e) for p in range(p_step)]
                 + [page_spec(p, dv) for p in range(p_step)],
        out_specs=seq_blk(dv),
        scratch_shapes=[pltpu.VMEM((r, e), BF16), pltpu.VMEM((r, 1), F32),
                        pltpu.VMEM((r, 1), F32), pltpu.VMEM((r, dv), F32)],
    )
    return pl.pallas_call(
        functools.partial(_decode_attn_kernel, n_pages_step=p_step, n_heads=n_heads, dk=e // 2),
        grid_spec=grid_spec,
        out_shape=jax.ShapeDtypeStruct((bd, n_heads, dv), BF16),
        compiler_params=_cparams("arbitrary", "arbitrary"),
        name="decode_attention",
    )(page_table, q, k_new, v_new, mb, mbl, b0, lamv, subln_g.reshape(1, -1),
      *([cache_k] * p_step), *([cache_v] * p_step))


def _conv_rows(src_ref, base, n_rows, cw_ref, conv_w):
    acc = src_ref[pl.ds(base, n_rows), :] * cw_ref[0:1, :]
    for w in range(1, conv_w):
        acc = acc + src_ref[pl.ds(base + w, n_rows), :] * cw_ref[w:w + 1, :]
    return acc


def _conv_tiles(win, cwb_ref, lanes, conv_w, first, n_out):
    y = None
    for k in range(SUBLANES):
        taps = [w for w in range(conv_w) if (first + w) % SUBLANES == k]
        if not taps:
            continue
        n_p = n_out if k == 0 else n_out + 1
        part = None
        for w in taps:
            j = (first + w) // SUBLANES
            term = win[j:j + n_p] * cwb_ref[w, :, lanes][None]
            part = term if part is None else part + term
        part = part.reshape(n_p * SUBLANES, part.shape[-1])
        part = part if k == 0 else part[k:k + n_out * SUBLANES]
        y = part if y is None else y + part
    return y


def _conv_merge_kernel(u_ref, uh_ref, head_ref, cwb_ref, cb_ref, lg_ref, lb_ref, o_ref, x_ref, w_ref,
                       y_ref, s_ref, yc_ref, c_ref, *, conv_w, halo, rows):
    i = pl.program_id(1)
    tm, c = u_ref.shape
    da = o_ref.shape[-1]
    s_ref[0:halo, :] = jnp.where(i == 0, head_ref[...], uh_ref[...])
    s_ref[halo:, :] = u_ref[...]
    first = halo - (conv_w - 1)
    n_win = (rows + halo) // SUBLANES

    y_ref[...] = x_ref[...] + jnp.dot(o_ref[...], w_ref[0:da, :], preferred_element_type=F32)
    for ci in range(tm // rows):
        r0 = ci * rows
        for lb in range(c // LANES):
            lanes = slice(lb * LANES, (lb + 1) * LANES)
            win = s_ref[r0:r0 + rows + halo, lanes].reshape(n_win, SUBLANES, LANES)
            yc_ref[r0:r0 + rows, lanes] = _conv_tiles(win, cwb_ref, lanes, conv_w, first, rows // SUBLANES)
        y = yc_ref[r0:r0 + rows, :] + cb_ref[...]
        c_ref[r0:r0 + rows, :] = _silu(_layernorm(y, lg_ref[...], lb_ref[...])).astype(c_ref.dtype)
    y_ref[...] += jnp.dot(c_ref[...], w_ref[da:, :], preferred_element_type=F32)


def conv_merge(kvu, head, conv_w, conv_b, ln_g, ln_b, o_n, x, w_o, tm):
    _, b, s, c = kvu.shape
    d = x.shape[-1]
    halo = head.shape[0]
    w = conv_w.shape[0]
    per = tm // halo
    vec = lambda a: a.reshape(1, -1)
    const = lambda bi, i: (0, 0)
    blk = lambda width: pl.BlockSpec((None, tm, width), lambda bi, i: (bi, i, 0))
    cwb = jnp.broadcast_to(conv_w[:, None, :], (w, SUBLANES, c))
    return pl.pallas_call(
        functools.partial(_conv_merge_kernel, conv_w=w, halo=halo, rows=CONV_ROWS),
        grid=(b, s // tm),
        in_specs=[pl.BlockSpec((None, None, tm, c), lambda bi, i: (2, bi, i, 0)),
                  pl.BlockSpec((None, None, halo, c), lambda bi, i: (2, bi, jnp.maximum(i * per - 1, 0), 0)),
                  pl.BlockSpec((halo, c), const),
                  pl.BlockSpec((w, SUBLANES, c), lambda bi, i: (0, 0, 0)),
                  pl.BlockSpec((1, c), const), pl.BlockSpec((1, c), const), pl.BlockSpec((1, c), const),
                  blk(c), blk(d),
                  pl.BlockSpec((d, d), const, pipeline_mode=pl.Buffered(1))],
        out_specs=blk(d),
        out_shape=jax.ShapeDtypeStruct((b, s, d), F32),
        scratch_shapes=[pltpu.VMEM((tm + halo, c), F32), pltpu.VMEM((tm, c), F32), pltpu.VMEM((tm, c), BF16)],
        compiler_params=_cparams("arbitrary", "arbitrary"),
        name="conv_merge",
    )(kvu, kvu, head, cwb, vec(conv_b), vec(ln_g), vec(ln_b), o_n, x, w_o)


def _small_conv_kernel(za_ref, zg_ref, st_ref, cw_ref, cb_ref, lg_ref, lb_ref, u_ref, c_ref, s_ref,
                       *, conv_w, n_sample):
    n_meta = za_ref.shape[0] - n_sample
    u = za_ref[...] * jax.nn.sigmoid(zg_ref[...])
    u_ref[...] = u
    y = u[:n_sample] * cw_ref[conv_w - 1:conv_w, :]
    for w in range(conv_w - 1):
        y = y + st_ref[w] * cw_ref[w:w + 1, :]
    y_s = y + cb_ref[...]
    hist = s_ref.shape[0] - n_meta
    s_ref[0:hist, :] = jnp.zeros((hist, s_ref.shape[1]), F32)
    s_ref[hist:, :] = u[n_sample:]
    y_m = _conv_rows(s_ref, hist - (conv_w - 1), n_meta, cw_ref, conv_w) + cb_ref[...]
    yy = jnp.concatenate([y_s, y_m], axis=0)
    c_ref[...] = _silu(_layernorm(yy, lg_ref[...], lb_ref[...])).astype(c_ref.dtype)


def small_conv_module(za, zg, state_t, conv_w, conv_b, ln_g, ln_b, n_sample):
    m, c = za.shape
    w = conv_w.shape[0]
    n_meta = m - n_sample
    hist = SUBLANES * ((w - 1 + SUBLANES - 1) // SUBLANES)
    vec = lambda a: a.reshape(1, -1)
    return pl.pallas_call(
        functools.partial(_small_conv_kernel, conv_w=w, n_sample=n_sample),
        out_shape=[jax.ShapeDtypeStruct((m, c), F32), jax.ShapeDtypeStruct((m, c), BF16)],
        scratch_shapes=[pltpu.VMEM((hist + n_meta, c), F32)],
        name="small_conv_module",
    )(za, zg, state_t, conv_w, vec(conv_b), vec(ln_g), vec(ln_b))


def _gelu(x):
    return 0.5 * x * (1.0 + lax.erf(x * math.sqrt(0.5)))


def _ffn_kernel(pt_ref, x_ref, g2_ref, gf_ref, wa_ref, wg_ref, wd_ref, cwa_ref, cwg_ref, cba_ref, cbg_ref,
                ha_ref, hg_ref,
                q_ref, kn_ref, vn_ref, mb_ref, mbl_ref, b0_ref, lamv_ref, sg_ref, *rest,
                blocks_per_seq, conv_w, n_sub, dec_pages, dec_groups, n_host, n_heads, dk):
    k_refs = rest[:dec_pages]
    v_refs = rest[dec_pages:2 * dec_pages]
    y_ref, ta_ref, tg_ref, od_ref = rest[2 * dec_pages:2 * dec_pages + 4]
    xn_ref, ea_ref, eg_ref, ca_ref, cg_ref = rest[2 * dec_pages + 4:2 * dec_pages + 9]
    dec_state = rest[2 * dec_pages + 9:]
    del pt_ref
    i = pl.program_id(0)
    f = pl.program_id(1)
    n_f = pl.num_programs(1)
    tm = x_ref.shape[0]
    tf = wa_ref.shape[1]
    ts = tf // n_sub
    pad = SUBLANES
    step = i * n_f + f
    grp = step % dec_groups
    hosting = step < n_host * dec_groups

    @pl.when(f == 0)
    def _():
        x = x_ref[...]
        xn_ref[...] = _rms(x, g2_ref[...]).astype(BF16)
        y_ref[...] = x

    @pl.when((grp == 0) & hosting)
    def _():
        _decode_init(q_ref, kn_ref, vn_ref, b0_ref, *dec_state, dk)

    hosted = _decode_pages_phases(grp == dec_groups - 1, mb_ref, mbl_ref, k_refs, v_refs, *dec_state, n_heads)
    xn = xn_ref[...]
    first = (i % blocks_per_seq) == 0
    base = pad - (conv_w - 1)
    down = None
    for sc in range(n_sub):
        next(hosted, None)
        cols = slice(sc * ts, (sc + 1) * ts)
        up_a = jnp.dot(xn, wa_ref[:, cols], preferred_element_type=F32)
        up_g = jnp.dot(xn, wg_ref[:, cols], preferred_element_type=F32)
        ta_ref[:, cols] = up_a[tm - pad:]
        tg_ref[:, cols] = up_g[tm - pad:]
        ea_ref[sc, 0:pad, :] = jnp.where(first, ha_ref[:, cols], ca_ref[f, :, cols])
        eg_ref[sc, 0:pad, :] = jnp.where(first, hg_ref[:, cols], cg_ref[f, :, cols])
        ea_ref[sc, pad:, :] = up_a
        eg_ref[sc, pad:, :] = up_g
        ca_ref[f, :, cols] = up_a[tm - pad:]
        cg_ref[f, :, cols] = up_g[tm - pad:]
        conv_a = _conv_rows(ea_ref.at[sc], base, tm, cwa_ref.at[:, cols], conv_w) + cba_ref[:, cols]
        conv_g = _conv_rows(eg_ref.at[sc], base, tm, cwg_ref.at[:, cols], conv_w) + cbg_ref[:, cols]
        hidden = (_gelu(conv_g) * conv_a).astype(BF16)
        d = jnp.dot(hidden, wd_ref[cols, :], preferred_element_type=F32)
        down = d if down is None else down + d
    y_ref[...] += down
    for _ in hosted:
        pass

    @pl.when((grp == dec_groups - 1) & hosting)
    def _():
        _decode_final(lamv_ref, sg_ref, od_ref, dec_state[2], dec_state[3], n_heads)

    @pl.when(f == n_f - 1)
    def _():
        y_ref[...] = _rms(y_ref[...], gf_ref[...])


def prompt_ffn(x, norm2_g, final_g, wab, wd, cwa, cwg, cba, cbg, head_a, head_g, rows_per_seq,
               tm, tf, dec, dec_pages):
    m, d = x.shape
    ffp = wab.shape[-1]
    n_f = ffp // tf
    bps = rows_per_seq // tm
    w = cwa.shape[0]
    ts = tf // FF_SUB
    page_table, q, k_new, v_new, cache_k, cache_v, mb, mbl, b0, lamv, subln_g = dec
    _, page, n_heads, e = cache_k.shape
    dv = cache_v.shape[-1]
    groups = page_table.shape[1] // dec_pages
    n_host = min(page_table.shape[0], (m // tm) * n_f // groups)
    assert page_table.shape[1] % dec_pages == 0 and n_host >= 1
    r = 2 * n_heads
    row = lambda i, f, pt: (i, 0)
    const = lambda i, f, pt: (0, 0)
    col = lambda i, f, pt: (0, f)
    tail = pl.BlockSpec((None, SUBLANES, tf), lambda i, f, pt: (i, 0, f))
    seq = lambda i, f: jnp.minimum((i * n_f + f) // groups, n_host - 1)
    grp = lambda i, f: (i * n_f + f) % groups
    seq_blk = lambda wd_: pl.BlockSpec((None, n_heads, wd_), lambda i, f, pt: (seq(i, f), 0, 0))

    def page_spec(p, wd_):
        return pl.BlockSpec((None, page, n_heads, wd_),
                            lambda i, f, pt: (pt[seq(i, f), grp(i, f) * dec_pages + p], 0, 0, 0))

    grid_spec = pltpu.PrefetchScalarGridSpec(
        num_scalar_prefetch=1,
        grid=(m // tm, n_f),
        in_specs=[pl.BlockSpec((tm, d), row),
                  pl.BlockSpec((1, d), const), pl.BlockSpec((1, d), const),
                  pl.BlockSpec((None, d, tf), lambda i, f, pt: (0, 0, f)),
                  pl.BlockSpec((None, d, tf), lambda i, f, pt: (1, 0, f)),
                  pl.BlockSpec((tf, d), lambda i, f, pt: (f, 0)),
                  pl.BlockSpec((w, tf), col), pl.BlockSpec((w, tf), col),
                  pl.BlockSpec((1, tf), col), pl.BlockSpec((1, tf), col),
                  pl.BlockSpec((SUBLANES, tf), col), pl.BlockSpec((SUBLANES, tf), col),
                  seq_blk(e), seq_blk(e), seq_blk(dv),
                  pl.BlockSpec((r, page * n_heads), const), pl.BlockSpec((r, page * n_heads), const),
                  pl.BlockSpec((r, 1), const), pl.BlockSpec((4, e // 2), const), pl.BlockSpec((1, dv), const)]
                 + [page_spec(p, e) for p in range(dec_pages)]
                 + [page_spec(p, dv) for p in range(dec_pages)],
        out_specs=[pl.BlockSpec((tm, d), row), tail, tail, seq_blk(dv)],
        scratch_shapes=[pltpu.VMEM((tm, d), BF16),
                        pltpu.VMEM((FF_SUB, tm + SUBLANES, ts), F32),
                        pltpu.VMEM((FF_SUB, tm + SUBLANES, ts), F32),
                        pltpu.VMEM((n_f, SUBLANES, tf), F32), pltpu.VMEM((n_f, SUBLANES, tf), F32),
                        pltpu.VMEM((r, e), BF16), pltpu.VMEM((r, 1), F32),
                        pltpu.VMEM((r, 1), F32), pltpu.VMEM((r, dv), F32)],
    )
    return pl.pallas_call(
        functools.partial(_ffn_kernel, blocks_per_seq=bps, conv_w=w, n_sub=FF_SUB, dec_pages=dec_pages,
                          dec_groups=groups, n_host=n_host, n_heads=n_heads, dk=e // 2),
        grid_spec=grid_spec,
        out_shape=[jax.ShapeDtypeStruct((m, d), F32),
                   jax.ShapeDtypeStruct((m // tm, SUBLANES, ffp), F32),
                   jax.ShapeDtypeStruct((m // tm, SUBLANES, ffp), F32),
                   jax.ShapeDtypeStruct((n_host, n_heads, dv), BF16)],
        compiler_params=_cparams("arbitrary", "arbitrary"),
        name="prompt_ffn",
    )(page_table[:n_host], x, norm2_g.reshape(1, -1), final_g.reshape(1, -1), wab, wab, wd, cwa, cwg,
      cba.reshape(1, -1), cbg.reshape(1, -1), head_a, head_g,
      q[:n_host], k_new[:n_host], v_new[:n_host], mb, mbl, b0, lamv, subln_g.reshape(1, -1),
      *([cache_k] * dec_pages), *([cache_v] * dec_pages))


def _small_gate_kernel(ua_ref, ug_ref, sa_ref, sg_ref, cwa_ref, cwg_ref, cba_ref, cbg_ref, h_ref,
                       *, conv_w):
    def conv(u_ref, s_ref, cw_ref, cb_ref):
        y = u_ref[...] * cw_ref[conv_w - 1:conv_w, :] + cb_ref[...]
        for w in range(conv_w - 1):
            y = y + s_ref[w] * cw_ref[w:w + 1, :]
        return y

    a = conv(ua_ref, sa_ref, cwa_ref, cba_ref)
    g = conv(ug_ref, sg_ref, cwg_ref, cbg_ref)
    h_ref[...] = (_gelu(g) * a).astype(h_ref.dtype)


def small_gate(up_a, up_g, st_a, st_g, cwa, cwg, cba, cbg):
    return pl.pallas_call(
        functools.partial(_small_gate_kernel, conv_w=cwa.shape[0]),
        out_shape=jax.ShapeDtypeStruct(up_a.shape, BF16),
        name="small_gate",
    )(up_a, up_g, st_a, st_g, cwa, cwg, cba.reshape(1, -1), cbg.reshape(1, -1))


def _residual_norm_kernel(x_ref, d_ref, g_ref, y_ref):
    y_ref[...] = _rms(x_ref[...] + d_ref[...], g_ref[...])


def residual_norm(x, delta, g):
    return pl.pallas_call(
        _residual_norm_kernel,
        out_shape=jax.ShapeDtypeStruct(x.shape, F32),
        name="residual_norm",
    )(x, delta, g.reshape(1, -1))


def kernel(x_prompt, x_sample, cache_k, cache_v, page_table, state_conv, state_ffn_conv, meta_tokens,
           rel_bias, norm1_g, w_in, lambda_q1, lambda_k1, lambda_q2, lambda_k2, subln_g, conv_w, conv_b,
           conv_ln_g, conv_ln_b, w_o, norm2_g, ffn_w_up, ffn_conv_w, ffn_conv_b, ffn_w_down,
           final_norm_g):
    depth = w_in.shape[0]
    assert depth == 1, "single-layer step only"
    b, s, d = x_prompt.shape
    bd, t_dec = x_sample.shape[:2]
    assert t_dec == 1
    n_meta = meta_tokens.shape[0]
    n_pool, page, n_heads, e = cache_k.shape[1:]
    dv = cache_v.shape[-1]
    dk = e // 2
    d_attn = n_heads * dv
    d_conv = conv_w.shape[-1]
    cw_len = conv_w.shape[1]
    fw_len = ffn_conv_w.shape[1]
    d_ff = ffn_w_down.shape[1]
    n_buckets = rel_bias.shape[0]
    t = ATTN_T
    tm = ROW_TILE
    tf = FF_TILE
    assert e == dv == LANES and d_conv == d_attn and w_in.shape[-1] == 5 * d_attn
    assert s % tm == 0 and s % t == 0 and n_meta <= t and page_table.shape[1] % DEC_PAGES == 0
    assert s % FF_ROWS == 0 and (b * s) % IN_ROWS == 0
    me = n_buckets // 2
    sat = math.ceil(me * (MAX_DISTANCE / me) ** ((n_buckets - 1 - me) / (n_buckets - me))) + 1
    assert t + 1 >= sat and page + 1 >= sat
    halo = SUBLANES * ((cw_len - 1 + SUBLANES - 1) // SUBLANES)
    assert n_meta <= halo and tm % halo == 0 and tm % CONV_ROWS == 0 and fw_len - 1 <= SUBLANES

    ffp = tf * ((d_ff + tf - 1) // tf)
    pad_c = lambda a: jnp.pad(a, ((0, 0), (0, ffp - d_ff)))
    cwa, cwg = pad_c(ffn_conv_w[0, :, :d_ff]), pad_c(ffn_conv_w[0, :, d_ff:])
    cba, cbg = pad_c(ffn_conv_b[:, :d_ff])[0], pad_c(ffn_conv_b[:, d_ff:])[0]
    lamv = jnp.stack([lambda_q1[0], lambda_k1[0], lambda_q2[0], lambda_k2[0]]).astype(F32)
    q_scale = dk ** -0.5 * LOG2E

    ar = lambda n: jnp.arange(n, dtype=jnp.int32)
    by_rel = _bias_tile(rel_bias, ar(2 * t)[None, :])[:, 0, :]
    d0, d1 = toeplitz_tiles(jnp.concatenate([jnp.full((n_heads, t - 1), NEG, F32), by_rel[:, :t]], axis=1),
                            by_rel[:, 1:], t)
    dm = _bias_tile(rel_bias, n_meta + ar(t)[:, None] - ar(n_meta)[None, :])
    dmeta = _bias_tile(rel_bias, ar(n_meta)[:, None] - ar(n_meta)[None, :])
    stack2 = lambda a: jnp.concatenate([a, a], axis=0)
    same_head = (ar(n_heads)[:, None, None] == ar(n_heads)[None, None, :])
    blast = _bias_tile(rel_bias, (page - ar(page))[None, :])[:, 0, :]
    mb = stack2(jnp.broadcast_to(jnp.where(same_head, 0.0, NEG), (n_heads, page, n_heads))
                ).reshape(2 * n_heads, page * n_heads)
    mbl = stack2(jnp.where(same_head, blast[:, :, None], NEG)).reshape(2 * n_heads, page * n_heads)
    b0 = stack2(_bias_tile(rel_bias, jnp.zeros((1, 1), jnp.int32))[:, 0, :])

    x_small = jnp.concatenate([x_sample[:, 0, :], meta_tokens.astype(F32)], axis=0)
    xn_small = rmsnorm_cast(x_small, norm1_g[0], BF16)
    z, w_in_b = mm_cast(xn_small, w_in[0], 512)
    q_small = (z[:, :d_attn] * q_scale).astype(BF16)
    k_small = z[:, d_attn:2 * d_attn]
    v_small = z[:, 2 * d_attn:3 * d_attn]
    state_t = jnp.transpose(state_conv[0], (1, 0, 2))
    u_small, c_small = small_conv_module(z[:, 3 * d_attn:4 * d_attn], z[:, 4 * d_attn:], state_t,
                                         conv_w[0], conv_b[0], conv_ln_g[0], conv_ln_b[0], bd)
    o_meta = meta_attention(q_small[bd:], k_small[bd:], v_small[bd:], dmeta, lamv, subln_g[0], n_heads)
    hm_delta, w_o_b = mm_cast(jnp.concatenate([o_meta, c_small[bd:]], axis=1), w_o[0], 512)
    _, xn2_meta = add_rmsnorm(x_small[bd:], hm_delta, norm2_g[0])
    up_meta, wab_b = up_cast(xn2_meta, ffn_w_up[0], d_ff, ffp, tf)
    wd_b = cast_pad(ffn_w_down[0], ffp, 256)

    xp = x_prompt.reshape(b * s, d)
    qkv_b, kvu = prompt_inproj(xp, norm1_g[0], w_in_b, d_attn, q_scale, IN_ROWS)
    qkv_b = qkv_b.reshape(3, b, s, d_attn)
    kvu = kvu.reshape(3, b, s, d_attn)
    o_p = prompt_attention(qkv_b, k_small[bd:].astype(BF16), v_small[bd:].astype(BF16),
                           d0, d1, dm, lamv, subln_g[0], n_heads, t, ATTN_HEADS)
    head_u = jnp.concatenate([jnp.zeros((halo - n_meta, d_conv), F32), u_small[bd:]], axis=0)
    h1_p = conv_merge(kvu, head_u, conv_w[0], conv_b[0], conv_ln_g[0], conv_ln_b[0], o_p, x_prompt, w_o_b,
                      tm).reshape(b * s, d)
    n_t = fw_len - 1
    head_rows = lambda a: jnp.concatenate([jnp.zeros((SUBLANES - n_t, ffp), F32), a[-n_t:]], axis=0)
    hd = lambda a: a.reshape(bd, n_heads, -1)
    dec = (page_table, hd(q_small[:bd]), hd(k_small[:bd]), hd(v_small[:bd]), cache_k[0], cache_v[0],
           mb, mbl, b0, lamv, subln_g[0])
    y_p, tail_a, tail_g, o_host = prompt_ffn(h1_p, norm2_g[0], final_norm_g, wab_b, wd_b, cwa, cwg, cba, cbg,
                                             head_rows(up_meta[0]), head_rows(up_meta[1]), s, FF_ROWS, tf,
                                             dec, FF_DEC_PAGES)

    n_host = o_host.shape[0]
    o_dec = o_host
    if n_host < bd:
        o_rest = decode_attention(*(a[n_host:] for a in dec[:4]), *dec[4:], DEC_PAGES)
        o_dec = jnp.concatenate([o_host, o_rest], axis=0)
    oc_s = jnp.concatenate([o_dec.reshape(bd, d_attn), c_small[:bd]], axis=1)
    hs_delta = mm_bf16(oc_s, w_o_b[None], 0, 512)
    h1_s, xn2_s = add_rmsnorm(x_small[:bd], hs_delta, norm2_g[0])
    up_a_s = mm_bf16(xn2_s, wab_b, 0, 512)
    up_g_s = mm_bf16(xn2_s, wab_b, 1, 512)
    st = jnp.transpose(state_ffn_conv[0], (1, 0, 2))
    st_pad = lambda a: pad_c(a.reshape(-1, d_ff)).reshape(fw_len - 1, bd, ffp)
    hid = small_gate(up_a_s, up_g_s, st_pad(st[:, :, :d_ff]), st_pad(st[:, :, d_ff:]), cwa, cwg, cba, cbg)
    down = mm_bf16(hid, wd_b[None], 0, 256)
    y_sample = residual_norm(h1_s, down, final_norm_g)

    k_prompt = kv_rows_with_meta(kvu, 0, k_small[bd:], n_heads, tm)[None]
    v_prompt = kv_rows_with_meta(kvu, 1, v_small[bd:], n_heads, tm)[None]
    conv_prompt = kvu[2][None, :, s - (cw_len - 1):]
    bps = s // FF_ROWS
    last_blk = lambda a: a[bps - 1::bps, SUBLANES - n_t:, :d_ff]
    ffn_prompt = jnp.concatenate([last_blk(tail_a), last_blk(tail_g)], axis=-1)[None]
    k_sample = k_small[:bd].reshape(1, bd, 1, n_heads, e)
    v_sample = v_small[:bd].reshape(1, bd, 1, n_heads, dv)
    conv_sample = jnp.concatenate([state_conv[0][:, 1:], u_small[:bd, None, :]], axis=1)[None]
    up_s = jnp.concatenate([up_a_s[:, :d_ff], up_g_s[:, :d_ff]], axis=-1)
    ffn_sample = jnp.concatenate([state_ffn_conv[0][:, 1:], up_s[:, None, :]], axis=1)[None]
    return (y_p.reshape(b, s, d), y_sample.reshape(bd, 1, d), k_prompt, v_prompt, conv_prompt, ffn_prompt,
            k_sample, v_sample, conv_sample, ffn_sample)
```

```python
import functools
import math

import jax
import jax.numpy as jnp
from jax import lax
from jax.experimental import pallas as pl
from jax.experimental.pallas import tpu as pltpu

F32 = jnp.float32
BF16 = jnp.bfloat16

EPS = 1e-6
MAX_DISTANCE = 128
LAMBDA_INIT_0 = 0.8 - 0.6 * math.exp(-0.3 * 0)
NEG = -1e30
LOG2E = math.log2(math.e)

LANES = 128
SUBLANES = 8
VMEM_LIMIT = 56 * 1024 * 1024

ATTN_T = 512
ATTN_HEADS = 2
ROW_TILE = 512
IN_ROWS = 1024
FF_ROWS = 512
FF_DEC_PAGES = 8
FF_TILE = 512
FF_SUB = 2
CONV_ROWS = 128
DEC_PAGES = 16


def _cparams(*sem):
    return pltpu.CompilerParams(dimension_semantics=sem, vmem_limit_bytes=VMEM_LIMIT)


def _rms(x, g):
    return x * lax.rsqrt(jnp.mean(x * x, axis=-1, keepdims=True) + EPS) * g


def _layernorm(y, g, b):
    mu = jnp.mean(y, axis=-1, keepdims=True)
    d = y - mu
    var = jnp.mean(d * d, axis=-1, keepdims=True)
    return d * lax.rsqrt(var + EPS) * g + b


def _silu(x):
    return x * jax.nn.sigmoid(x)


def _lambda(lamv_ref):
    a = jnp.sum(lamv_ref[0:1, :] * lamv_ref[1:2, :], axis=-1, keepdims=True)
    b = jnp.sum(lamv_ref[2:3, :] * lamv_ref[3:4, :], axis=-1, keepdims=True)
    return jnp.exp(a) - jnp.exp(b) + LAMBDA_INIT_0


def _t5_bucket(rel, n_buckets):
    n = jnp.maximum(rel, 0)
    max_exact = n_buckets // 2
    nf = jnp.maximum(n, 1).astype(F32)
    large = max_exact + (jnp.log(nf / max_exact) / math.log(MAX_DISTANCE / max_exact)
                         * (n_buckets - max_exact)).astype(jnp.int32)
    large = jnp.minimum(large, n_buckets - 1)
    return jnp.where(n < max_exact, n, large)


def _bias_tile(rel_bias, rel):
    nb = rel_bias.shape[0]
    onehot = (_t5_bucket(rel, nb)[..., None] == jnp.arange(nb, dtype=jnp.int32)).astype(F32)
    table = (rel_bias - rel_bias[nb - 1]).astype(F32) * LOG2E
    b = jnp.einsum("rcn,nh->hrc", onehot, table, precision=lax.Precision.HIGHEST)
    return jnp.where((rel >= 0)[None], b, NEG)


def _toeplitz_kernel(u0_ref, u1_ref, d0_ref, d1_ref):
    t = d0_ref.shape[0]
    for u_ref, d_ref in ((u0_ref, d0_ref), (u1_ref, d1_ref)):
        x = jnp.broadcast_to(u_ref[...], (t, 2 * t))
        d_ref[...] = pltpu.roll(x, 0, 1, stride=1, stride_axis=0)[:, :t]


def toeplitz_tiles(w0, w1, t):
    h = w0.shape[0]

    def wrap(w):
        return jnp.concatenate([w[:, :t][:, ::-1], jnp.zeros((h, 1), F32), w[:, t:][:, ::-1]],
                               axis=1).reshape(h, 1, 2 * t)

    vec = pl.BlockSpec((None, 1, 2 * t), lambda hh: (hh, 0, 0))
    tile = pl.BlockSpec((None, t, t), lambda hh: (hh, 0, 0))
    shape = jax.ShapeDtypeStruct((h, t, t), F32)
    return pl.pallas_call(
        _toeplitz_kernel, grid=(h,), in_specs=[vec, vec], out_specs=[tile, tile],
        out_shape=[shape, shape], name="toeplitz_tiles",
    )(wrap(w0), wrap(w1))


def _rmsnorm_cast_kernel(x_ref, g_ref, o_ref):
    o_ref[...] = _rms(x_ref[...], g_ref[...]).astype(o_ref.dtype)


def rmsnorm_cast(x, g, dtype):
    return pl.pallas_call(
        _rmsnorm_cast_kernel,
        out_shape=jax.ShapeDtypeStruct(x.shape, dtype),
        name="rmsnorm_cast",
    )(x, g.reshape(1, -1))


def _mm_cast_kernel(x_ref, w_ref, o_ref, wb_ref):
    k = w_ref.shape[0]
    wb = w_ref[...].astype(BF16)
    wb_ref[0:k, :] = wb
    if wb_ref.shape[0] > k:
        wb_ref[k:, :] = jnp.zeros((wb_ref.shape[0] - k, wb_ref.shape[1]), BF16)
    o_ref[...] = jnp.dot(x_ref[:, 0:k], wb, preferred_element_type=F32)


def mm_cast(x, w, tn, k_pad=None):
    m, kx = x.shape
    k, n = w.shape
    kp = k if k_pad is None else k_pad
    return pl.pallas_call(
        _mm_cast_kernel,
        grid=(n // tn,),
        in_specs=[pl.BlockSpec((m, kx), lambda j: (0, 0)),
                  pl.BlockSpec((k, tn), lambda j: (0, j))],
        out_specs=[pl.BlockSpec((m, tn), lambda j: (0, j)), pl.BlockSpec((kp, tn), lambda j: (0, j))],
        out_shape=[jax.ShapeDtypeStruct((m, n), F32), jax.ShapeDtypeStruct((kp, n), BF16)],
        compiler_params=_cparams("arbitrary"),
        name="mm_cast",
    )(x, w)


def _up_cast_kernel(x_ref, w_ref, o_ref, wb_ref, *, rem):
    hf = pl.program_id(0)
    j = pl.program_id(1)
    last = pl.num_programs(1) - 1
    d, tn = w_ref.shape

    def emit(w):
        wb = w.astype(BF16)
        wb_ref[...] = wb
        o_ref[...] = jnp.dot(x_ref[...], wb, preferred_element_type=F32)

    @pl.when(j < last)
    def _():
        emit(w_ref[...])

    @pl.when((j == last) & (hf == 0))
    def _():
        lane = lax.broadcasted_iota(jnp.int32, (d, tn), 1)
        emit(jnp.where(lane < rem, w_ref[...], 0.0))

    @pl.when((j == last) & (hf == 1))
    def _():
        emit(jnp.concatenate([w_ref[:, tn - rem:], jnp.zeros((d, tn - rem), F32)], axis=1))


def up_cast(x, w_up, d_ff, ffp, tn):
    m, d = x.shape
    n_blk = ffp // tn
    rem = d_ff - (n_blk - 1) * tn
    assert 0 < rem <= tn and rem % LANES == 0 and d_ff >= tn
    el = pl.Element
    return pl.pallas_call(
        functools.partial(_up_cast_kernel, rem=rem),
        grid=(2, n_blk),
        in_specs=[pl.BlockSpec((m, d), lambda hf, j: (0, 0)),
                  pl.BlockSpec((el(d), el(tn)),
                               lambda hf, j: (0, pl.multiple_of(
                                   jnp.minimum(hf * d_ff + j * tn, 2 * d_ff - tn), LANES)))],
        out_specs=[pl.BlockSpec((None, m, tn), lambda hf, j: (hf, 0, j)),
                   pl.BlockSpec((None, d, tn), lambda hf, j: (hf, 0, j))],
        out_shape=[jax.ShapeDtypeStruct((2, m, ffp), F32), jax.ShapeDtypeStruct((2, d, ffp), BF16)],
        compiler_params=_cparams("arbitrary", "arbitrary"),
        name="up_cast",
    )(x, w_up)


def _cast_pad_kernel(w_ref, wb_ref):
    k = w_ref.shape[0]
    wb_ref[0:k, :] = w_ref[...].astype(BF16)
    if wb_ref.shape[0] > k:
        wb_ref[k:, :] = jnp.zeros((wb_ref.shape[0] - k, wb_ref.shape[1]), BF16)


def cast_pad(w, k_pad, tn):
    k, n = w.shape
    return pl.pallas_call(
        _cast_pad_kernel,
        grid=(n // tn,),
        in_specs=[pl.BlockSpec((k, tn), lambda j: (0, j))],
        out_specs=pl.BlockSpec((k_pad, tn), lambda j: (0, j)),
        out_shape=jax.ShapeDtypeStruct((k_pad, n), BF16),
        compiler_params=_cparams("arbitrary"),
        name="cast_pad",
    )(w)


def _mm_bf16_kernel(x_ref, w_ref, o_ref):
    o_ref[...] = jnp.dot(x_ref[...], w_ref[...], preferred_element_type=F32)


def mm_bf16(x, w3, slab, tn):
    m, k = x.shape
    n = w3.shape[-1]
    return pl.pallas_call(
        _mm_bf16_kernel,
        grid=(n // tn,),
        in_specs=[pl.BlockSpec((m, k), lambda j: (0, 0)),
                  pl.BlockSpec((None, k, tn), lambda j: (slab, 0, j))],
        out_specs=pl.BlockSpec((m, tn), lambda j: (0, j)),
        out_shape=jax.ShapeDtypeStruct((m, n), F32),
        compiler_params=_cparams("arbitrary"),
        name="mm_bf16",
    )(x, w3)


def _add_rmsnorm_kernel(x_ref, d_ref, g_ref, h_ref, xn_ref):
    h = x_ref[...] + d_ref[...]
    h_ref[...] = h
    xn_ref[...] = _rms(h, g_ref[...]).astype(xn_ref.dtype)


def add_rmsnorm(x, delta, g):
    return pl.pallas_call(
        _add_rmsnorm_kernel,
        out_shape=[jax.ShapeDtypeStruct(x.shape, F32), jax.ShapeDtypeStruct(x.shape, BF16)],
        name="add_rmsnorm",
    )(x, delta, g.reshape(1, -1))


def _inproj_kernel(x_ref, g_ref, w_ref, wg_ref, ob_ref, of_ref, xn_ref, *, q_scale):
    j = pl.program_id(1)

    @pl.when(j == 0)
    def _():
        xn_ref[...] = _rms(x_ref[...], g_ref[...]).astype(BF16)

    z = jnp.dot(xn_ref[...], w_ref[...], preferred_element_type=F32)

    @pl.when(j == 0)
    def _():
        ob_ref[...] = (z * q_scale).astype(BF16)

    @pl.when((j == 1) | (j == 2))
    def _():
        of_ref[...] = z
        ob_ref[...] = z.astype(BF16)

    @pl.when(j == 3)
    def _():
        gate = jnp.dot(xn_ref[...], wg_ref[...], preferred_element_type=F32)
        of_ref[...] = z * jax.nn.sigmoid(gate)


def prompt_inproj(x, g, w_in, d_attn, q_scale, tm):
    m, d = x.shape
    c = d_attn
    return pl.pallas_call(
        functools.partial(_inproj_kernel, q_scale=q_scale),
        grid=(m // tm, 4),
        in_specs=[pl.BlockSpec((tm, d), lambda i, j: (i, 0)),
                  pl.BlockSpec((1, d), lambda i, j: (0, 0)),
                  pl.BlockSpec((d, c), lambda i, j: (0, j)),
                  pl.BlockSpec((d, c), lambda i, j: (0, 4), pipeline_mode=pl.Buffered(1))],
        out_specs=[pl.BlockSpec((None, tm, c), lambda i, j: (jnp.minimum(j, 2), i, 0)),
                   pl.BlockSpec((None, tm, c), lambda i, j: (jnp.maximum(j - 1, 0), i, 0))],
        out_shape=[jax.ShapeDtypeStruct((3, m, c), BF16), jax.ShapeDtypeStruct((3, m, c), F32)],
        scratch_shapes=[pltpu.VMEM((tm, d), BF16)],
        compiler_params=_cparams("arbitrary", "arbitrary"),
        name="prompt_inproj",
    )(x, g.reshape(1, -1), w_in, w_in)


def _kv_rows_kernel(x_ref, meta_ref, o_ref, *, n_meta, n_heads):
    i = pl.program_id(1)
    n_blk = pl.num_programs(1)
    tm = o_ref.shape[0]
    heads = lambda a: a.reshape(a.shape[0], n_heads, a.shape[1] // n_heads)

    @pl.when(i == 0)
    def _():
        o_ref[0:n_meta] = heads(meta_ref[...])
        o_ref[n_meta:] = heads(x_ref[0, 0, 0:tm - n_meta, :])

    @pl.when((i > 0) & (i < n_blk - 1))
    def _():
        o_ref[...] = heads(x_ref[0, 0])

    @pl.when(i == n_blk - 1)
    def _():
        o_ref[0:n_meta] = heads(x_ref[0, 0, tm - n_meta:tm, :])
        o_ref[n_meta:] = jnp.zeros((tm - n_meta,) + o_ref.shape[1:], o_ref.dtype)


def kv_rows_with_meta(kvu, slab, meta_rows, n_heads, tm):
    _, b, s, c = kvu.shape
    n_meta = meta_rows.shape[0]
    n_blk = s // tm + 1
    assert n_meta % SUBLANES == 0 and n_meta < tm
    el = pl.Element
    return pl.pallas_call(
        functools.partial(_kv_rows_kernel, n_meta=n_meta, n_heads=n_heads),
        grid=(b, n_blk),
        in_specs=[pl.BlockSpec((el(1), el(1), el(tm), el(c)),
                               lambda bi, i: (slab, bi, pl.multiple_of(
                                   jnp.clip(i * tm - n_meta, 0, s - tm), SUBLANES), 0)),
                  pl.BlockSpec((n_meta, c), lambda bi, i: (0, 0))],
        out_specs=pl.BlockSpec((None, tm, n_heads, c // n_heads), lambda bi, i: (bi, i, 0, 0)),
        out_shape=jax.ShapeDtypeStruct((b, n_meta + s, n_heads, c // n_heads), F32),
        compiler_params=_cparams("arbitrary", "arbitrary"),
        name="kv_rows_with_meta",
    )(kvu, meta_rows)


def _stack_q(q, dk):
    lane = lax.broadcasted_iota(jnp.int32, q.shape, 1)
    zero = jnp.zeros_like(q)
    return jnp.concatenate([jnp.where(lane < dk, q, zero), jnp.where(lane >= dk, q, zero)], axis=0)


def _scores(qq, kblk):
    return lax.dot_general(qq, kblk, (((1,), (1,)), ((), ())), preferred_element_type=F32)


def _add_tile(s, tile):
    t, c = tile.shape
    return (s.reshape(2, t, c) + tile[None]).reshape(2 * t, c)


def _with_ones(v):
    return jnp.concatenate([v, jnp.ones_like(v)], axis=-1)


def _softmax_update(s, vx, carry):
    return _softmax_update_with_max(s, jnp.max(s, axis=-1, keepdims=True), vx, carry)


def _softmax_update_with_max(s, s_max, vx, carry):
    m, acc = carry
    m_new = jnp.maximum(m, s_max)
    alpha = jnp.exp2(m - m_new)
    p = jnp.exp2(s - m_new).astype(BF16)
    return m_new, alpha * acc + jnp.dot(p, vx, preferred_element_type=F32)


def _combine_heads_out(acc, lam, g, t):
    dv = acc.shape[-1] // 2
    o = acc[:t, :dv] / acc[:t, dv:] - lam * (acc[t:, :dv] / acc[t:, dv:])
    return _rms(o, g) * (1.0 - LAMBDA_INIT_0)


def _prompt_attn_kernel(q_ref, k_ref, v_ref, km_ref, vm_ref, d0_ref, d1_ref, dm_ref,
                        lamv_ref, g_ref, o_ref, vx_ref, qa_ref, qb_ref, *, t, dk, n_grp):
    qi = pl.program_id(2)
    r = 2 * t
    e = 2 * dk
    dv = v_ref.shape[-1] // n_grp
    heads = range(n_grp)
    hs = lambda hh, w: slice(hh * w, (hh + 1) * w)

    @pl.when(qi == 0)
    def _():
        for hh in heads:
            vx_ref[hh, :, :dv] = v_ref[:, hs(hh, dv)]
            vx_ref[hh, :, dv:] = jnp.ones((v_ref.shape[0], dv), BF16)

    qq = [_stack_q(q_ref[:, hs(hh, e)], dk) for hh in heads]
    rows = lambda j: pl.ds(pl.multiple_of(j * t, t), t)
    kblk = lambda hh, j: k_ref[rows(j), hs(hh, e)]
    vxblk = lambda hh, j: vx_ref[hh, rows(j), :]

    n_far = jnp.maximum(qi - 1, 0)

    for hh in heads:
        qa_ref[hh] = qq[hh]
        qb_ref[hh] = qq[hh]

    def far_body(j, carry):
        out = []
        for hh in heads:
            m, acc = carry[hh]
            kb = kblk(hh, j)
            m_new = jnp.maximum(m, jnp.max(_scores(qa_ref[hh], kb), axis=-1, keepdims=True))
            alpha = jnp.exp2(m - m_new)
            p = jnp.exp2(_scores(qb_ref[hh], kb) - m_new).astype(BF16)
            out.append((m_new, alpha * acc + jnp.dot(p, vxblk(hh, j), preferred_element_type=F32)))
        return tuple(out)

    init = tuple((jnp.full((r, 1), NEG, F32), jnp.zeros((r, 2 * dv), F32)) for hh in heads)
    carry = lax.fori_loop(0, n_far, far_body, init)

    lam = _lambda(lamv_ref)
    for hh in heads:
        m, acc = carry[hh]
        s_prev = _scores(qq[hh], kblk(hh, n_far))
        s_sub = _add_tile(s_prev, jnp.where(qi >= 1, d1_ref[hh], NEG))
        s_diag = _add_tile(_scores(qq[hh], kblk(hh, qi)), d0_ref[hh])
        s_meta = _add_tile(_scores(qq[hh], km_ref[:, hs(hh, e)]), jnp.where(qi == 0, dm_ref[hh], 0.0))
        c = _softmax_update(s_meta, _with_ones(vm_ref[:, hs(hh, dv)]), (m, acc))
        c = _softmax_update(s_diag, vxblk(hh, qi), c)
        m, acc = _softmax_update(s_sub, vxblk(hh, n_far), c)
        o_ref[:, hs(hh, dv)] = _combine_heads_out(acc, lam, g_ref[...], t).astype(o_ref.dtype)


def prompt_attention(qkv, kmb, vmb, d0, d1, dm, lamv, subln_g, n_heads, t, n_grp):
    _, b, s, width = qkv.shape
    e = dv = width // n_heads
    nm = kmb.shape[0]
    ge, gv = n_grp * e, n_grp * dv
    tile = lambda c: pl.BlockSpec((n_grp, t, c), lambda bi, h, qi: (h, 0, 0))
    return pl.pallas_call(
        functools.partial(_prompt_attn_kernel, t=t, dk=e // 2, n_grp=n_grp),
        grid=(b, n_heads // n_grp, s // t),
        in_specs=[
            pl.BlockSpec((None, None, t, ge), lambda bi, h, qi: (0, bi, qi, h)),
            pl.BlockSpec((None, None, s, ge), lambda bi, h, qi: (1, bi, 0, h)),
            pl.BlockSpec((None, None, s, gv), lambda bi, h, qi: (2, bi, 0, h)),
            pl.BlockSpec((nm, ge), lambda bi, h, qi: (0, h)),
            pl.BlockSpec((nm, gv), lambda bi, h, qi: (0, h)),
            tile(t), tile(t), tile(nm),
            pl.BlockSpec((4, e // 2), lambda bi, h, qi: (0, 0)),
            pl.BlockSpec((1, dv), lambda bi, h, qi: (0, 0)),
        ],
        out_specs=pl.BlockSpec((None, t, gv), lambda bi, h, qi: (bi, qi, h)),
        out_shape=jax.ShapeDtypeStruct((b, s, n_heads * dv), BF16),
        scratch_shapes=[pltpu.VMEM((n_grp, s, 2 * dv), BF16),
                        pltpu.VMEM((n_grp, 2 * t, e), BF16), pltpu.VMEM((n_grp, 2 * t, e), BF16)],
        compiler_params=_cparams("arbitrary", "arbitrary", "arbitrary"),
        name="prompt_attention",
    )(qkv, qkv, qkv, kmb, vmb, d0, d1, dm, lamv, subln_g.reshape(1, -1))


def _meta_attn_kernel(q_ref, k_ref, v_ref, d_ref, lamv_ref, g_ref, o_ref, *, n_heads, dk):
    t = q_ref.shape[0]
    e = 2 * dk
    dv = v_ref.shape[-1] // n_heads
    lam = _lambda(lamv_ref)
    for h in range(n_heads):
        qq = _stack_q(q_ref[:, h * e:(h + 1) * e], dk)
        kblk = k_ref[:, h * e:(h + 1) * e].astype(BF16)
        vx = _with_ones(v_ref[:, h * dv:(h + 1) * dv].astype(BF16))
        carry = (jnp.full((2 * t, 1), NEG, F32), jnp.zeros((2 * t, 2 * dv), F32))
        m, acc = _softmax_update(_add_tile(_scores(qq, kblk), d_ref[h]), vx, carry)
        o_ref[:, h * dv:(h + 1) * dv] = _combine_heads_out(acc, lam, g_ref[...], t).astype(o_ref.dtype)


def meta_attention(q, k, v, dmeta, lamv, subln_g, n_heads):
    t = q.shape[0]
    dv = v.shape[-1] // n_heads
    return pl.pallas_call(
        functools.partial(_meta_attn_kernel, n_heads=n_heads, dk=k.shape[-1] // n_heads // 2),
        out_shape=jax.ShapeDtypeStruct((t, n_heads * dv), BF16),
        name="meta_attention",
    )(q, k, v, dmeta, lamv, subln_g.reshape(1, -1))


def _decode_init(q_ref, kn_ref, vn_ref, b0_ref, qq_ref, m_ref, l_ref, acc_ref, dk):
    qf = q_ref[...].astype(F32)
    lane = lax.broadcasted_iota(jnp.int32, qf.shape, 1)
    qq = jnp.concatenate([jnp.where(lane < dk, qf, 0.0), jnp.where(lane >= dk, qf, 0.0)], axis=0)
    qq_ref[...] = qq.astype(BF16)
    kn = jnp.concatenate([kn_ref[...], kn_ref[...]], axis=0)
    s_new = jnp.sum(qq * kn, axis=-1, keepdims=True) + b0_ref[...]
    m_ref[...] = s_new
    l_ref[...] = jnp.ones_like(s_new)
    acc_ref[...] = jnp.concatenate([vn_ref[...], vn_ref[...]], axis=0)


def _decode_pages(*args):
    for _ in _decode_pages_phases(*args):
        pass


def _decode_pages_phases(is_last_group, mb_ref, mbl_ref, k_refs, v_refs, qq_ref, m_ref, l_ref, acc_ref,
                         n_heads):
    qq = qq_ref[...]
    e = qq.shape[-1]
    n_p = len(k_refs)
    s_pages = []
    for p in range(n_p):
        bias = mb_ref[...]
        if p == n_p - 1:
            bias = jnp.where(is_last_group, mbl_ref[...], bias)
        rows = k_refs[p].shape[0] * n_heads
        s_pages.append(_scores(qq, k_refs[p][...].reshape(rows, e).astype(BF16)) + bias)
    yield
    m_old = m_ref[...]
    m_loc = s_pages[0]
    for s in s_pages[1:]:
        m_loc = jnp.maximum(m_loc, s)
    m_new = jnp.maximum(m_old, jnp.max(m_loc, axis=-1, keepdims=True))
    alpha = jnp.exp2(m_old - m_new)
    weights = [jnp.exp2(s - m_new) for s in s_pages]
    yield
    l_sum = None
    pv = None
    for p in range(n_p):
        w = weights[p]
        rows = v_refs[p].shape[0] * n_heads
        vblk = v_refs[p][...].reshape(rows, v_refs[p].shape[-1]).astype(BF16)
        d = jnp.dot(w.astype(BF16), vblk, preferred_element_type=F32)
        l_sum = w if l_sum is None else l_sum + w
        pv = d if pv is None else pv + d
    l_ref[...] = alpha * l_ref[...] + jnp.sum(l_sum, axis=-1, keepdims=True)
    acc_ref[...] = alpha * acc_ref[...] + pv
    m_ref[...] = m_new


def _decode_final(lamv_ref, g_ref, o_ref, l_ref, acc_ref, n_heads):
    acc, l = acc_ref[...], l_ref[...]
    o = acc[:n_heads] / l[:n_heads] - _lambda(lamv_ref) * (acc[n_heads:] / l[n_heads:])
    o_ref[...] = (_rms(o, g_ref[...]) * (1.0 - LAMBDA_INIT_0)).astype(o_ref.dtype)


def _decode_attn_kernel(pt_ref, q_ref, kn_ref, vn_ref, mb_ref, mbl_ref, b0_ref, lamv_ref, g_ref, *rest,
                        n_pages_step, n_heads, dk):
    k_refs = rest[:n_pages_step]
    v_refs = rest[n_pages_step:2 * n_pages_step]
    o_ref = rest[2 * n_pages_step]
    state = rest[2 * n_pages_step + 1:]
    del pt_ref
    g = pl.program_id(1)
    last = g == pl.num_programs(1) - 1

    @pl.when(g == 0)
    def _():
        _decode_init(q_ref, kn_ref, vn_ref, b0_ref, *state, dk)

    _decode_pages(last, mb_ref, mbl_ref, k_refs, v_refs, *state, n_heads)

    @pl.when(last)
    def _():
        _decode_final(lamv_ref, g_ref, o_ref, state[2], state[3], n_heads)


def decode_attention(page_table, q, k_new, v_new, cache_k, cache_v, mb, mbl, b0, lamv, subln_g,
                     pages_per_step):
    bd, n_pages = page_table.shape
    _, page, n_heads, e = cache_k.shape
    dv = cache_v.shape[-1]
    p_step = pages_per_step
    r = 2 * n_heads
    seq_blk = lambda w: pl.BlockSpec((None, n_heads, w), lambda b, g, pt: (b, 0, 0))
    const2 = lambda b, g, pt: (0, 0)

    def page_spec(p, w):
        return pl.BlockSpec((None, page, n_heads, w), lambda b, g, pt: (pt[b, g * p_step + p], 0, 0, 0))

    grid_spec = pltpu.PrefetchScalarGridSpec(
        num_scalar_prefetch=1,
        grid=(bd, n_pages // p_step),
        in_specs=[seq_blk(e), seq_blk(e), seq_blk(dv),
                  pl.BlockSpec((r, page * n_heads), const2),
                  pl.BlockSpec((r, page * n_heads), const2),
                  pl.BlockSpec((r, 1), const2),
                  pl.BlockSpec((4, e // 2), const2),
                  pl.BlockSpec((1, dv), const2)]
                 + [page_spec(p, e) for p in range(p_step)]
                 + [page_spec(p, dv) for p in range(p_step)],
        out_specs=seq_blk(dv),
        scratch_shapes=[pltpu.VMEM((r, e), BF16), pltpu.VMEM((r, 1), F32),
                        pltpu.VMEM((r, 1), F32), pltpu.VMEM((r, dv), F32)],
    )
    return pl.pallas_call(
        functools.partial(_decode_attn_kernel, n_pages_step=p_step, n_heads=n_heads, dk=e // 2),
        grid_spec=grid_spec,
        out_shape=jax.ShapeDtypeStruct((bd, n_heads, dv), BF16),
        compiler_params=_cparams("arbitrary", "arbitrary"),
        name="decode_attention",
    )(page_table, q, k_new, v_new, mb, mbl, b0, lamv, subln_g.reshape(1, -1),
      *([cache_k] * p_step), *([cache_v] * p_step))


def _conv_rows(src_ref, base, n_rows, cw_ref, conv_w):
    acc = src_ref[pl.ds(base, n_rows), :] * cw_ref[0:1, :]
    for w in range(1, conv_w):
        acc = acc + src_ref[pl.ds(base + w, n_rows), :] * cw_ref[w:w + 1, :]
    return acc


def _conv_tiles(win, cwb_ref, lanes, conv_w, first, n_out):
    y = None
    for k in range(SUBLANES):
        taps = [w for w in range(conv_w) if (first + w) % SUBLANES == k]
        if not taps:
            continue
        n_p = n_out if k == 0 else n_out + 1
        part = None
        for w in taps:
            j = (first + w) // SUBLANES
            term = win[j:j + n_p] * cwb_ref[w, :, lanes][None]
            part = term if part is None else part + term
        part = part.reshape(n_p * SUBLANES, part.shape[-1])
        part = part if k == 0 else part[k:k + n_out * SUBLANES]
        y = part if y is None else y + part
    return y


def _conv_merge_kernel(u_ref, uh_ref, head_ref, cwb_ref, cb_ref, lg_ref, lb_ref, o_ref, x_ref, w_ref,
                       y_ref, s_ref, yc_ref, c_ref, *, conv_w, halo, rows):
    i = pl.program_id(1)
    tm, c = u_ref.shape
    da = o_ref.shape[-1]
    s_ref[0:halo, :] = jnp.where(i == 0, head_ref[...], uh_ref[...])
    s_ref[halo:, :] = u_ref[...]
    first = halo - (conv_w - 1)
    n_win = (rows + halo) // SUBLANES

    y_ref[...] = x_ref[...] + jnp.dot(o_ref[...], w_ref[0:da, :], preferred_element_type=F32)
    for ci in range(tm // rows):
        r0 = ci * rows
        for lb in range(c // LANES):
            lanes = slice(lb * LANES, (lb + 1) * LANES)
            win = s_ref[r0:r0 + rows + halo, lanes].reshape(n_win, SUBLANES, LANES)
            yc_ref[r0:r0 + rows, lanes] = _conv_tiles(win, cwb_ref, lanes, conv_w, first, rows // SUBLANES)
        y = yc_ref[r0:r0 + rows, :] + cb_ref[...]
        c_ref[r0:r0 + rows, :] = _silu(_layernorm(y, lg_ref[...], lb_ref[...])).astype(c_ref.dtype)
    y_ref[...] += jnp.dot(c_ref[...], w_ref[da:, :], preferred_element_type=F32)


def conv_merge(kvu, head, conv_w, conv_b, ln_g, ln_b, o_n, x, w_o, tm):
    _, b, s, c = kvu.shape
    d = x.shape[-1]
    halo = head.shape[0]
    w = conv_w.shape[0]
    per = tm // halo
    vec = lambda a: a.reshape(1, -1)
    const = lambda bi, i: (0, 0)
    blk = lambda width: pl.BlockSpec((None, tm, width), lambda bi, i: (bi, i, 0))
    cwb = jnp.broadcast_to(conv_w[:, None, :], (w, SUBLANES, c))
    return pl.pallas_call(
        functools.partial(_conv_merge_kernel, conv_w=w, halo=halo, rows=CONV_ROWS),
        grid=(b, s // tm),
        in_specs=[pl.BlockSpec((None, None, tm, c), lambda bi, i: (2, bi, i, 0)),
                  pl.BlockSpec((None, None, halo, c), lambda bi, i: (2, bi, jnp.maximum(i * per - 1, 0), 0)),
                  pl.BlockSpec((halo, c), const),
                  pl.BlockSpec((w, SUBLANES, c), lambda bi, i: (0, 0, 0)),
                  pl.BlockSpec((1, c), const), pl.BlockSpec((1, c), const), pl.BlockSpec((1, c), const),
                  blk(c), blk(d),
                  pl.BlockSpec((d, d), const, pipeline_mode=pl.Buffered(1))],
        out_specs=blk(d),
        out_shape=jax.ShapeDtypeStruct((b, s, d), F32),
        scratch_shapes=[pltpu.VMEM((tm + halo, c), F32), pltpu.VMEM((tm, c), F32), pltpu.VMEM((tm, c), BF16)],
        compiler_params=_cparams("arbitrary", "arbitrary"),
        name="conv_merge",
    )(kvu, kvu, head, cwb, vec(conv_b), vec(ln_g), vec(ln_b), o_n, x, w_o)


def _small_conv_kernel(za_ref, zg_ref, st_ref, cw_ref, cb_ref, lg_ref, lb_ref, u_ref, c_ref, s_ref,
                       *, conv_w, n_sample):
    n_meta = za_ref.shape[0] - n_sample
    u = za_ref[...] * jax.nn.sigmoid(zg_ref[...])
    u_ref[...] = u
    y = u[:n_sample] * cw_ref[conv_w - 1:conv_w, :]
    for w in range(conv_w - 1):
        y = y + st_ref[w] * cw_ref[w:w + 1, :]
    y_s = y + cb_ref[...]
    hist = s_ref.shape[0] - n_meta
    s_ref[0:hist, :] = jnp.zeros((hist, s_ref.shape[1]), F32)
    s_ref[hist:, :] = u[n_sample:]
    y_m = _conv_rows(s_ref, hist - (conv_w - 1), n_meta, cw_ref, conv_w) + cb_ref[...]
    yy = jnp.concatenate([y_s, y_m], axis=0)
    c_ref[...] = _silu(_layernorm(yy, lg_ref[...], lb_ref[...])).astype(c_ref.dtype)


def small_conv_module(za, zg, state_t, conv_w, conv_b, ln_g, ln_b, n_sample):
    m, c = za.shape
    w = conv_w.shape[0]
    n_meta = m - n_sample
    hist = SUBLANES * ((w - 1 + SUBLANES - 1) // SUBLANES)
    vec = lambda a: a.reshape(1, -1)
    return pl.pallas_call(
        functools.partial(_small_conv_kernel, conv_w=w, n_sample=n_sample),
        out_shape=[jax.ShapeDtypeStruct((m, c), F32), jax.ShapeDtypeStruct((m, c), BF16)],
        scratch_shapes=[pltpu.VMEM((hist + n_meta, c), F32)],
        name="small_conv_module",
    )(za, zg, state_t, conv_w, vec(conv_b), vec(ln_g), vec(ln_b))


def _gelu(x):
    return 0.5 * x * (1.0 + lax.erf(x * math.sqrt(0.5)))


def _ffn_kernel(pt_ref, x_ref, g2_ref, gf_ref, wa_ref, wg_ref, wd_ref, cwa_ref, cwg_ref, cba_ref, cbg_ref,
                ha_ref, hg_ref,
                q_ref, kn_ref, vn_ref, mb_ref, mbl_ref, b0_ref, lamv_ref, sg_ref, *rest,
                blocks_per_seq, conv_w, n_sub, dec_pages, dec_groups, n_host, n_heads, dk):
    k_refs = rest[:dec_pages]
    v_refs = rest[dec_pages:2 * dec_pages]
    y_ref, ta_ref, tg_ref, od_ref = rest[2 * dec_pages:2 * dec_pages + 4]
    xn_ref, ea_ref, eg_ref, ca_ref, cg_ref = rest[2 * dec_pages + 4:2 * dec_pages + 9]
    dec_state = rest[2 * dec_pages + 9:]
    del pt_ref
    i = pl.program_id(0)
    f = pl.program_id(1)
    n_f = pl.num_programs(1)
    tm = x_ref.shape[0]
    tf = wa_ref.shape[1]
    ts = tf // n_sub
    pad = SUBLANES
    step = i * n_f + f
    grp = step % dec_groups
    hosting = step < n_host * dec_groups

    @pl.when(f == 0)
    def _():
        x = x_ref[...]
        xn_ref[...] = _rms(x, g2_ref[...]).astype(BF16)
        y_ref[...] = x

    @pl.when((grp == 0) & hosting)
    def _():
        _decode_init(q_ref, kn_ref, vn_ref, b0_ref, *dec_state, dk)

    hosted = _decode_pages_phases(grp == dec_groups - 1, mb_ref, mbl_ref, k_refs, v_refs, *dec_state, n_heads)
    xn = xn_ref[...]
    first = (i % blocks_per_seq) == 0
    base = pad - (conv_w - 1)
    down = None
    for sc in range(n_sub):
        next(hosted, None)
        cols = slice(sc * ts, (sc + 1) * ts)
        up_a = jnp.dot(xn, wa_ref[:, cols], preferred_element_type=F32)
        up_g = jnp.dot(xn, wg_ref[:, cols], preferred_element_type=F32)
        ta_ref[:, cols] = up_a[tm - pad:]
        tg_ref[:, cols] = up_g[tm - pad:]
        ea_ref[sc, 0:pad, :] = jnp.where(first, ha_ref[:, cols], ca_ref[f, :, cols])
        eg_ref[sc, 0:pad, :] = jnp.where(first, hg_ref[:, cols], cg_ref[f, :, cols])
        ea_ref[sc, pad:, :] = up_a
        eg_ref[sc, pad:, :] = up_g
        ca_ref[f, :, cols] = up_a[tm - pad:]
        cg_ref[f, :, cols] = up_g[tm - pad:]
        conv_a = _conv_rows(ea_ref.at[sc], base, tm, cwa_ref.at[:, cols], conv_w) + cba_ref[:, cols]
        conv_g = _conv_rows(eg_ref.at[sc], base, tm, cwg_ref.at[:, cols], conv_w) + cbg_ref[:, cols]
        hidden = (_gelu(conv_g) * conv_a).astype(BF16)
        d = jnp.dot(hidden, wd_ref[cols, :], preferred_element_type=F32)
        down = d if down is None else down + d
    y_ref[...] += down
    for _ in hosted:
        pass

    @pl.when((grp == dec_groups - 1) & hosting)
    def _():
        _decode_final(lamv_ref, sg_ref, od_ref, dec_state[2], dec_state[3], n_heads)

    @pl.when(f == n_f - 1)
    def _():
        y_ref[...] = _rms(y_ref[...], gf_ref[...])


def prompt_ffn(x, norm2_g, final_g, wab, wd, cwa, cwg, cba, cbg, head_a, head_g, rows_per_seq,
               tm, tf, dec, dec_pages):
    m, d = x.shape
    ffp = wab.shape[-1]
    n_f = ffp // tf
    bps = rows_per_seq // tm
    w = cwa.shape[0]
    ts = tf // FF_SUB
    page_table, q, k_new, v_new, cache_k, cache_v, mb, mbl, b0, lamv, subln_g = dec
    _, page, n_heads, e = cache_k.shape
    dv = cache_v.shape[-1]
    groups = page_table.shape[1] // dec_pages
    n_host = min(page_table.shape[0], (m // tm) * n_f // groups)
    assert page_table.shape[1] % dec_pages == 0 and n_host >= 1
    r = 2 * n_heads
    row = lambda i, f, pt: (i, 0)
    const = lambda i, f, pt: (0, 0)
    col = lambda i, f, pt: (0, f)
    tail = pl.BlockSpec((None, SUBLANES, tf), lambda i, f, pt: (i, 0, f))
    seq = lambda i, f: jnp.minimum((i * n_f + f) // groups, n_host - 1)
    seq_blk = lambda wd_: pl.BlockSpec((None, n_heads, wd_), lambda i, f, pt: (seq(i, f), 0, 0))

    n_steps = (m // tm) * n_f
    pt_steps = page_table[:n_host].reshape(n_host * groups, dec_pages)
    pt_steps = jnp.concatenate([pt_steps, jnp.broadcast_to(pt_steps[-1:], (n_steps - n_host * groups, dec_pages))],
                               axis=0).reshape(n_steps * dec_pages)

    def page_spec(p, wd_):
        return pl.BlockSpec((None, page, n_heads, wd_),
                            lambda i, f, pt: (pt[(i * n_f + f) * dec_pages + p], 0, 0, 0))

    grid_spec = pltpu.PrefetchScalarGridSpec(
        num_scalar_prefetch=1,
        grid=(m // tm, n_f),
        in_specs=[pl.BlockSpec((tm, d), row),
                  pl.BlockSpec((1, d), const), pl.BlockSpec((1, d), const),
                  pl.BlockSpec((None, d, tf), lambda i, f, pt: (0, 0, f)),
                  pl.BlockSpec((None, d, tf), lambda i, f, pt: (1, 0, f)),
                  pl.BlockSpec((tf, d), lambda i, f, pt: (f, 0)),
                  pl.BlockSpec((w, tf), col), pl.BlockSpec((w, tf), col),
                  pl.BlockSpec((1, tf), col), pl.BlockSpec((1, tf), col),
                  pl.BlockSpec((SUBLANES, tf), col), pl.BlockSpec((SUBLANES, tf), col),
                  seq_blk(e), seq_blk(e), seq_blk(dv),
                  pl.BlockSpec((r, page * n_heads), const), pl.BlockSpec((r, page * n_heads), const),
                  pl.BlockSpec((r, 1), const), pl.BlockSpec((4, e // 2), const), pl.BlockSpec((1, dv), const)]
                 + [page_spec(p, e) for p in range(dec_pages)]
                 + [page_spec(p, dv) for p in range(dec_pages)],
        out_specs=[pl.BlockSpec((tm, d), row), tail, tail, seq_blk(dv)],
        scratch_shapes=[pltpu.VMEM((tm, d), BF16),
                        pltpu.VMEM((FF_SUB, tm + SUBLANES, ts), F32),
                        pltpu.VMEM((FF_SUB, tm + SUBLANES, ts), F32),
                        pltpu.VMEM((n_f, SUBLANES, tf), F32), pltpu.VMEM((n_f, SUBLANES, tf), F32),
                        pltpu.VMEM((r, e), BF16), pltpu.VMEM((r, 1), F32),
                        pltpu.VMEM((r, 1), F32), pltpu.VMEM((r, dv), F32)],
    )
    return pl.pallas_call(
        functools.partial(_ffn_kernel, blocks_per_seq=bps, conv_w=w, n_sub=FF_SUB, dec_pages=dec_pages,
                          dec_groups=groups, n_host=n_host, n_heads=n_heads, dk=e // 2),
        grid_spec=grid_spec,
        out_shape=[jax.ShapeDtypeStruct((m, d), F32),
                   jax.ShapeDtypeStruct((m // tm, SUBLANES, ffp), F32),
                   jax.ShapeDtypeStruct((m // tm, SUBLANES, ffp), F32),
                   jax.ShapeDtypeStruct((n_host, n_heads, dv), BF16)],
        compiler_params=_cparams("arbitrary", "arbitrary"),
        name="prompt_ffn",
    )(pt_steps, x, norm2_g.reshape(1, -1), final_g.reshape(1, -1), wab, wab, wd, cwa, cwg,
      cba.reshape(1, -1), cbg.reshape(1, -1), head_a, head_g,
      q[:n_host], k_new[:n_host], v_new[:n_host], mb, mbl, b0, lamv, subln_g.reshape(1, -1),
      *([cache_k] * dec_pages), *([cache_v] * dec_pages))


def _small_gate_kernel(ua_ref, ug_ref, sa_ref, sg_ref, cwa_ref, cwg_ref, cba_ref, cbg_ref, h_ref,
                       *, conv_w):
    def conv(u_ref, s_ref, cw_ref, cb_ref):
        y = u_ref[...] * cw_ref[conv_w - 1:conv_w, :] + cb_ref[...]
        for w in range(conv_w - 1):
            y = y + s_ref[w] * cw_ref[w:w + 1, :]
        return y

    a = conv(ua_ref, sa_ref, cwa_ref, cba_ref)
    g = conv(ug_ref, sg_ref, cwg_ref, cbg_ref)
    h_ref[...] = (_gelu(g) * a).astype(h_ref.dtype)


def small_gate(up_a, up_g, st_a, st_g, cwa, cwg, cba, cbg):
    return pl.pallas_call(
        functools.partial(_small_gate_kernel, conv_w=cwa.shape[0]),
        out_shape=jax.ShapeDtypeStruct(up_a.shape, BF16),
        name="small_gate",
    )(up_a, up_g, st_a, st_g, cwa, cwg, cba.reshape(1, -1), cbg.reshape(1, -1))


def _residual_norm_kernel(x_ref, d_ref, g_ref, y_ref):
    y_ref[...] = _rms(x_ref[...] + d_ref[...], g_ref[...])


def residual_norm(x, delta, g):
    return pl.pallas_call(
        _residual_norm_kernel,
        out_shape=jax.ShapeDtypeStruct(x.shape, F32),
        name="residual_norm",
    )(x, delta, g.reshape(1, -1))


def kernel(x_prompt, x_sample, cache_k, cache_v, page_table, state_conv, state_ffn_conv, meta_tokens,
           rel_bias, norm1_g, w_in, lambda_q1, lambda_k1, lambda_q2, lambda_k2, subln_g, conv_w, conv_b,
           conv_ln_g, conv_ln_b, w_o, norm2_g, ffn_w_up, ffn_conv_w, ffn_conv_b, ffn_w_down,
           final_norm_g):
    depth = w_in.shape[0]
    assert depth == 1, "single-layer step only"
    b, s, d = x_prompt.shape
    bd, t_dec = x_sample.shape[:2]
    assert t_dec == 1
    n_meta = meta_tokens.shape[0]
    n_pool, page, n_heads, e = cache_k.shape[1:]
    dv = cache_v.shape[-1]
    dk = e // 2
    d_attn = n_heads * dv
    d_conv = conv_w.shape[-1]
    cw_len = conv_w.shape[1]
    fw_len = ffn_conv_w.shape[1]
    d_ff = ffn_w_down.shape[1]
    n_buckets = rel_bias.shape[0]
    t = ATTN_T
    tm = ROW_TILE
    tf = FF_TILE
    assert e == dv == LANES and d_conv == d_attn and w_in.shape[-1] == 5 * d_attn
    assert s % tm == 0 and s % t == 0 and n_meta <= t and page_table.shape[1] % DEC_PAGES == 0
    assert s % FF_ROWS == 0 and (b * s) % IN_ROWS == 0
    me = n_buckets // 2
    sat = math.ceil(me * (MAX_DISTANCE / me) ** ((n_buckets - 1 - me) / (n_buckets - me))) + 1
    assert t + 1 >= sat and page + 1 >= sat
    halo = SUBLANES * ((cw_len - 1 + SUBLANES - 1) // SUBLANES)
    assert n_meta <= halo and tm % halo == 0 and tm % CONV_ROWS == 0 and fw_len - 1 <= SUBLANES

    ffp = tf * ((d_ff + tf - 1) // tf)
    pad_c = lambda a: jnp.pad(a, ((0, 0), (0, ffp - d_ff)))
    cwa, cwg = pad_c(ffn_conv_w[0, :, :d_ff]), pad_c(ffn_conv_w[0, :, d_ff:])
    cba, cbg = pad_c(ffn_conv_b[:, :d_ff])[0], pad_c(ffn_conv_b[:, d_ff:])[0]
    lamv = jnp.stack([lambda_q1[0], lambda_k1[0], lambda_q2[0], lambda_k2[0]]).astype(F32)
    q_scale = dk ** -0.5 * LOG2E

    ar = lambda n: jnp.arange(n, dtype=jnp.int32)
    by_rel = _bias_tile(rel_bias, ar(2 * t)[None, :])[:, 0, :]
    d0, d1 = toeplitz_tiles(jnp.concatenate([jnp.full((n_heads, t - 1), NEG, F32), by_rel[:, :t]], axis=1),
                            by_rel[:, 1:], t)
    dm = _bias_tile(rel_bias, n_meta + ar(t)[:, None] - ar(n_meta)[None, :])
    dmeta = _bias_tile(rel_bias, ar(n_meta)[:, None] - ar(n_meta)[None, :])
    stack2 = lambda a: jnp.concatenate([a, a], axis=0)
    same_head = (ar(n_heads)[:, None, None] == ar(n_heads)[None, None, :])
    blast = _bias_tile(rel_bias, (page - ar(page))[None, :])[:, 0, :]
    mb = stack2(jnp.broadcast_to(jnp.where(same_head, 0.0, NEG), (n_heads, page, n_heads))
                ).reshape(2 * n_heads, page * n_heads)
    mbl = stack2(jnp.where(same_head, blast[:, :, None], NEG)).reshape(2 * n_heads, page * n_heads)
    b0 = stack2(_bias_tile(rel_bias, jnp.zeros((1, 1), jnp.int32))[:, 0, :])

    x_small = jnp.concatenate([x_sample[:, 0, :], meta_tokens.astype(F32)], axis=0)
    xn_small = rmsnorm_cast(x_small, norm1_g[0], BF16)
    z, w_in_b = mm_cast(xn_small, w_in[0], 512)
    q_small = (z[:, :d_attn] * q_scale).astype(BF16)
    k_small = z[:, d_attn:2 * d_attn]
    v_small = z[:, 2 * d_attn:3 * d_attn]
    state_t = jnp.transpose(state_conv[0], (1, 0, 2))
    u_small, c_small = small_conv_module(z[:, 3 * d_attn:4 * d_attn], z[:, 4 * d_attn:], state_t,
                                         conv_w[0], conv_b[0], conv_ln_g[0], conv_ln_b[0], bd)
    o_meta = meta_attention(q_small[bd:], k_small[bd:], v_small[bd:], dmeta, lamv, subln_g[0], n_heads)
    hm_delta, w_o_b = mm_cast(jnp.concatenate([o_meta, c_small[bd:]], axis=1), w_o[0], 512)
    _, xn2_meta = add_rmsnorm(x_small[bd:], hm_delta, norm2_g[0])
    up_meta, wab_b = up_cast(xn2_meta, ffn_w_up[0], d_ff, ffp, tf)
    wd_b = cast_pad(ffn_w_down[0], ffp, 256)

    xp = x_prompt.reshape(b * s, d)
    qkv_b, kvu = prompt_inproj(xp, norm1_g[0], w_in_b, d_attn, q_scale, IN_ROWS)
    qkv_b = qkv_b.reshape(3, b, s, d_attn)
    kvu = kvu.reshape(3, b, s, d_attn)
    o_p = prompt_attention(qkv_b, k_small[bd:].astype(BF16), v_small[bd:].astype(BF16),
                           d0, d1, dm, lamv, subln_g[0], n_heads, t, ATTN_HEADS)
    head_u = jnp.concatenate([jnp.zeros((halo - n_meta, d_conv), F32), u_small[bd:]], axis=0)
    h1_p = conv_merge(kvu, head_u, conv_w[0], conv_b[0], conv_ln_g[0], conv_ln_b[0], o_p, x_prompt, w_o_b,
                      tm).reshape(b * s, d)
    n_t = fw_len - 1
    head_rows = lambda a: jnp.concatenate([jnp.zeros((SUBLANES - n_t, ffp), F32), a[-n_t:]], axis=0)
    hd = lambda a: a.reshape(bd, n_heads, -1)
    dec = (page_table, hd(q_small[:bd]), hd(k_small[:bd]), hd(v_small[:bd]), cache_k[0], cache_v[0],
           mb, mbl, b0, lamv, subln_g[0])
    y_p, tail_a, tail_g, o_host = prompt_ffn(h1_p, norm2_g[0], final_norm_g, wab_b, wd_b, cwa, cwg, cba, cbg,
                                             head_rows(up_meta[0]), head_rows(up_meta[1]), s, FF_ROWS, tf,
                                             dec, FF_DEC_PAGES)

    n_host = o_host.shape[0]
    o_dec = o_host
    if n_host < bd:
        o_rest = decode_attention(*(a[n_host:] for a in dec[:4]), *dec[4:], DEC_PAGES)
        o_dec = jnp.concatenate([o_host, o_rest], axis=0)
    oc_s = jnp.concatenate([o_dec.reshape(bd, d_attn), c_small[:bd]], axis=1)
    hs_delta = mm_bf16(oc_s, w_o_b[None], 0, 512)
    h1_s, xn2_s = add_rmsnorm(x_small[:bd], hs_delta, norm2_g[0])
    up_a_s = mm_bf16(xn2_s, wab_b, 0, 512)
    up_g_s = mm_bf16(xn2_s, wab_b, 1, 512)
    st = jnp.transpose(state_ffn_conv[0], (1, 0, 2))
    st_pad = lambda a: pad_c(a.reshape(-1, d_ff)).reshape(fw_len - 1, bd, ffp)
    hid = small_gate(up_a_s, up_g_s, st_pad(st[:, :, :d_ff]), st_pad(st[:, :, d_ff:]), cwa, cwg, cba, cbg)
    down = mm_bf16(hid, wd_b[None], 0, 256)
    y_sample = residual_norm(h1_s, down, final_norm_g)

    k_prompt = kv_rows_with_meta(kvu, 0, k_small[bd:], n_heads, tm)[None]
    v_prompt = kv_rows_with_meta(kvu, 1, v_small[bd:], n_heads, tm)[None]
    conv_prompt = kvu[2:3, :, s - (cw_len - 1):, :]
    bps = s // FF_ROWS
    last_blk = lambda a: a[bps - 1::bps, SUBLANES - n_t:, :d_ff]
    ffn_prompt = jnp.concatenate([last_blk(tail_a), last_blk(tail_g)], axis=-1)[None]
    k_sample = k_small[:bd].reshape(1, bd, 1, n_heads, e)
    v_sample = v_small[:bd].reshape(1, bd, 1, n_heads, dv)
    conv_sample = jnp.concatenate([state_conv[0][:, 1:], u_small[:bd, None, :]], axis=1)[None]
    up_s = jnp.concatenate([up_a_s[:, :d_ff], up_g_s[:, :d_ff]], axis=-1)
    ffn_sample = jnp.concatenate([state_ffn_conv[0][:, 1:], up_s[:, None, :]], axis=1)[None]
    return (y_p.reshape(b, s, d), y_sample.reshape(bd, 1, d), k_prompt, v_prompt, conv_prompt, ffn_prompt,
            k_sample, v_sample, conv_sample, ffn_sample)
```

```python
import functools
import math

import jax
import jax.numpy as jnp
from jax import lax
from jax.experimental import pallas as pl
from jax.experimental.pallas import tpu as pltpu

F32 = jnp.float32
BF16 = jnp.bfloat16

EPS = 1e-6
MAX_DISTANCE = 128
LAMBDA_INIT_0 = 0.8 - 0.6 * math.exp(-0.3 * 0)
NEG = -1e30
LOG2E = math.log2(math.e)

LANES = 128
SUBLANES = 8
VMEM_LIMIT = 56 * 1024 * 1024

ATTN_T = 512
ATTN_HEADS = 2
ROW_TILE = 512
IN_ROWS = 1024
FF_ROWS = 512
FF_DEC_PAGES = 8
W_COLS = 512
WD_COLS = 256
FF_TILE = 512
FF_SUB = 2
CONV_ROWS = 128
DEC_PAGES = 16


def _cparams(*sem):
    return pltpu.CompilerParams(dimension_semantics=sem, vmem_limit_bytes=VMEM_LIMIT)


def _rms(x, g):
    return x * lax.rsqrt(jnp.mean(x * x, axis=-1, keepdims=True) + EPS) * g


def _layernorm(y, g, b):
    mu = jnp.mean(y, axis=-1, keepdims=True)
    d = y - mu
    var = jnp.mean(d * d, axis=-1, keepdims=True)
    return d * lax.rsqrt(var + EPS) * g + b


def _silu(x):
    return x * jax.nn.sigmoid(x)


def _lambda(lamv_ref):
    a = jnp.sum(lamv_ref[0:1, :] * lamv_ref[1:2, :], axis=-1, keepdims=True)
    b = jnp.sum(lamv_ref[2:3, :] * lamv_ref[3:4, :], axis=-1, keepdims=True)
    return jnp.exp(a) - jnp.exp(b) + LAMBDA_INIT_0


def _t5_bucket(rel, n_buckets):
    n = jnp.maximum(rel, 0)
    max_exact = n_buckets // 2
    nf = jnp.maximum(n, 1).astype(F32)
    large = max_exact + (jnp.log(nf / max_exact) / math.log(MAX_DISTANCE / max_exact)
                         * (n_buckets - max_exact)).astype(jnp.int32)
    large = jnp.minimum(large, n_buckets - 1)
    return jnp.where(n < max_exact, n, large)


def _bias_tile(rel_bias, rel):
    nb = rel_bias.shape[0]
    onehot = (_t5_bucket(rel, nb)[..., None] == jnp.arange(nb, dtype=jnp.int32)).astype(F32)
    table = (rel_bias - rel_bias[nb - 1]).astype(F32) * LOG2E
    b = jnp.einsum("rcn,nh->hrc", onehot, table, precision=lax.Precision.HIGHEST)
    return jnp.where((rel >= 0)[None], b, NEG)


def _toeplitz_kernel(u0_ref, u1_ref, d0_ref, d1_ref):
    t = d0_ref.shape[0]
    for u_ref, d_ref in ((u0_ref, d0_ref), (u1_ref, d1_ref)):
        x = jnp.broadcast_to(u_ref[...], (t, 2 * t))
        d_ref[...] = pltpu.roll(x, 0, 1, stride=1, stride_axis=0)[:, :t]


def toeplitz_tiles(w0, w1, t):
    h = w0.shape[0]

    def wrap(w):
        return jnp.concatenate([w[:, :t][:, ::-1], jnp.zeros((h, 1), F32), w[:, t:][:, ::-1]],
                               axis=1).reshape(h, 1, 2 * t)

    vec = pl.BlockSpec((None, 1, 2 * t), lambda hh: (hh, 0, 0))
    tile = pl.BlockSpec((None, t, t), lambda hh: (hh, 0, 0))
    shape = jax.ShapeDtypeStruct((h, t, t), F32)
    return pl.pallas_call(
        _toeplitz_kernel, grid=(h,), in_specs=[vec, vec], out_specs=[tile, tile],
        out_shape=[shape, shape], name="toeplitz_tiles",
    )(wrap(w0), wrap(w1))


def _rmsnorm_cast_kernel(x_ref, g_ref, o_ref):
    o_ref[...] = _rms(x_ref[...], g_ref[...]).astype(o_ref.dtype)


def rmsnorm_cast(x, g):
    return pl.pallas_call(
        _rmsnorm_cast_kernel,
        out_shape=jax.ShapeDtypeStruct(x.shape, BF16),
        name="rmsnorm_cast",
    )(x, g.reshape(1, -1))


def _mm_cast_kernel(x_ref, w_ref, o_ref, wb_ref):
    wb = w_ref[...].astype(BF16)
    wb_ref[...] = wb
    o_ref[...] = jnp.dot(x_ref[...], wb, preferred_element_type=F32)


def mm_cast(x, w, tn):
    m, k = x.shape
    n = w.shape[1]
    return pl.pallas_call(
        _mm_cast_kernel,
        grid=(n // tn,),
        in_specs=[pl.BlockSpec((m, k), lambda j: (0, 0)),
                  pl.BlockSpec((k, tn), lambda j: (0, j))],
        out_specs=[pl.BlockSpec((m, tn), lambda j: (0, j)), pl.BlockSpec((k, tn), lambda j: (0, j))],
        out_shape=[jax.ShapeDtypeStruct((m, n), F32), jax.ShapeDtypeStruct((k, n), BF16)],
        compiler_params=_cparams("arbitrary"),
        name="mm_cast",
    )(x, w)


def _up_cast_kernel(x_ref, w_ref, o_ref, wb_ref, *, rem):
    hf = pl.program_id(0)
    j = pl.program_id(1)
    last = pl.num_programs(1) - 1
    d, tn = w_ref.shape

    def emit(w):
        wb = w.astype(BF16)
        wb_ref[...] = wb
        o_ref[...] = jnp.dot(x_ref[...], wb, preferred_element_type=F32)

    @pl.when(j < last)
    def _():
        emit(w_ref[...])

    @pl.when((j == last) & (hf == 0))
    def _():
        lane = lax.broadcasted_iota(jnp.int32, (d, tn), 1)
        emit(jnp.where(lane < rem, w_ref[...], 0.0))

    @pl.when((j == last) & (hf == 1))
    def _():
        emit(jnp.concatenate([w_ref[:, tn - rem:], jnp.zeros((d, tn - rem), F32)], axis=1))


def up_cast(x, w_up, d_ff, ffp, tn):
    m, d = x.shape
    n_blk = ffp // tn
    rem = d_ff - (n_blk - 1) * tn
    assert 0 < rem <= tn and rem % LANES == 0 and d_ff >= tn
    el = pl.Element
    return pl.pallas_call(
        functools.partial(_up_cast_kernel, rem=rem),
        grid=(2, n_blk),
        in_specs=[pl.BlockSpec((m, d), lambda hf, j: (0, 0)),
                  pl.BlockSpec((el(d), el(tn)),
                               lambda hf, j: (0, pl.multiple_of(
                                   jnp.minimum(hf * d_ff + j * tn, 2 * d_ff - tn), LANES)))],
        out_specs=[pl.BlockSpec((None, m, tn), lambda hf, j: (hf, 0, j)),
                   pl.BlockSpec((None, d, tn), lambda hf, j: (hf, 0, j))],
        out_shape=[jax.ShapeDtypeStruct((2, m, ffp), F32), jax.ShapeDtypeStruct((2, d, ffp), BF16)],
        compiler_params=_cparams("arbitrary", "arbitrary"),
        name="up_cast",
    )(x, w_up)


def _cast_pad_kernel(w_ref, wb_ref):
    k = w_ref.shape[0]
    wb_ref[0:k, :] = w_ref[...].astype(BF16)
    if wb_ref.shape[0] > k:
        wb_ref[k:, :] = jnp.zeros((wb_ref.shape[0] - k, wb_ref.shape[1]), BF16)


def cast_pad(w, k_pad, tn):
    k, n = w.shape
    return pl.pallas_call(
        _cast_pad_kernel,
        grid=(n // tn,),
        in_specs=[pl.BlockSpec((k, tn), lambda j: (0, j))],
        out_specs=pl.BlockSpec((k_pad, tn), lambda j: (0, j)),
        out_shape=jax.ShapeDtypeStruct((k_pad, n), BF16),
        compiler_params=_cparams("arbitrary"),
        name="cast_pad",
    )(w)


def _mm_bf16_kernel(x_ref, w_ref, o_ref):
    o_ref[...] = jnp.dot(x_ref[...], w_ref[...], preferred_element_type=F32)


def mm_bf16(x, w3, slab, tn):
    m, k = x.shape
    n = w3.shape[-1]
    return pl.pallas_call(
        _mm_bf16_kernel,
        grid=(n // tn,),
        in_specs=[pl.BlockSpec((m, k), lambda j: (0, 0)),
                  pl.BlockSpec((None, k, tn), lambda j: (slab, 0, j))],
        out_specs=pl.BlockSpec((m, tn), lambda j: (0, j)),
        out_shape=jax.ShapeDtypeStruct((m, n), F32),
        compiler_params=_cparams("arbitrary"),
        name="mm_bf16",
    )(x, w3)


def _add_rmsnorm_kernel(x_ref, d_ref, g_ref, h_ref, xn_ref):
    h = x_ref[...] + d_ref[...]
    h_ref[...] = h
    xn_ref[...] = _rms(h, g_ref[...]).astype(xn_ref.dtype)


def add_rmsnorm(x, delta, g):
    return pl.pallas_call(
        _add_rmsnorm_kernel,
        out_shape=[jax.ShapeDtypeStruct(x.shape, F32), jax.ShapeDtypeStruct(x.shape, BF16)],
        name="add_rmsnorm",
    )(x, delta, g.reshape(1, -1))


def _inproj_kernel(x_ref, g_ref, w_ref, wg_ref, ob_ref, of_ref, xn_ref, *, q_scale):
    j = pl.program_id(1)

    @pl.when(j == 0)
    def _():
        xn_ref[...] = _rms(x_ref[...], g_ref[...]).astype(BF16)

    z = jnp.dot(xn_ref[...], w_ref[...], preferred_element_type=F32)

    @pl.when(j == 0)
    def _():
        ob_ref[...] = (z * q_scale).astype(BF16)

    @pl.when((j == 1) | (j == 2))
    def _():
        of_ref[...] = z
        ob_ref[...] = z.astype(BF16)

    @pl.when(j == 3)
    def _():
        gate = jnp.dot(xn_ref[...], wg_ref[...], preferred_element_type=F32)
        of_ref[...] = z * jax.nn.sigmoid(gate)


def prompt_inproj(x, g, w_in, d_attn, q_scale, tm):
    m, d = x.shape
    c = d_attn
    return pl.pallas_call(
        functools.partial(_inproj_kernel, q_scale=q_scale),
        grid=(m // tm, 4),
        in_specs=[pl.BlockSpec((tm, d), lambda i, j: (i, 0)),
                  pl.BlockSpec((1, d), lambda i, j: (0, 0)),
                  pl.BlockSpec((d, c), lambda i, j: (0, j)),
                  pl.BlockSpec((d, c), lambda i, j: (0, 4), pipeline_mode=pl.Buffered(1))],
        out_specs=[pl.BlockSpec((None, tm, c), lambda i, j: (jnp.minimum(j, 2), i, 0)),
                   pl.BlockSpec((None, tm, c), lambda i, j: (jnp.maximum(j - 1, 0), i, 0))],
        out_shape=[jax.ShapeDtypeStruct((3, m, c), BF16), jax.ShapeDtypeStruct((3, m, c), F32)],
        scratch_shapes=[pltpu.VMEM((tm, d), BF16)],
        compiler_params=_cparams("arbitrary", "arbitrary"),
        name="prompt_inproj",
    )(x, g.reshape(1, -1), w_in, w_in)


def _kv_rows_kernel(x_ref, meta_ref, o_ref, *, n_meta, n_heads):
    i = pl.program_id(1)
    n_blk = pl.num_programs(1)
    tm = o_ref.shape[0]
    heads = lambda a: a.reshape(a.shape[0], n_heads, a.shape[1] // n_heads)

    @pl.when(i == 0)
    def _():
        o_ref[0:n_meta] = heads(meta_ref[...])
        o_ref[n_meta:] = heads(x_ref[0, 0, 0:tm - n_meta, :])

    @pl.when((i > 0) & (i < n_blk - 1))
    def _():
        o_ref[...] = heads(x_ref[0, 0])

    @pl.when(i == n_blk - 1)
    def _():
        o_ref[0:n_meta] = heads(x_ref[0, 0, tm - n_meta:tm, :])
        o_ref[n_meta:] = jnp.zeros((tm - n_meta,) + o_ref.shape[1:], o_ref.dtype)


def kv_rows_with_meta(kvu, slab, meta_rows, n_heads, tm):
    _, b, s, c = kvu.shape
    n_meta = meta_rows.shape[0]
    n_blk = s // tm + 1
    assert n_meta % SUBLANES == 0 and n_meta < tm
    el = pl.Element
    return pl.pallas_call(
        functools.partial(_kv_rows_kernel, n_meta=n_meta, n_heads=n_heads),
        grid=(b, n_blk),
        in_specs=[pl.BlockSpec((el(1), el(1), el(tm), el(c)),
                               lambda bi, i: (slab, bi, pl.multiple_of(
                                   jnp.clip(i * tm - n_meta, 0, s - tm), SUBLANES), 0)),
                  pl.BlockSpec((n_meta, c), lambda bi, i: (0, 0))],
        out_specs=pl.BlockSpec((None, tm, n_heads, c // n_heads), lambda bi, i: (bi, i, 0, 0)),
        out_shape=jax.ShapeDtypeStruct((b, n_meta + s, n_heads, c // n_heads), F32),
        compiler_params=_cparams("arbitrary", "arbitrary"),
        name="kv_rows_with_meta",
    )(kvu, meta_rows)


def _stack_q(q, dk):
    lane = lax.broadcasted_iota(jnp.int32, q.shape, 1)
    zero = jnp.zeros_like(q)
    return jnp.concatenate([jnp.where(lane < dk, q, zero), jnp.where(lane >= dk, q, zero)], axis=0)


def _scores(qq, kblk):
    return lax.dot_general(qq, kblk, (((1,), (1,)), ((), ())), preferred_element_type=F32)


def _add_tile(s, tile):
    t, c = tile.shape
    return (s.reshape(2, t, c) + tile[None]).reshape(2 * t, c)


def _with_ones(v):
    return jnp.concatenate([v, jnp.ones_like(v)], axis=-1)


def _softmax_update(s, vx, carry):
    m, acc = carry
    m_new = jnp.maximum(m, jnp.max(s, axis=-1, keepdims=True))
    alpha = jnp.exp2(m - m_new)
    p = jnp.exp2(s - m_new).astype(BF16)
    return m_new, alpha * acc + jnp.dot(p, vx, preferred_element_type=F32)


def _combine_heads_out(acc, lam, g, t):
    dv = acc.shape[-1] // 2
    o = acc[:t, :dv] / acc[:t, dv:] - lam * (acc[t:, :dv] / acc[t:, dv:])
    return _rms(o, g) * (1.0 - LAMBDA_INIT_0)


def _prompt_attn_kernel(q_ref, k_ref, v_ref, km_ref, vm_ref, d0_ref, d1_ref, dm_ref,
                        lamv_ref, g_ref, o_ref, vx_ref, qa_ref, qb_ref, *, t, dk, n_grp):
    qi = pl.program_id(2)
    r = 2 * t
    e = 2 * dk
    dv = v_ref.shape[-1] // n_grp
    heads = range(n_grp)
    hs = lambda hh, w: slice(hh * w, (hh + 1) * w)

    @pl.when(qi == 0)
    def _():
        for hh in heads:
            vx_ref[hh, :, :dv] = v_ref[:, hs(hh, dv)]
            vx_ref[hh, :, dv:] = jnp.ones((v_ref.shape[0], dv), BF16)

    qq = [_stack_q(q_ref[:, hs(hh, e)], dk) for hh in heads]
    rows = lambda j: pl.ds(pl.multiple_of(j * t, t), t)
    kblk = lambda hh, j: k_ref[rows(j), hs(hh, e)]
    vxblk = lambda hh, j: vx_ref[hh, rows(j), :]

    n_far = jnp.maximum(qi - 1, 0)

    for hh in heads:
        qa_ref[hh] = qq[hh]
        qb_ref[hh] = qq[hh]

    def far_body(j, carry):
        out = []
        for hh in heads:
            m, acc = carry[hh]
            kb = kblk(hh, j)
            m_new = jnp.maximum(m, jnp.max(_scores(qa_ref[hh], kb), axis=-1, keepdims=True))
            alpha = jnp.exp2(m - m_new)
            p = jnp.exp2(_scores(qb_ref[hh], kb) - m_new).astype(BF16)
            out.append((m_new, alpha * acc + jnp.dot(p, vxblk(hh, j), preferred_element_type=F32)))
        return tuple(out)

    init = tuple((jnp.full((r, 1), NEG, F32), jnp.zeros((r, 2 * dv), F32)) for hh in heads)
    carry = lax.fori_loop(0, n_far, far_body, init)

    lam = _lambda(lamv_ref)
    for hh in heads:
        m, acc = carry[hh]
        s_prev = _scores(qq[hh], kblk(hh, n_far))
        s_sub = _add_tile(s_prev, jnp.where(qi >= 1, d1_ref[hh], NEG))
        s_diag = _add_tile(_scores(qq[hh], kblk(hh, qi)), d0_ref[hh])
        s_meta = _add_tile(_scores(qq[hh], km_ref[:, hs(hh, e)]), jnp.where(qi == 0, dm_ref[hh], 0.0))
        c = _softmax_update(s_meta, _with_ones(vm_ref[:, hs(hh, dv)]), (m, acc))
        c = _softmax_update(s_diag, vxblk(hh, qi), c)
        m, acc = _softmax_update(s_sub, vxblk(hh, n_far), c)
        o_ref[:, hs(hh, dv)] = _combine_heads_out(acc, lam, g_ref[...], t).astype(o_ref.dtype)


def prompt_attention(qkv, kmb, vmb, d0, d1, dm, lamv, subln_g, n_heads, t, n_grp):
    _, b, s, width = qkv.shape
    e = dv = width // n_heads
    nm = kmb.shape[0]
    ge, gv = n_grp * e, n_grp * dv
    tile = lambda c: pl.BlockSpec((n_grp, t, c), lambda bi, h, qi: (h, 0, 0))
    return pl.pallas_call(
        functools.partial(_prompt_attn_kernel, t=t, dk=e // 2, n_grp=n_grp),
        grid=(b, n_heads // n_grp, s // t),
        in_specs=[
            pl.BlockSpec((None, None, t, ge), lambda bi, h, qi: (0, bi, qi, h)),
            pl.BlockSpec((None, None, s, ge), lambda bi, h, qi: (1, bi, 0, h)),
            pl.BlockSpec((None, None, s, gv), lambda bi, h, qi: (2, bi, 0, h)),
            pl.BlockSpec((nm, ge), lambda bi, h, qi: (0, h)),
            pl.BlockSpec((nm, gv), lambda bi, h, qi: (0, h)),
            tile(t), tile(t), tile(nm),
            pl.BlockSpec((4, e // 2), lambda bi, h, qi: (0, 0)),
            pl.BlockSpec((1, dv), lambda bi, h, qi: (0, 0)),
        ],
        out_specs=pl.BlockSpec((None, t, gv), lambda bi, h, qi: (bi, qi, h)),
        out_shape=jax.ShapeDtypeStruct((b, s, n_heads * dv), BF16),
        scratch_shapes=[pltpu.VMEM((n_grp, s, 2 * dv), BF16),
                        pltpu.VMEM((n_grp, 2 * t, e), BF16), pltpu.VMEM((n_grp, 2 * t, e), BF16)],
        compiler_params=_cparams("arbitrary", "arbitrary", "arbitrary"),
        name="prompt_attention",
    )(qkv, qkv, qkv, kmb, vmb, d0, d1, dm, lamv, subln_g.reshape(1, -1))


def _meta_attn_kernel(q_ref, k_ref, v_ref, d_ref, lamv_ref, g_ref, o_ref, *, n_heads, dk):
    t = q_ref.shape[0]
    e = 2 * dk
    dv = v_ref.shape[-1] // n_heads
    lam = _lambda(lamv_ref)
    for h in range(n_heads):
        qq = _stack_q(q_ref[:, h * e:(h + 1) * e], dk)
        kblk = k_ref[:, h * e:(h + 1) * e].astype(BF16)
        vx = _with_ones(v_ref[:, h * dv:(h + 1) * dv].astype(BF16))
        carry = (jnp.full((2 * t, 1), NEG, F32), jnp.zeros((2 * t, 2 * dv), F32))
        m, acc = _softmax_update(_add_tile(_scores(qq, kblk), d_ref[h]), vx, carry)
        o_ref[:, h * dv:(h + 1) * dv] = _combine_heads_out(acc, lam, g_ref[...], t).astype(o_ref.dtype)


def meta_attention(q, k, v, dmeta, lamv, subln_g, n_heads):
    t = q.shape[0]
    dv = v.shape[-1] // n_heads
    return pl.pallas_call(
        functools.partial(_meta_attn_kernel, n_heads=n_heads, dk=k.shape[-1] // n_heads // 2),
        out_shape=jax.ShapeDtypeStruct((t, n_heads * dv), BF16),
        name="meta_attention",
    )(q, k, v, dmeta, lamv, subln_g.reshape(1, -1))


def _decode_init(q_ref, kn_ref, vn_ref, b0_ref, qq_ref, m_ref, l_ref, acc_ref, dk):
    qf = q_ref[...].astype(F32)
    lane = lax.broadcasted_iota(jnp.int32, qf.shape, 1)
    qq = jnp.concatenate([jnp.where(lane < dk, qf, 0.0), jnp.where(lane >= dk, qf, 0.0)], axis=0)
    qq_ref[...] = qq.astype(BF16)
    kn = jnp.concatenate([kn_ref[...], kn_ref[...]], axis=0)
    s_new = jnp.sum(qq * kn, axis=-1, keepdims=True) + b0_ref[...]
    m_ref[...] = s_new
    l_ref[...] = jnp.ones_like(s_new)
    acc_ref[...] = jnp.concatenate([vn_ref[...], vn_ref[...]], axis=0)


def _decode_pages(*args):
    for _ in _decode_pages_phases(*args):
        pass


def _decode_pages_phases(is_last_group, mb_ref, mbl_ref, k_refs, v_refs, qq_ref, m_ref, l_ref, acc_ref,
                         n_heads):
    qq = qq_ref[...]
    e = qq.shape[-1]
    n_p = len(k_refs)
    s_pages = []
    for p in range(n_p):
        bias = mb_ref[...]
        if p == n_p - 1:
            bias = jnp.where(is_last_group, mbl_ref[...], bias)
        rows = k_refs[p].shape[0] * n_heads
        s_pages.append(_scores(qq, k_refs[p][...].reshape(rows, e).astype(BF16)) + bias)
    yield
    m_old = m_ref[...]
    m_loc = s_pages[0]
    for s in s_pages[1:]:
        m_loc = jnp.maximum(m_loc, s)
    m_new = jnp.maximum(m_old, jnp.max(m_loc, axis=-1, keepdims=True))
    alpha = jnp.exp2(m_old - m_new)
    weights = [jnp.exp2(s - m_new) for s in s_pages]
    yield
    l_sum = None
    pv = None
    for p in range(n_p):
        w = weights[p]
        rows = v_refs[p].shape[0] * n_heads
        vblk = v_refs[p][...].reshape(rows, v_refs[p].shape[-1]).astype(BF16)
        d = jnp.dot(w.astype(BF16), vblk, preferred_element_type=F32)
        l_sum = w if l_sum is None else l_sum + w
        pv = d if pv is None else pv + d
    l_ref[...] = alpha * l_ref[...] + jnp.sum(l_sum, axis=-1, keepdims=True)
    acc_ref[...] = alpha * acc_ref[...] + pv
    m_ref[...] = m_new


def _decode_final(lamv_ref, g_ref, o_ref, l_ref, acc_ref, n_heads):
    acc, l = acc_ref[...], l_ref[...]
    o = acc[:n_heads] / l[:n_heads] - _lambda(lamv_ref) * (acc[n_heads:] / l[n_heads:])
    o_ref[...] = (_rms(o, g_ref[...]) * (1.0 - LAMBDA_INIT_0)).astype(o_ref.dtype)


def _decode_attn_kernel(pt_ref, q_ref, kn_ref, vn_ref, mb_ref, mbl_ref, b0_ref, lamv_ref, g_ref, *rest,
                        n_pages_step, n_heads, dk):
    k_refs = rest[:n_pages_step]
    v_refs = rest[n_pages_step:2 * n_pages_step]
    o_ref = rest[2 * n_pages_step]
    state = rest[2 * n_pages_step + 1:]
    del pt_ref
    g = pl.program_id(1)
    last = g == pl.num_programs(1) - 1

    @pl.when(g == 0)
    def _():
        _decode_init(q_ref, kn_ref, vn_ref, b0_ref, *state, dk)

    _decode_pages(last, mb_ref, mbl_ref, k_refs, v_refs, *state, n_heads)

    @pl.when(last)
    def _():
        _decode_final(lamv_ref, g_ref, o_ref, state[2], state[3], n_heads)


def decode_attention(page_table, q, k_new, v_new, cache_k, cache_v, mb, mbl, b0, lamv, subln_g,
                     pages_per_step):
    bd, n_pages = page_table.shape
    _, page, n_heads, e = cache_k.shape
    dv = cache_v.shape[-1]
    p_step = pages_per_step
    r = 2 * n_heads
    seq_blk = lambda w: pl.BlockSpec((None, n_heads, w), lambda b, g, pt: (b, 0, 0))
    const2 = lambda b, g, pt: (0, 0)

    def page_spec(p, w):
        return pl.BlockSpec((None, page, n_heads, w), lambda b, g, pt: (pt[b, g * p_step + p], 0, 0, 0))

    grid_spec = pltpu.PrefetchScalarGridSpec(
        num_scalar_prefetch=1,
        grid=(bd, n_pages // p_step),
        in_specs=[seq_blk(e), seq_blk(e), seq_blk(dv),
                  pl.BlockSpec((r, page * n_heads), const2),
                  pl.BlockSpec((r, page * n_heads), const2),
                  pl.BlockSpec((r, 1), const2),
                  pl.BlockSpec((4, e // 2), const2),
                  pl.BlockSpec((1, dv), const2)]
                 + [page_spec(p, e) for p in range(p_step)]
                 + [page_spec(p, dv) for p in range(p_step)],
        out_specs=seq_blk(dv),
        scratch_shapes=[pltpu.VMEM((r, e), BF16), pltpu.VMEM((r, 1), F32),
                        pltpu.VMEM((r, 1), F32), pltpu.VMEM((r, dv), F32)],
    )
    return pl.pallas_call(
        functools.partial(_decode_attn_kernel, n_pages_step=p_step, n_heads=n_heads, dk=e // 2),
        grid_spec=grid_spec,
        out_shape=jax.ShapeDtypeStruct((bd, n_heads, dv), BF16),
        compiler_params=_cparams("arbitrary", "arbitrary"),
        name="decode_attention",
    )(page_table, q, k_new, v_new, mb, mbl, b0, lamv, subln_g.reshape(1, -1),
      *([cache_k] * p_step), *([cache_v] * p_step))


def _conv_rows(src_ref, base, n_rows, cw_ref, conv_w):
    acc = src_ref[pl.ds(base, n_rows), :] * cw_ref[0:1, :]
    for w in range(1, conv_w):
        acc = acc + src_ref[pl.ds(base + w, n_rows), :] * cw_ref[w:w + 1, :]
    return acc


def _conv_tiles(win, cwb_ref, lanes, conv_w, first, n_out):
    y = None
    for k in range(SUBLANES):
        taps = [w for w in range(conv_w) if (first + w) % SUBLANES == k]
        if not taps:
            continue
        n_p = n_out if k == 0 else n_out + 1
        part = None
        for w in taps:
            j = (first + w) // SUBLANES
            term = win[j:j + n_p] * cwb_ref[w, :, lanes][None]
            part = term if part is None else part + term
        part = part.reshape(n_p * SUBLANES, part.shape[-1])
        part = part if k == 0 else part[k:k + n_out * SUBLANES]
        y = part if y is None else y + part
    return y


def _conv_merge_kernel(u_ref, uh_ref, head_ref, cwb_ref, cb_ref, lg_ref, lb_ref, o_ref, x_ref, w_ref,
                       y_ref, s_ref, yc_ref, c_ref, *, conv_w, halo, rows):
    i = pl.program_id(1)
    tm, c = u_ref.shape
    da = o_ref.shape[-1]
    s_ref[0:halo, :] = jnp.where(i == 0, head_ref[...], uh_ref[...])
    s_ref[halo:, :] = u_ref[...]
    first = halo - (conv_w - 1)
    n_win = (rows + halo) // SUBLANES

    y_ref[...] = x_ref[...] + jnp.dot(o_ref[...], w_ref[0:da, :], preferred_element_type=F32)
    for ci in range(tm // rows):
        r0 = ci * rows
        for lb in range(c // LANES):
            lanes = slice(lb * LANES, (lb + 1) * LANES)
            win = s_ref[r0:r0 + rows + halo, lanes].reshape(n_win, SUBLANES, LANES)
            yc_ref[r0:r0 + rows, lanes] = _conv_tiles(win, cwb_ref, lanes, conv_w, first, rows // SUBLANES)
        y = yc_ref[r0:r0 + rows, :] + cb_ref[...]
        c_ref[r0:r0 + rows, :] = _silu(_layernorm(y, lg_ref[...], lb_ref[...])).astype(c_ref.dtype)
    y_ref[...] += jnp.dot(c_ref[...], w_ref[da:, :], preferred_element_type=F32)


def conv_merge(kvu, head, conv_w, conv_b, ln_g, ln_b, o_n, x, w_o, tm):
    _, b, s, c = kvu.shape
    d = x.shape[-1]
    halo = head.shape[0]
    w = conv_w.shape[0]
    per = tm // halo
    vec = lambda a: a.reshape(1, -1)
    const = lambda bi, i: (0, 0)
    blk = lambda width: pl.BlockSpec((None, tm, width), lambda bi, i: (bi, i, 0))
    cwb = jnp.broadcast_to(conv_w[:, None, :], (w, SUBLANES, c))
    return pl.pallas_call(
        functools.partial(_conv_merge_kernel, conv_w=w, halo=halo, rows=CONV_ROWS),
        grid=(b, s // tm),
        in_specs=[pl.BlockSpec((None, None, tm, c), lambda bi, i: (2, bi, i, 0)),
                  pl.BlockSpec((None, None, halo, c), lambda bi, i: (2, bi, jnp.maximum(i * per - 1, 0), 0)),
                  pl.BlockSpec((halo, c), const),
                  pl.BlockSpec((w, SUBLANES, c), lambda bi, i: (0, 0, 0)),
                  pl.BlockSpec((1, c), const), pl.BlockSpec((1, c), const), pl.BlockSpec((1, c), const),
                  blk(c), blk(d),
                  pl.BlockSpec((d, d), const, pipeline_mode=pl.Buffered(1))],
        out_specs=blk(d),
        out_shape=jax.ShapeDtypeStruct((b, s, d), F32),
        scratch_shapes=[pltpu.VMEM((tm + halo, c), F32), pltpu.VMEM((tm, c), F32), pltpu.VMEM((tm, c), BF16)],
        compiler_params=_cparams("arbitrary", "arbitrary"),
        name="conv_merge",
    )(kvu, kvu, head, cwb, vec(conv_b), vec(ln_g), vec(ln_b), o_n, x, w_o)


def _small_conv_kernel(za_ref, zg_ref, st_ref, cw_ref, cb_ref, lg_ref, lb_ref, u_ref, c_ref, s_ref,
                       *, conv_w, n_sample):
    n_meta = za_ref.shape[0] - n_sample
    u = za_ref[...] * jax.nn.sigmoid(zg_ref[...])
    u_ref[...] = u
    y = u[:n_sample] * cw_ref[conv_w - 1:conv_w, :]
    for w in range(conv_w - 1):
        y = y + st_ref[w] * cw_ref[w:w + 1, :]
    y_s = y + cb_ref[...]
    hist = s_ref.shape[0] - n_meta
    s_ref[0:hist, :] = jnp.zeros((hist, s_ref.shape[1]), F32)
    s_ref[hist:, :] = u[n_sample:]
    y_m = _conv_rows(s_ref, hist - (conv_w - 1), n_meta, cw_ref, conv_w) + cb_ref[...]
    yy = jnp.concatenate([y_s, y_m], axis=0)
    c_ref[...] = _silu(_layernorm(yy, lg_ref[...], lb_ref[...])).astype(c_ref.dtype)


def small_conv_module(za, zg, state_t, conv_w, conv_b, ln_g, ln_b, n_sample):
    m, c = za.shape
    w = conv_w.shape[0]
    n_meta = m - n_sample
    hist = SUBLANES * ((w - 1 + SUBLANES - 1) // SUBLANES)
    vec = lambda a: a.reshape(1, -1)
    return pl.pallas_call(
        functools.partial(_small_conv_kernel, conv_w=w, n_sample=n_sample),
        out_shape=[jax.ShapeDtypeStruct((m, c), F32), jax.ShapeDtypeStruct((m, c), BF16)],
        scratch_shapes=[pltpu.VMEM((hist + n_meta, c), F32)],
        name="small_conv_module",
    )(za, zg, state_t, conv_w, vec(conv_b), vec(ln_g), vec(ln_b))


def _gelu(x):
    return 0.5 * x * (1.0 + lax.erf(x * math.sqrt(0.5)))


def _ffn_kernel(pt_ref, x_ref, g2_ref, gf_ref, wa_ref, wg_ref, wd_ref, cwa_ref, cwg_ref, cba_ref, cbg_ref,
                ha_ref, hg_ref,
                q_ref, kn_ref, vn_ref, mb_ref, mbl_ref, b0_ref, lamv_ref, sg_ref, *rest,
                blocks_per_seq, conv_w, n_sub, dec_pages, dec_groups, n_host, n_heads, dk):
    k_refs = rest[:dec_pages]
    v_refs = rest[dec_pages:2 * dec_pages]
    y_ref, ta_ref, tg_ref, od_ref = rest[2 * dec_pages:2 * dec_pages + 4]
    xn_ref, ea_ref, eg_ref, ca_ref, cg_ref = rest[2 * dec_pages + 4:2 * dec_pages + 9]
    dec_state = rest[2 * dec_pages + 9:]
    del pt_ref
    i = pl.program_id(0)
    f = pl.program_id(1)
    n_f = pl.num_programs(1)
    tm = x_ref.shape[0]
    tf = wa_ref.shape[1]
    ts = tf // n_sub
    pad = SUBLANES
    step = i * n_f + f
    grp = step % dec_groups
    hosting = step < n_host * dec_groups

    @pl.when(f == 0)
    def _():
        x = x_ref[...]
        xn_ref[...] = _rms(x, g2_ref[...]).astype(BF16)
        y_ref[...] = x

    @pl.when((grp == 0) & hosting)
    def _():
        _decode_init(q_ref, kn_ref, vn_ref, b0_ref, *dec_state, dk)

    hosted = _decode_pages_phases(grp == dec_groups - 1, mb_ref, mbl_ref, k_refs, v_refs, *dec_state, n_heads)
    xn = xn_ref[...]
    first = (i % blocks_per_seq) == 0
    base = pad - (conv_w - 1)
    down = None
    for sc in range(n_sub):
        next(hosted, None)
        cols = slice(sc * ts, (sc + 1) * ts)
        up_a = jnp.dot(xn, wa_ref[:, cols], preferred_element_type=F32)
        up_g = jnp.dot(xn, wg_ref[:, cols], preferred_element_type=F32)
        ta_ref[:, cols] = up_a[tm - pad:]
        tg_ref[:, cols] = up_g[tm - pad:]
        ea_ref[sc, 0:pad, :] = jnp.where(first, ha_ref[:, cols], ca_ref[f, :, cols])
        eg_ref[sc, 0:pad, :] = jnp.where(first, hg_ref[:, cols], cg_ref[f, :, cols])
        ea_ref[sc, pad:, :] = up_a
        eg_ref[sc, pad:, :] = up_g
        ca_ref[f, :, cols] = up_a[tm - pad:]
        cg_ref[f, :, cols] = up_g[tm - pad:]
        conv_a = _conv_rows(ea_ref.at[sc], base, tm, cwa_ref.at[:, cols], conv_w) + cba_ref[:, cols]
        conv_g = _conv_rows(eg_ref.at[sc], base, tm, cwg_ref.at[:, cols], conv_w) + cbg_ref[:, cols]
        hidden = (_gelu(conv_g) * conv_a).astype(BF16)
        d = jnp.dot(hidden, wd_ref[cols, :], preferred_element_type=F32)
        down = d if down is None else down + d
    y_ref[...] += down
    for _ in hosted:
        pass

    @pl.when((grp == dec_groups - 1) & hosting)
    def _():
        _decode_final(lamv_ref, sg_ref, od_ref, dec_state[2], dec_state[3], n_heads)

    @pl.when(f == n_f - 1)
    def _():
        y_ref[...] = _rms(y_ref[...], gf_ref[...])


def prompt_ffn(x, norm2_g, final_g, wab, wd, cwa, cwg, cba, cbg, head_a, head_g, rows_per_seq,
               tm, tf, dec, dec_pages):
    m, d = x.shape
    ffp = wab.shape[-1]
    n_f = ffp // tf
    bps = rows_per_seq // tm
    w = cwa.shape[0]
    ts = tf // FF_SUB
    page_table, q, k_new, v_new, cache_k, cache_v, mb, mbl, b0, lamv, subln_g = dec
    _, page, n_heads, e = cache_k.shape
    dv = cache_v.shape[-1]
    groups = page_table.shape[1] // dec_pages
    n_host = min(page_table.shape[0], (m // tm) * n_f // groups)
    assert page_table.shape[1] % dec_pages == 0 and n_host >= 1
    r = 2 * n_heads
    row = lambda i, f, pt: (i, 0)
    const = lambda i, f, pt: (0, 0)
    col = lambda i, f, pt: (0, f)
    tail = pl.BlockSpec((None, SUBLANES, tf), lambda i, f, pt: (i, 0, f))
    seq = lambda i, f: jnp.minimum((i * n_f + f) // groups, n_host - 1)
    seq_blk = lambda wd_: pl.BlockSpec((None, n_heads, wd_), lambda i, f, pt: (seq(i, f), 0, 0))

    n_steps = (m // tm) * n_f
    pt_steps = page_table[:n_host].reshape(n_host * groups, dec_pages)
    pt_steps = jnp.concatenate([pt_steps, jnp.broadcast_to(pt_steps[-1:], (n_steps - n_host * groups, dec_pages))],
                               axis=0).reshape(n_steps * dec_pages)

    def page_spec(p, wd_):
        return pl.BlockSpec((None, page, n_heads, wd_),
                            lambda i, f, pt: (pt[(i * n_f + f) * dec_pages + p], 0, 0, 0))

    grid_spec = pltpu.PrefetchScalarGridSpec(
        num_scalar_prefetch=1,
        grid=(m // tm, n_f),
        in_specs=[pl.BlockSpec((tm, d), row),
                  pl.BlockSpec((1, d), const), pl.BlockSpec((1, d), const),
                  pl.BlockSpec((None, d, tf), lambda i, f, pt: (0, 0, f)),
                  pl.BlockSpec((None, d, tf), lambda i, f, pt: (1, 0, f)),
                  pl.BlockSpec((tf, d), lambda i, f, pt: (f, 0)),
                  pl.BlockSpec((w, tf), col), pl.BlockSpec((w, tf), col),
                  pl.BlockSpec((1, tf), col), pl.BlockSpec((1, tf), col),
                  pl.BlockSpec((SUBLANES, tf), col), pl.BlockSpec((SUBLANES, tf), col),
                  seq_blk(e), seq_blk(e), seq_blk(dv),
                  pl.BlockSpec((r, page * n_heads), const), pl.BlockSpec((r, page * n_heads), const),
                  pl.BlockSpec((r, 1), const), pl.BlockSpec((4, e // 2), const), pl.BlockSpec((1, dv), const)]
                 + [page_spec(p, e) for p in range(dec_pages)]
                 + [page_spec(p, dv) for p in range(dec_pages)],
        out_specs=[pl.BlockSpec((tm, d), row), tail, tail, seq_blk(dv)],
        scratch_shapes=[pltpu.VMEM((tm, d), BF16),
                        pltpu.VMEM((FF_SUB, tm + SUBLANES, ts), F32),
                        pltpu.VMEM((FF_SUB, tm + SUBLANES, ts), F32),
                        pltpu.VMEM((n_f, SUBLANES, tf), F32), pltpu.VMEM((n_f, SUBLANES, tf), F32),
                        pltpu.VMEM((r, e), BF16), pltpu.VMEM((r, 1), F32),
                        pltpu.VMEM((r, 1), F32), pltpu.VMEM((r, dv), F32)],
    )
    return pl.pallas_call(
        functools.partial(_ffn_kernel, blocks_per_seq=bps, conv_w=w, n_sub=FF_SUB, dec_pages=dec_pages,
                          dec_groups=groups, n_host=n_host, n_heads=n_heads, dk=e // 2),
        grid_spec=grid_spec,
        out_shape=[jax.ShapeDtypeStruct((m, d), F32),
                   jax.ShapeDtypeStruct((m // tm, SUBLANES, ffp), F32),
                   jax.ShapeDtypeStruct((m // tm, SUBLANES, ffp), F32),
                   jax.ShapeDtypeStruct((n_host, n_heads, dv), BF16)],
        compiler_params=_cparams("arbitrary", "arbitrary"),
        name="prompt_ffn",
    )(pt_steps, x, norm2_g.reshape(1, -1), final_g.reshape(1, -1), wab, wab, wd, cwa, cwg,
      cba.reshape(1, -1), cbg.reshape(1, -1), head_a, head_g,
      q[:n_host], k_new[:n_host], v_new[:n_host], mb, mbl, b0, lamv, subln_g.reshape(1, -1),
      *([cache_k] * dec_pages), *([cache_v] * dec_pages))


def _small_gate_kernel(ua_ref, ug_ref, sa_ref, sg_ref, cwa_ref, cwg_ref, cba_ref, cbg_ref, h_ref,
                       *, conv_w):
    def conv(u_ref, s_ref, cw_ref, cb_ref):
        y = u_ref[...] * cw_ref[conv_w - 1:conv_w, :] + cb_ref[...]
        for w in range(conv_w - 1):
            y = y + s_ref[w] * cw_ref[w:w + 1, :]
        return y

    a = conv(ua_ref, sa_ref, cwa_ref, cba_ref)
    g = conv(ug_ref, sg_ref, cwg_ref, cbg_ref)
    h_ref[...] = (_gelu(g) * a).astype(h_ref.dtype)


def small_gate(up_a, up_g, st_a, st_g, cwa, cwg, cba, cbg):
    return pl.pallas_call(
        functools.partial(_small_gate_kernel, conv_w=cwa.shape[0]),
        out_shape=jax.ShapeDtypeStruct(up_a.shape, BF16),
        name="small_gate",
    )(up_a, up_g, st_a, st_g, cwa, cwg, cba.reshape(1, -1), cbg.reshape(1, -1))


def _residual_norm_kernel(x_ref, d_ref, g_ref, y_ref):
    y_ref[...] = _rms(x_ref[...] + d_ref[...], g_ref[...])


def residual_norm(x, delta, g):
    return pl.pallas_call(
        _residual_norm_kernel,
        out_shape=jax.ShapeDtypeStruct(x.shape, F32),
        name="residual_norm",
    )(x, delta, g.reshape(1, -1))


def kernel(x_prompt, x_sample, cache_k, cache_v, page_table, state_conv, state_ffn_conv, meta_tokens,
           rel_bias, norm1_g, w_in, lambda_q1, lambda_k1, lambda_q2, lambda_k2, subln_g, conv_w, conv_b,
           conv_ln_g, conv_ln_b, w_o, norm2_g, ffn_w_up, ffn_conv_w, ffn_conv_b, ffn_w_down,
           final_norm_g):
    depth = w_in.shape[0]
    assert depth == 1, "single-layer step only"
    b, s, d = x_prompt.shape
    bd, t_dec = x_sample.shape[:2]
    assert t_dec == 1
    n_meta = meta_tokens.shape[0]
    n_pool, page, n_heads, e = cache_k.shape[1:]
    dv = cache_v.shape[-1]
    dk = e // 2
    d_attn = n_heads * dv
    d_conv = conv_w.shape[-1]
    cw_len = conv_w.shape[1]
    fw_len = ffn_conv_w.shape[1]
    d_ff = ffn_w_down.shape[1]
    n_buckets = rel_bias.shape[0]
    t = ATTN_T
    tm = ROW_TILE
    tf = FF_TILE
    assert e == dv == LANES and d_conv == d_attn and w_in.shape[-1] == 5 * d_attn
    assert s % tm == 0 and s % t == 0 and n_meta <= t and page_table.shape[1] % DEC_PAGES == 0
    assert s % FF_ROWS == 0 and (b * s) % IN_ROWS == 0
    me = n_buckets // 2
    sat = math.ceil(me * (MAX_DISTANCE / me) ** ((n_buckets - 1 - me) / (n_buckets - me))) + 1
    assert t + 1 >= sat and page + 1 >= sat
    halo = SUBLANES * ((cw_len - 1 + SUBLANES - 1) // SUBLANES)
    assert n_meta <= halo and tm % halo == 0 and tm % CONV_ROWS == 0 and fw_len - 1 <= SUBLANES

    ffp = tf * ((d_ff + tf - 1) // tf)
    pad_c = lambda a: jnp.pad(a, ((0, 0), (0, ffp - d_ff)))
    cwa, cwg = pad_c(ffn_conv_w[0, :, :d_ff]), pad_c(ffn_conv_w[0, :, d_ff:])
    cba, cbg = pad_c(ffn_conv_b[:, :d_ff])[0], pad_c(ffn_conv_b[:, d_ff:])[0]
    lamv = jnp.stack([lambda_q1[0], lambda_k1[0], lambda_q2[0], lambda_k2[0]]).astype(F32)
    q_scale = dk ** -0.5 * LOG2E

    ar = lambda n: jnp.arange(n, dtype=jnp.int32)
    by_rel = _bias_tile(rel_bias, ar(2 * t)[None, :])[:, 0, :]
    d0, d1 = toeplitz_tiles(jnp.concatenate([jnp.full((n_heads, t - 1), NEG, F32), by_rel[:, :t]], axis=1),
                            by_rel[:, 1:], t)
    dm = _bias_tile(rel_bias, n_meta + ar(t)[:, None] - ar(n_meta)[None, :])
    dmeta = _bias_tile(rel_bias, ar(n_meta)[:, None] - ar(n_meta)[None, :])
    stack2 = lambda a: jnp.concatenate([a, a], axis=0)
    same_head = (ar(n_heads)[:, None, None] == ar(n_heads)[None, None, :])
    blast = _bias_tile(rel_bias, (page - ar(page))[None, :])[:, 0, :]
    mb = stack2(jnp.broadcast_to(jnp.where(same_head, 0.0, NEG), (n_heads, page, n_heads))
                ).reshape(2 * n_heads, page * n_heads)
    mbl = stack2(jnp.where(same_head, blast[:, :, None], NEG)).reshape(2 * n_heads, page * n_heads)
    b0 = stack2(_bias_tile(rel_bias, jnp.zeros((1, 1), jnp.int32))[:, 0, :])

    x_small = jnp.concatenate([x_sample[:, 0, :], meta_tokens.astype(F32)], axis=0)
    xn_small = rmsnorm_cast(x_small, norm1_g[0])
    z, w_in_b = mm_cast(xn_small, w_in[0], W_COLS)
    q_small = (z[:, :d_attn] * q_scale).astype(BF16)
    k_small = z[:, d_attn:2 * d_attn]
    v_small = z[:, 2 * d_attn:3 * d_attn]
    state_t = jnp.transpose(state_conv[0], (1, 0, 2))
    u_small, c_small = small_conv_module(z[:, 3 * d_attn:4 * d_attn], z[:, 4 * d_attn:], state_t,
                                         conv_w[0], conv_b[0], conv_ln_g[0], conv_ln_b[0], bd)
    o_meta = meta_attention(q_small[bd:], k_small[bd:], v_small[bd:], dmeta, lamv, subln_g[0], n_heads)
    hm_delta, w_o_b = mm_cast(jnp.concatenate([o_meta, c_small[bd:]], axis=1), w_o[0], W_COLS)
    _, xn2_meta = add_rmsnorm(x_small[bd:], hm_delta, norm2_g[0])
    up_meta, wab_b = up_cast(xn2_meta, ffn_w_up[0], d_ff, ffp, tf)
    wd_b = cast_pad(ffn_w_down[0], ffp, WD_COLS)

    xp = x_prompt.reshape(b * s, d)
    qkv_b, kvu = prompt_inproj(xp, norm1_g[0], w_in_b, d_attn, q_scale, IN_ROWS)
    qkv_b = qkv_b.reshape(3, b, s, d_attn)
    kvu = kvu.reshape(3, b, s, d_attn)
    o_p = prompt_attention(qkv_b, k_small[bd:].astype(BF16), v_small[bd:].astype(BF16),
                           d0, d1, dm, lamv, subln_g[0], n_heads, t, ATTN_HEADS)
    head_u = jnp.concatenate([jnp.zeros((halo - n_meta, d_conv), F32), u_small[bd:]], axis=0)
    h1_p = conv_merge(kvu, head_u, conv_w[0], conv_b[0], conv_ln_g[0], conv_ln_b[0], o_p, x_prompt, w_o_b,
                      tm).reshape(b * s, d)
    n_t = fw_len - 1
    head_rows = lambda a: jnp.concatenate([jnp.zeros((SUBLANES - n_t, ffp), F32), a[-n_t:]], axis=0)
    hd = lambda a: a.reshape(bd, n_heads, -1)
    dec = (page_table, hd(q_small[:bd]), hd(k_small[:bd]), hd(v_small[:bd]), cache_k[0], cache_v[0],
           mb, mbl, b0, lamv, subln_g[0])
    y_p, tail_a, tail_g, o_host = prompt_ffn(h1_p, norm2_g[0], final_norm_g, wab_b, wd_b, cwa, cwg, cba, cbg,
                                             head_rows(up_meta[0]), head_rows(up_meta[1]), s, FF_ROWS, tf,
                                             dec, FF_DEC_PAGES)

    n_host = o_host.shape[0]
    o_dec = o_host
    if n_host < bd:
        o_rest = decode_attention(*(a[n_host:] for a in dec[:4]), *dec[4:], DEC_PAGES)
        o_dec = jnp.concatenate([o_host, o_rest], axis=0)
    oc_s = jnp.concatenate([o_dec.reshape(bd, d_attn), c_small[:bd]], axis=1)
    hs_delta = mm_bf16(oc_s, w_o_b[None], 0, W_COLS)
    h1_s, xn2_s = add_rmsnorm(x_small[:bd], hs_delta, norm2_g[0])
    up_a_s = mm_bf16(xn2_s, wab_b, 0, W_COLS)
    up_g_s = mm_bf16(xn2_s, wab_b, 1, W_COLS)
    st = jnp.transpose(state_ffn_conv[0], (1, 0, 2))
    st_pad = lambda a: pad_c(a.reshape(-1, d_ff)).reshape(fw_len - 1, bd, ffp)
    hid = small_gate(up_a_s, up_g_s, st_pad(st[:, :, :d_ff]), st_pad(st[:, :, d_ff:]), cwa, cwg, cba, cbg)
    down = mm_bf16(hid, wd_b[None], 0, WD_COLS)
    y_sample = residual_norm(h1_s, down, final_norm_g)

    k_prompt = kv_rows_with_meta(kvu, 0, k_small[bd:], n_heads, tm)[None]
    v_prompt = kv_rows_with_meta(kvu, 1, v_small[bd:], n_heads, tm)[None]
    conv_prompt = kvu[2:3, :, s - (cw_len - 1):, :]
    bps = s // FF_ROWS
    last_blk = lambda a: a[bps - 1::bps, SUBLANES - n_t:, :d_ff]
    ffn_prompt = jnp.concatenate([last_blk(tail_a), last_blk(tail_g)], axis=-1)[None]
    k_sample = k_small[:bd].reshape(1, bd, 1, n_heads, e)
    v_sample = v_small[:bd].reshape(1, bd, 1, n_heads, dv)
    conv_sample = jnp.concatenate([state_conv[0][:, 1:], u_small[:bd, None, :]], axis=1)[None]
    up_s = jnp.concatenate([up_a_s[:, :d_ff], up_g_s[:, :d_ff]], axis=-1)
    ffn_sample = jnp.concatenate([state_ffn_conv[0][:, 1:], up_s[:, None, :]], axis=1)[None]
    return (y_p.reshape(b, s, d), y_sample.reshape(bd, 1, d), k_prompt, v_prompt, conv_prompt, ffn_prompt,
            k_sample, v_sample, conv_sample, ffn_sample)
```

```python
import functools
import math

import jax
import jax.numpy as jnp
from jax import lax
from jax.experimental import pallas as pl
from jax.experimental.pallas import tpu as pltpu

F32 = jnp.float32
BF16 = jnp.bfloat16

EPS = 1e-6
MAX_DISTANCE = 128
LAMBDA_INIT_0 = 0.8 - 0.6 * math.exp(-0.3 * 0)
NEG = -1e30
LOG2E = math.log2(math.e)

LANES = 128
SUBLANES = 8
VMEM_LIMIT = 56 * 1024 * 1024

ATTN_T = 512
ATTN_HEADS = 2
ROW_TILE = 512
IN_ROWS = 1024
FF_ROWS = 512
FF_DEC_PAGES = 8
W_COLS = 512
WD_COLS = 512
KV_ROWS = 1024
FF_TILE = 512
FF_SUB = 2
CONV_ROWS = 128
DEC_PAGES = 16


def _cparams(*sem):
    return pltpu.CompilerParams(dimension_semantics=sem, vmem_limit_bytes=VMEM_LIMIT)


def _rms(x, g):
    return x * lax.rsqrt(jnp.mean(x * x, axis=-1, keepdims=True) + EPS) * g


def _layernorm(y, g, b):
    mu = jnp.mean(y, axis=-1, keepdims=True)
    d = y - mu
    var = jnp.mean(d * d, axis=-1, keepdims=True)
    return d * lax.rsqrt(var + EPS) * g + b


def _silu(x):
    return x * jax.nn.sigmoid(x)


def _lambda(lamv_ref):
    a = jnp.sum(lamv_ref[0:1, :] * lamv_ref[1:2, :], axis=-1, keepdims=True)
    b = jnp.sum(lamv_ref[2:3, :] * lamv_ref[3:4, :], axis=-1, keepdims=True)
    return jnp.exp(a) - jnp.exp(b) + LAMBDA_INIT_0


def _t5_bucket(rel, n_buckets):
    n = jnp.maximum(rel, 0)
    max_exact = n_buckets // 2
    nf = jnp.maximum(n, 1).astype(F32)
    large = max_exact + (jnp.log(nf / max_exact) / math.log(MAX_DISTANCE / max_exact)
                         * (n_buckets - max_exact)).astype(jnp.int32)
    large = jnp.minimum(large, n_buckets - 1)
    return jnp.where(n < max_exact, n, large)


def _bias_tile(rel_bias, rel):
    nb = rel_bias.shape[0]
    onehot = (_t5_bucket(rel, nb)[..., None] == jnp.arange(nb, dtype=jnp.int32)).astype(F32)
    table = (rel_bias - rel_bias[nb - 1]).astype(F32) * LOG2E
    b = jnp.einsum("rcn,nh->hrc", onehot, table, precision=lax.Precision.HIGHEST)
    return jnp.where((rel >= 0)[None], b, NEG)


def _toeplitz_kernel(u0_ref, u1_ref, d0_ref, d1_ref):
    t = d0_ref.shape[0]
    for u_ref, d_ref in ((u0_ref, d0_ref), (u1_ref, d1_ref)):
        x = jnp.broadcast_to(u_ref[...], (t, 2 * t))
        d_ref[...] = pltpu.roll(x, 0, 1, stride=1, stride_axis=0)[:, :t]


def toeplitz_tiles(w0, w1, t):
    h = w0.shape[0]

    def wrap(w):
        return jnp.concatenate([w[:, :t][:, ::-1], jnp.zeros((h, 1), F32), w[:, t:][:, ::-1]],
                               axis=1).reshape(h, 1, 2 * t)

    vec = pl.BlockSpec((None, 1, 2 * t), lambda hh: (hh, 0, 0))
    tile = pl.BlockSpec((None, t, t), lambda hh: (hh, 0, 0))
    shape = jax.ShapeDtypeStruct((h, t, t), F32)
    return pl.pallas_call(
        _toeplitz_kernel, grid=(h,), in_specs=[vec, vec], out_specs=[tile, tile],
        out_shape=[shape, shape], name="toeplitz_tiles",
    )(wrap(w0), wrap(w1))


def _rmsnorm_cast_kernel(x_ref, g_ref, o_ref):
    o_ref[...] = _rms(x_ref[...], g_ref[...]).astype(o_ref.dtype)


def rmsnorm_cast(x, g):
    return pl.pallas_call(
        _rmsnorm_cast_kernel,
        out_shape=jax.ShapeDtypeStruct(x.shape, BF16),
        name="rmsnorm_cast",
    )(x, g.reshape(1, -1))


def _mm_cast_kernel(x_ref, w_ref, o_ref, wb_ref):
    wb = w_ref[...].astype(BF16)
    wb_ref[...] = wb
    o_ref[...] = jnp.dot(x_ref[...], wb, preferred_element_type=F32)


def mm_cast(x, w, tn):
    m, k = x.shape
    n = w.shape[1]
    return pl.pallas_call(
        _mm_cast_kernel,
        grid=(n // tn,),
        in_specs=[pl.BlockSpec((m, k), lambda j: (0, 0)),
                  pl.BlockSpec((k, tn), lambda j: (0, j))],
        out_specs=[pl.BlockSpec((m, tn), lambda j: (0, j)), pl.BlockSpec((k, tn), lambda j: (0, j))],
        out_shape=[jax.ShapeDtypeStruct((m, n), F32), jax.ShapeDtypeStruct((k, n), BF16)],
        compiler_params=_cparams("arbitrary"),
        name="mm_cast",
    )(x, w)


def _up_cast_kernel(x_ref, w_ref, o_ref, wb_ref, *, rem):
    hf = pl.program_id(0)
    j = pl.program_id(1)
    last = pl.num_programs(1) - 1
    d, tn = w_ref.shape

    def emit(w):
        wb = w.astype(BF16)
        wb_ref[...] = wb
        o_ref[...] = jnp.dot(x_ref[...], wb, preferred_element_type=F32)

    @pl.when(j < last)
    def _():
        emit(w_ref[...])

    @pl.when((j == last) & (hf == 0))
    def _():
        lane = lax.broadcasted_iota(jnp.int32, (d, tn), 1)
        emit(jnp.where(lane < rem, w_ref[...], 0.0))

    @pl.when((j == last) & (hf == 1))
    def _():
        emit(jnp.concatenate([w_ref[:, tn - rem:], jnp.zeros((d, tn - rem), F32)], axis=1))


def up_cast(x, w_up, d_ff, ffp, tn):
    m, d = x.shape
    n_blk = ffp // tn
    rem = d_ff - (n_blk - 1) * tn
    assert 0 < rem <= tn and rem % LANES == 0 and d_ff >= tn
    el = pl.Element
    return pl.pallas_call(
        functools.partial(_up_cast_kernel, rem=rem),
        grid=(2, n_blk),
        in_specs=[pl.BlockSpec((m, d), lambda hf, j: (0, 0)),
                  pl.BlockSpec((el(d), el(tn)),
                               lambda hf, j: (0, pl.multiple_of(
                                   jnp.minimum(hf * d_ff + j * tn, 2 * d_ff - tn), LANES)))],
        out_specs=[pl.BlockSpec((None, m, tn), lambda hf, j: (hf, 0, j)),
                   pl.BlockSpec((None, d, tn), lambda hf, j: (hf, 0, j))],
        out_shape=[jax.ShapeDtypeStruct((2, m, ffp), F32), jax.ShapeDtypeStruct((2, d, ffp), BF16)],
        compiler_params=_cparams("arbitrary", "arbitrary"),
        name="up_cast",
    )(x, w_up)


def _cast_pad_kernel(w_ref, wb_ref):
    k = w_ref.shape[0]
    wb_ref[0:k, :] = w_ref[...].astype(BF16)
    if wb_ref.shape[0] > k:
        wb_ref[k:, :] = jnp.zeros((wb_ref.shape[0] - k, wb_ref.shape[1]), BF16)


def cast_pad(w, k_pad, tn):
    k, n = w.shape
    return pl.pallas_call(
        _cast_pad_kernel,
        grid=(n // tn,),
        in_specs=[pl.BlockSpec((k, tn), lambda j: (0, j))],
        out_specs=pl.BlockSpec((k_pad, tn), lambda j: (0, j)),
        out_shape=jax.ShapeDtypeStruct((k_pad, n), BF16),
        compiler_params=_cparams("arbitrary"),
        name="cast_pad",
    )(w)


def _mm_bf16_kernel(x_ref, w_ref, o_ref):
    o_ref[...] = jnp.dot(x_ref[...], w_ref[...], preferred_element_type=F32)


def mm_bf16(x, w3, slab, tn):
    m, k = x.shape
    n = w3.shape[-1]
    return pl.pallas_call(
        _mm_bf16_kernel,
        grid=(n // tn,),
        in_specs=[pl.BlockSpec((m, k), lambda j: (0, 0)),
                  pl.BlockSpec((None, k, tn), lambda j: (slab, 0, j))],
        out_specs=pl.BlockSpec((m, tn), lambda j: (0, j)),
        out_shape=jax.ShapeDtypeStruct((m, n), F32),
        compiler_params=_cparams("arbitrary"),
        name="mm_bf16",
    )(x, w3)


def _add_rmsnorm_kernel(x_ref, d_ref, g_ref, h_ref, xn_ref):
    h = x_ref[...] + d_ref[...]
    h_ref[...] = h
    xn_ref[...] = _rms(h, g_ref[...]).astype(xn_ref.dtype)


def add_rmsnorm(x, delta, g):
    return pl.pallas_call(
        _add_rmsnorm_kernel,
        out_shape=[jax.ShapeDtypeStruct(x.shape, F32), jax.ShapeDtypeStruct(x.shape, BF16)],
        name="add_rmsnorm",
    )(x, delta, g.reshape(1, -1))


def _inproj_kernel(x_ref, g_ref, w_ref, wg_ref, ob_ref, of_ref, xn_ref, *, q_scale):
    j = pl.program_id(1)

    @pl.when(j == 0)
    def _():
        xn_ref[...] = _rms(x_ref[...], g_ref[...]).astype(BF16)

    z = jnp.dot(xn_ref[...], w_ref[...], preferred_element_type=F32)

    @pl.when(j == 0)
    def _():
        ob_ref[...] = (z * q_scale).astype(BF16)

    @pl.when((j == 1) | (j == 2))
    def _():
        of_ref[...] = z
        ob_ref[...] = z.astype(BF16)

    @pl.when(j == 3)
    def _():
        gate = jnp.dot(xn_ref[...], wg_ref[...], preferred_element_type=F32)
        of_ref[...] = z * jax.nn.sigmoid(gate)


def prompt_inproj(x, g, w_in, d_attn, q_scale, tm):
    m, d = x.shape
    c = d_attn
    return pl.pallas_call(
        functools.partial(_inproj_kernel, q_scale=q_scale),
        grid=(m // tm, 4),
        in_specs=[pl.BlockSpec((tm, d), lambda i, j: (i, 0)),
                  pl.BlockSpec((1, d), lambda i, j: (0, 0)),
                  pl.BlockSpec((d, c), lambda i, j: (0, j)),
                  pl.BlockSpec((d, c), lambda i, j: (0, 4), pipeline_mode=pl.Buffered(1))],
        out_specs=[pl.BlockSpec((None, tm, c), lambda i, j: (jnp.minimum(j, 2), i, 0)),
                   pl.BlockSpec((None, tm, c), lambda i, j: (jnp.maximum(j - 1, 0), i, 0))],
        out_shape=[jax.ShapeDtypeStruct((3, m, c), BF16), jax.ShapeDtypeStruct((3, m, c), F32)],
        scratch_shapes=[pltpu.VMEM((tm, d), BF16)],
        compiler_params=_cparams("arbitrary", "arbitrary"),
        name="prompt_inproj",
    )(x, g.reshape(1, -1), w_in, w_in)


def _kv_rows_kernel(x_ref, meta_ref, o_ref, *, n_meta, n_heads):
    i = pl.program_id(1)
    n_blk = pl.num_programs(1)
    tm = o_ref.shape[0]
    heads = lambda a: a.reshape(a.shape[0], n_heads, a.shape[1] // n_heads)

    @pl.when(i == 0)
    def _():
        o_ref[0:n_meta] = heads(meta_ref[...])
        o_ref[n_meta:] = heads(x_ref[0, 0, 0:tm - n_meta, :])

    @pl.when((i > 0) & (i < n_blk - 1))
    def _():
        o_ref[...] = heads(x_ref[0, 0])

    @pl.when(i == n_blk - 1)
    def _():
        o_ref[0:n_meta] = heads(x_ref[0, 0, tm - n_meta:tm, :])
        o_ref[n_meta:] = jnp.zeros((tm - n_meta,) + o_ref.shape[1:], o_ref.dtype)


def kv_rows_with_meta(kvu, slab, meta_rows, n_heads, tm):
    _, b, s, c = kvu.shape
    n_meta = meta_rows.shape[0]
    n_blk = s // tm + 1
    assert n_meta % SUBLANES == 0 and n_meta < tm
    el = pl.Element
    return pl.pallas_call(
        functools.partial(_kv_rows_kernel, n_meta=n_meta, n_heads=n_heads),
        grid=(b, n_blk),
        in_specs=[pl.BlockSpec((el(1), el(1), el(tm), el(c)),
                               lambda bi, i: (slab, bi, pl.multiple_of(
                                   jnp.clip(i * tm - n_meta, 0, s - tm), SUBLANES), 0)),
                  pl.BlockSpec((n_meta, c), lambda bi, i: (0, 0))],
        out_specs=pl.BlockSpec((None, tm, n_heads, c // n_heads), lambda bi, i: (bi, i, 0, 0)),
        out_shape=jax.ShapeDtypeStruct((b, n_meta + s, n_heads, c // n_heads), F32),
        compiler_params=_cparams("arbitrary", "arbitrary"),
        name="kv_rows_with_meta",
    )(kvu, meta_rows)


def _stack_q(q, dk):
    lane = lax.broadcasted_iota(jnp.int32, q.shape, 1)
    zero = jnp.zeros_like(q)
    return jnp.concatenate([jnp.where(lane < dk, q, zero), jnp.where(lane >= dk, q, zero)], axis=0)


def _scores(qq, kblk):
    return lax.dot_general(qq, kblk, (((1,), (1,)), ((), ())), preferred_element_type=F32)


def _add_tile(s, tile):
    t, c = tile.shape
    return (s.reshape(2, t, c) + tile[None]).reshape(2 * t, c)


def _with_ones(v):
    return jnp.concatenate([v, jnp.ones_like(v)], axis=-1)


def _softmax_update(s, vx, carry):
    m, acc = carry
    m_new = jnp.maximum(m, jnp.max(s, axis=-1, keepdims=True))
    alpha = jnp.exp2(m - m_new)
    p = jnp.exp2(s - m_new).astype(BF16)
    return m_new, alpha * acc + jnp.dot(p, vx, preferred_element_type=F32)


def _combine_heads_out(acc, lam, g, t):
    dv = acc.shape[-1] // 2
    o = acc[:t, :dv] / acc[:t, dv:] - lam * (acc[t:, :dv] / acc[t:, dv:])
    return _rms(o, g) * (1.0 - LAMBDA_INIT_0)


def _prompt_attn_kernel(q_ref, k_ref, v_ref, km_ref, vm_ref, d0_ref, d1_ref, dm_ref,
                        lamv_ref, g_ref, o_ref, vx_ref, qa_ref, qb_ref, *, t, dk, n_grp):
    qi = pl.program_id(2)
    r = 2 * t
    e = 2 * dk
    dv = v_ref.shape[-1] // n_grp
    heads = range(n_grp)
    hs = lambda hh, w: slice(hh * w, (hh + 1) * w)

    @pl.when(qi == 0)
    def _():
        for hh in heads:
            vx_ref[hh, :, :dv] = v_ref[:, hs(hh, dv)]
            vx_ref[hh, :, dv:] = jnp.ones((v_ref.shape[0], dv), BF16)

    qq = [_stack_q(q_ref[:, hs(hh, e)], dk) for hh in heads]
    rows = lambda j: pl.ds(pl.multiple_of(j * t, t), t)
    kblk = lambda hh, j: k_ref[rows(j), hs(hh, e)]
    vxblk = lambda hh, j: vx_ref[hh, rows(j), :]

    n_far = jnp.maximum(qi - 1, 0)

    for hh in heads:
        qa_ref[hh] = qq[hh]
        qb_ref[hh] = qq[hh]

    def far_body(j, carry):
        out = []
        for hh in heads:
            m, acc = carry[hh]
            kb = kblk(hh, j)
            m_new = jnp.maximum(m, jnp.max(_scores(qa_ref[hh], kb), axis=-1, keepdims=True))
            alpha = jnp.exp2(m - m_new)
            p = jnp.exp2(_scores(qb_ref[hh], kb) - m_new).astype(BF16)
            out.append((m_new, alpha * acc + jnp.dot(p, vxblk(hh, j), preferred_element_type=F32)))
        return tuple(out)

    init = tuple((jnp.full((r, 1), NEG, F32), jnp.zeros((r, 2 * dv), F32)) for hh in heads)
    carry = lax.fori_loop(0, n_far, far_body, init)

    lam = _lambda(lamv_ref)
    for hh in heads:
        m, acc = carry[hh]
        s_prev = _scores(qq[hh], kblk(hh, n_far))
        s_sub = _add_tile(s_prev, jnp.where(qi >= 1, d1_ref[hh], NEG))
        s_diag = _add_tile(_scores(qq[hh], kblk(hh, qi)), d0_ref[hh])
        s_meta = _add_tile(_scores(qq[hh], km_ref[:, hs(hh, e)]), jnp.where(qi == 0, dm_ref[hh], 0.0))
        c = _softmax_update(s_meta, _with_ones(vm_ref[:, hs(hh, dv)]), (m, acc))
        c = _softmax_update(s_diag, vxblk(hh, qi), c)
        m, acc = _softmax_update(s_sub, vxblk(hh, n_far), c)
        o_ref[:, hs(hh, dv)] = _combine_heads_out(acc, lam, g_ref[...], t).astype(o_ref.dtype)


def prompt_attention(qkv, kmb, vmb, d0, d1, dm, lamv, subln_g, n_heads, t, n_grp):
    _, b, s, width = qkv.shape
    e = dv = width // n_heads
    nm = kmb.shape[0]
    ge, gv = n_grp * e, n_grp * dv
    tile = lambda c: pl.BlockSpec((n_grp, t, c), lambda bi, h, qi: (h, 0, 0))
    return pl.pallas_call(
        functools.partial(_prompt_attn_kernel, t=t, dk=e // 2, n_grp=n_grp),
        grid=(b, n_heads // n_grp, s // t),
        in_specs=[
            pl.BlockSpec((None, None, t, ge), lambda bi, h, qi: (0, bi, qi, h)),
            pl.BlockSpec((None, None, s, ge), lambda bi, h, qi: (1, bi, 0, h)),
            pl.BlockSpec((None, None, s, gv), lambda bi, h, qi: (2, bi, 0, h)),
            pl.BlockSpec((nm, ge), lambda bi, h, qi: (0, h)),
            pl.BlockSpec((nm, gv), lambda bi, h, qi: (0, h)),
            tile(t), tile(t), tile(nm),
            pl.BlockSpec((4, e // 2), lambda bi, h, qi: (0, 0)),
            pl.BlockSpec((1, dv), lambda bi, h, qi: (0, 0)),
        ],
        out_specs=pl.BlockSpec((None, t, gv), lambda bi, h, qi: (bi, qi, h)),
        out_shape=jax.ShapeDtypeStruct((b, s, n_heads * dv), BF16),
        scratch_shapes=[pltpu.VMEM((n_grp, s, 2 * dv), BF16),
                        pltpu.VMEM((n_grp, 2 * t, e), BF16), pltpu.VMEM((n_grp, 2 * t, e), BF16)],
        compiler_params=_cparams("arbitrary", "arbitrary", "arbitrary"),
        name="prompt_attention",
    )(qkv, qkv, qkv, kmb, vmb, d0, d1, dm, lamv, subln_g.reshape(1, -1))


def _meta_attn_kernel(q_ref, k_ref, v_ref, d_ref, lamv_ref, g_ref, o_ref, *, n_heads, dk):
    t = q_ref.shape[0]
    e = 2 * dk
    dv = v_ref.shape[-1] // n_heads
    lam = _lambda(lamv_ref)
    for h in range(n_heads):
        qq = _stack_q(q_ref[:, h * e:(h + 1) * e], dk)
        kblk = k_ref[:, h * e:(h + 1) * e].astype(BF16)
        vx = _with_ones(v_ref[:, h * dv:(h + 1) * dv].astype(BF16))
        carry = (jnp.full((2 * t, 1), NEG, F32), jnp.zeros((2 * t, 2 * dv), F32))
        m, acc = _softmax_update(_add_tile(_scores(qq, kblk), d_ref[h]), vx, carry)
        o_ref[:, h * dv:(h + 1) * dv] = _combine_heads_out(acc, lam, g_ref[...], t).astype(o_ref.dtype)


def meta_attention(q, k, v, dmeta, lamv, subln_g, n_heads):
    t = q.shape[0]
    dv = v.shape[-1] // n_heads
    return pl.pallas_call(
        functools.partial(_meta_attn_kernel, n_heads=n_heads, dk=k.shape[-1] // n_heads // 2),
        out_shape=jax.ShapeDtypeStruct((t, n_heads * dv), BF16),
        name="meta_attention",
    )(q, k, v, dmeta, lamv, subln_g.reshape(1, -1))


def _decode_init(q_ref, kn_ref, vn_ref, b0_ref, qq_ref, m_ref, l_ref, acc_ref, dk):
    qf = q_ref[...].astype(F32)
    lane = lax.broadcasted_iota(jnp.int32, qf.shape, 1)
    qq = jnp.concatenate([jnp.where(lane < dk, qf, 0.0), jnp.where(lane >= dk, qf, 0.0)], axis=0)
    qq_ref[...] = qq.astype(BF16)
    kn = jnp.concatenate([kn_ref[...], kn_ref[...]], axis=0)
    s_new = jnp.sum(qq * kn, axis=-1, keepdims=True) + b0_ref[...]
    m_ref[...] = s_new
    l_ref[...] = jnp.ones_like(s_new)
    acc_ref[...] = jnp.concatenate([vn_ref[...], vn_ref[...]], axis=0)


def _decode_pages(*args):
    for _ in _decode_pages_phases(*args):
        pass


def _decode_pages_phases(is_last_group, mb_ref, mbl_ref, k_refs, v_refs, qq_ref, m_ref, l_ref, acc_ref,
                         n_heads):
    qq = qq_ref[...]
    e = qq.shape[-1]
    n_p = len(k_refs)
    s_pages = []
    for p in range(n_p):
        bias = mb_ref[...]
        if p == n_p - 1:
            bias = jnp.where(is_last_group, mbl_ref[...], bias)
        rows = k_refs[p].shape[0] * n_heads
        s_pages.append(_scores(qq, k_refs[p][...].reshape(rows, e).astype(BF16)) + bias)
    yield
    m_old = m_ref[...]
    m_loc = s_pages[0]
    for s in s_pages[1:]:
        m_loc = jnp.maximum(m_loc, s)
    m_new = jnp.maximum(m_old, jnp.max(m_loc, axis=-1, keepdims=True))
    alpha = jnp.exp2(m_old - m_new)
    weights = [jnp.exp2(s - m_new) for s in s_pages]
    yield
    l_sum = None
    pv = None
    for p in range(n_p):
        w = weights[p]
        rows = v_refs[p].shape[0] * n_heads
        vblk = v_refs[p][...].reshape(rows, v_refs[p].shape[-1]).astype(BF16)
        d = jnp.dot(w.astype(BF16), vblk, preferred_element_type=F32)
        l_sum = w if l_sum is None else l_sum + w
        pv = d if pv is None else pv + d
    l_ref[...] = alpha * l_ref[...] + jnp.sum(l_sum, axis=-1, keepdims=True)
    acc_ref[...] = alpha * acc_ref[...] + pv
    m_ref[...] = m_new


def _decode_final(lamv_ref, g_ref, o_ref, l_ref, acc_ref, n_heads):
    acc, l = acc_ref[...], l_ref[...]
    o = acc[:n_heads] / l[:n_heads] - _lambda(lamv_ref) * (acc[n_heads:] / l[n_heads:])
    o_ref[...] = (_rms(o, g_ref[...]) * (1.0 - LAMBDA_INIT_0)).astype(o_ref.dtype)


def _decode_attn_kernel(pt_ref, q_ref, kn_ref, vn_ref, mb_ref, mbl_ref, b0_ref, lamv_ref, g_ref, *rest,
                        n_pages_step, n_heads, dk):
    k_refs = rest[:n_pages_step]
    v_refs = rest[n_pages_step:2 * n_pages_step]
    o_ref = rest[2 * n_pages_step]
    state = rest[2 * n_pages_step + 1:]
    del pt_ref
    g = pl.program_id(1)
    last = g == pl.num_programs(1) - 1

    @pl.when(g == 0)
    def _():
        _decode_init(q_ref, kn_ref, vn_ref, b0_ref, *state, dk)

    _decode_pages(last, mb_ref, mbl_ref, k_refs, v_refs, *state, n_heads)

    @pl.when(last)
    def _():
        _decode_final(lamv_ref, g_ref, o_ref, state[2], state[3], n_heads)


def decode_attention(page_table, q, k_new, v_new, cache_k, cache_v, mb, mbl, b0, lamv, subln_g,
                     pages_per_step):
    bd, n_pages = page_table.shape
    _, page, n_heads, e = cache_k.shape
    dv = cache_v.shape[-1]
    p_step = pages_per_step
    r = 2 * n_heads
    seq_blk = lambda w: pl.BlockSpec((None, n_heads, w), lambda b, g, pt: (b, 0, 0))
    const2 = lambda b, g, pt: (0, 0)

    def page_spec(p, w):
        return pl.BlockSpec((None, page, n_heads, w), lambda b, g, pt: (pt[b, g * p_step + p], 0, 0, 0))

    grid_spec = pltpu.PrefetchScalarGridSpec(
        num_scalar_prefetch=1,
        grid=(bd, n_pages // p_step),
        in_specs=[seq_blk(e), seq_blk(e), seq_blk(dv),
                  pl.BlockSpec((r, page * n_heads), const2),
                  pl.BlockSpec((r, page * n_heads), const2),
                  pl.BlockSpec((r, 1), const2),
                  pl.BlockSpec((4, e // 2), const2),
                  pl.BlockSpec((1, dv), const2)]
                 + [page_spec(p, e) for p in range(p_step)]
                 + [page_spec(p, dv) for p in range(p_step)],
        out_specs=seq_blk(dv),
        scratch_shapes=[pltpu.VMEM((r, e), BF16), pltpu.VMEM((r, 1), F32),
                        pltpu.VMEM((r, 1), F32), pltpu.VMEM((r, dv), F32)],
    )
    return pl.pallas_call(
        functools.partial(_decode_attn_kernel, n_pages_step=p_step, n_heads=n_heads, dk=e // 2),
        grid_spec=grid_spec,
        out_shape=jax.ShapeDtypeStruct((bd, n_heads, dv), BF16),
        compiler_params=_cparams("arbitrary", "arbitrary"),
        name="decode_attention",
    )(page_table, q, k_new, v_new, mb, mbl, b0, lamv, subln_g.reshape(1, -1),
      *([cache_k] * p_step), *([cache_v] * p_step))


def _conv_rows(src_ref, base, n_rows, cw_ref, conv_w):
    acc = src_ref[pl.ds(base, n_rows), :] * cw_ref[0:1, :]
    for w in range(1, conv_w):
        acc = acc + src_ref[pl.ds(base + w, n_rows), :] * cw_ref[w:w + 1, :]
    return acc


def _conv_tiles(win, cwb_ref, lanes, conv_w, first, n_out):
    y = None
    for k in range(SUBLANES):
        taps = [w for w in range(conv_w) if (first + w) % SUBLANES == k]
        if not taps:
            continue
        n_p = n_out if k == 0 else n_out + 1
        part = None
        for w in taps:
            j = (first + w) // SUBLANES
            term = win[j:j + n_p] * cwb_ref[w, :, lanes][None]
            part = term if part is None else part + term
        part = part.reshape(n_p * SUBLANES, part.shape[-1])
        part = part if k == 0 else part[k:k + n_out * SUBLANES]
        y = part if y is None else y + part
    return y


def _conv_merge_kernel(u_ref, uh_ref, head_ref, cwb_ref, cb_ref, lg_ref, lb_ref, o_ref, x_ref, w_ref,
                       y_ref, s_ref, yc_ref, c_ref, *, conv_w, halo, rows):
    i = pl.program_id(1)
    tm, c = u_ref.shape
    da = o_ref.shape[-1]
    s_ref[0:halo, :] = jnp.where(i == 0, head_ref[...], uh_ref[...])
    s_ref[halo:, :] = u_ref[...]
    first = halo - (conv_w - 1)
    n_win = (rows + halo) // SUBLANES

    y_ref[...] = x_ref[...] + jnp.dot(o_ref[...], w_ref[0:da, :], preferred_element_type=F32)
    for ci in range(tm // rows):
        r0 = ci * rows
        for lb in range(c // LANES):
            lanes = slice(lb * LANES, (lb + 1) * LANES)
            win = s_ref[r0:r0 + rows + halo, lanes].reshape(n_win, SUBLANES, LANES)
            yc_ref[r0:r0 + rows, lanes] = _conv_tiles(win, cwb_ref, lanes, conv_w, first, rows // SUBLANES)
        y = yc_ref[r0:r0 + rows, :] + cb_ref[...]
        c_ref[r0:r0 + rows, :] = _silu(_layernorm(y, lg_ref[...], lb_ref[...])).astype(c_ref.dtype)
    y_ref[...] += jnp.dot(c_ref[...], w_ref[da:, :], preferred_element_type=F32)


def conv_merge(kvu, head, conv_w, conv_b, ln_g, ln_b, o_n, x, w_o, tm):
    _, b, s, c = kvu.shape
    d = x.shape[-1]
    halo = head.shape[0]
    w = conv_w.shape[0]
    per = tm // halo
    vec = lambda a: a.reshape(1, -1)
    const = lambda bi, i: (0, 0)
    blk = lambda width: pl.BlockSpec((None, tm, width), lambda bi, i: (bi, i, 0))
    cwb = jnp.broadcast_to(conv_w[:, None, :], (w, SUBLANES, c))
    return pl.pallas_call(
        functools.partial(_conv_merge_kernel, conv_w=w, halo=halo, rows=CONV_ROWS),
        grid=(b, s // tm),
        in_specs=[pl.BlockSpec((None, None, tm, c), lambda bi, i: (2, bi, i, 0)),
                  pl.BlockSpec((None, None, halo, c), lambda bi, i: (2, bi, jnp.maximum(i * per - 1, 0), 0)),
                  pl.BlockSpec((halo, c), const),
                  pl.BlockSpec((w, SUBLANES, c), lambda bi, i: (0, 0, 0)),
                  pl.BlockSpec((1, c), const), pl.BlockSpec((1, c), const), pl.BlockSpec((1, c), const),
                  blk(c), blk(d),
                  pl.BlockSpec((d, d), const, pipeline_mode=pl.Buffered(1))],
        out_specs=blk(d),
        out_shape=jax.ShapeDtypeStruct((b, s, d), F32),
        scratch_shapes=[pltpu.VMEM((tm + halo, c), F32), pltpu.VMEM((tm, c), F32), pltpu.VMEM((tm, c), BF16)],
        compiler_params=_cparams("arbitrary", "arbitrary"),
        name="conv_merge",
    )(kvu, kvu, head, cwb, vec(conv_b), vec(ln_g), vec(ln_b), o_n, x, w_o)


def _small_conv_kernel(za_ref, zg_ref, st_ref, cw_ref, cb_ref, lg_ref, lb_ref, u_ref, c_ref, s_ref,
                       *, conv_w, n_sample):
    n_meta = za_ref.shape[0] - n_sample
    u = za_ref[...] * jax.nn.sigmoid(zg_ref[...])
    u_ref[...] = u
    y = u[:n_sample] * cw_ref[conv_w - 1:conv_w, :]
    for w in range(conv_w - 1):
        y = y + st_ref[w] * cw_ref[w:w + 1, :]
    y_s = y + cb_ref[...]
    hist = s_ref.shape[0] - n_meta
    s_ref[0:hist, :] = jnp.zeros((hist, s_ref.shape[1]), F32)
    s_ref[hist:, :] = u[n_sample:]
    y_m = _conv_rows(s_ref, hist - (conv_w - 1), n_meta, cw_ref, conv_w) + cb_ref[...]
    yy = jnp.concatenate([y_s, y_m], axis=0)
    c_ref[...] = _silu(_layernorm(yy, lg_ref[...], lb_ref[...])).astype(c_ref.dtype)


def small_conv_module(za, zg, state_t, conv_w, conv_b, ln_g, ln_b, n_sample):
    m, c = za.shape
    w = conv_w.shape[0]
    n_meta = m - n_sample
    hist = SUBLANES * ((w - 1 + SUBLANES - 1) // SUBLANES)
    vec = lambda a: a.reshape(1, -1)
    return pl.pallas_call(
        functools.partial(_small_conv_kernel, conv_w=w, n_sample=n_sample),
        out_shape=[jax.ShapeDtypeStruct((m, c), F32), jax.ShapeDtypeStruct((m, c), BF16)],
        scratch_shapes=[pltpu.VMEM((hist + n_meta, c), F32)],
        name="small_conv_module",
    )(za, zg, state_t, conv_w, vec(conv_b), vec(ln_g), vec(ln_b))


def _gelu(x):
    return 0.5 * x * (1.0 + lax.erf(x * math.sqrt(0.5)))


def _ffn_kernel(pt_ref, x_ref, g2_ref, gf_ref, wa_ref, wg_ref, wd_ref, cwa_ref, cwg_ref, cba_ref, cbg_ref,
                ha_ref, hg_ref,
                q_ref, kn_ref, vn_ref, mb_ref, mbl_ref, b0_ref, lamv_ref, sg_ref, *rest,
                blocks_per_seq, conv_w, n_sub, dec_pages, dec_groups, n_host, n_heads, dk):
    k_refs = rest[:dec_pages]
    v_refs = rest[dec_pages:2 * dec_pages]
    y_ref, ta_ref, tg_ref, od_ref = rest[2 * dec_pages:2 * dec_pages + 4]
    xn_ref, ea_ref, eg_ref, ca_ref, cg_ref = rest[2 * dec_pages + 4:2 * dec_pages + 9]
    dec_state = rest[2 * dec_pages + 9:]
    del pt_ref
    i = pl.program_id(0)
    f = pl.program_id(1)
    n_f = pl.num_programs(1)
    tm = x_ref.shape[0]
    tf = wa_ref.shape[1]
    ts = tf // n_sub
    pad = SUBLANES
    step = i * n_f + f
    grp = step % dec_groups
    hosting = step < n_host * dec_groups

    @pl.when(f == 0)
    def _():
        x = x_ref[...]
        xn_ref[...] = _rms(x, g2_ref[...]).astype(BF16)
        y_ref[...] = x

    @pl.when((grp == 0) & hosting)
    def _():
        _decode_init(q_ref, kn_ref, vn_ref, b0_ref, *dec_state, dk)

    hosted = _decode_pages_phases(grp == dec_groups - 1, mb_ref, mbl_ref, k_refs, v_refs, *dec_state, n_heads)
    xn = xn_ref[...]
    first = (i % blocks_per_seq) == 0
    base = pad - (conv_w - 1)
    down = None
    for sc in range(n_sub):
        next(hosted, None)
        cols = slice(sc * ts, (sc + 1) * ts)
        up_a = jnp.dot(xn, wa_ref[:, cols], preferred_element_type=F32)
        up_g = jnp.dot(xn, wg_ref[:, cols], preferred_element_type=F32)
        ta_ref[:, cols] = up_a[tm - pad:]
        tg_ref[:, cols] = up_g[tm - pad:]
        ea_ref[sc, 0:pad, :] = jnp.where(first, ha_ref[:, cols], ca_ref[f, :, cols])
        eg_ref[sc, 0:pad, :] = jnp.where(first, hg_ref[:, cols], cg_ref[f, :, cols])
        ea_ref[sc, pad:, :] = up_a
        eg_ref[sc, pad:, :] = up_g
        ca_ref[f, :, cols] = up_a[tm - pad:]
        cg_ref[f, :, cols] = up_g[tm - pad:]
        conv_a = _conv_rows(ea_ref.at[sc], base, tm, cwa_ref.at[:, cols], conv_w) + cba_ref[:, cols]
        conv_g = _conv_rows(eg_ref.at[sc], base, tm, cwg_ref.at[:, cols], conv_w) + cbg_ref[:, cols]
        hidden = (_gelu(conv_g) * conv_a).astype(BF16)
        d = jnp.dot(hidden, wd_ref[cols, :], preferred_element_type=F32)
        down = d if down is None else down + d
    y_ref[...] += down
    for _ in hosted:
        pass

    @pl.when((grp == dec_groups - 1) & hosting)
    def _():
        _decode_final(lamv_ref, sg_ref, od_ref, dec_state[2], dec_state[3], n_heads)

    @pl.when(f == n_f - 1)
    def _():
        y_ref[...] = _rms(y_ref[...], gf_ref[...])


def prompt_ffn(x, norm2_g, final_g, wab, wd, cwa, cwg, cba, cbg, head_a, head_g, rows_per_seq,
               tm, tf, dec, dec_pages):
    m, d = x.shape
    ffp = wab.shape[-1]
    n_f = ffp // tf
    bps = rows_per_seq // tm
    w = cwa.shape[0]
    ts = tf // FF_SUB
    page_table, q, k_new, v_new, cache_k, cache_v, mb, mbl, b0, lamv, subln_g = dec
    _, page, n_heads, e = cache_k.shape
    dv = cache_v.shape[-1]
    groups = page_table.shape[1] // dec_pages
    n_host = min(page_table.shape[0], (m // tm) * n_f // groups)
    assert page_table.shape[1] % dec_pages == 0 and n_host >= 1
    r = 2 * n_heads
    row = lambda i, f, pt: (i, 0)
    const = lambda i, f, pt: (0, 0)
    col = lambda i, f, pt: (0, f)
    tail = pl.BlockSpec((None, SUBLANES, tf), lambda i, f, pt: (i, 0, f))
    seq = lambda i, f: jnp.minimum((i * n_f + f) // groups, n_host - 1)
    seq_blk = lambda wd_: pl.BlockSpec((None, n_heads, wd_), lambda i, f, pt: (seq(i, f), 0, 0))

    n_steps = (m // tm) * n_f
    pt_steps = page_table[:n_host].reshape(n_host * groups, dec_pages)
    pt_steps = jnp.concatenate([pt_steps, jnp.broadcast_to(pt_steps[-1:], (n_steps - n_host * groups, dec_pages))],
                               axis=0).reshape(n_steps * dec_pages)

    def page_spec(p, wd_):
        return pl.BlockSpec((None, page, n_heads, wd_),
                            lambda i, f, pt: (pt[(i * n_f + f) * dec_pages + p], 0, 0, 0))

    grid_spec = pltpu.PrefetchScalarGridSpec(
        num_scalar_prefetch=1,
        grid=(m // tm, n_f),
        in_specs=[pl.BlockSpec((tm, d), row),
                  pl.BlockSpec((1, d), const), pl.BlockSpec((1, d), const),
                  pl.BlockSpec((None, d, tf), lambda i, f, pt: (0, 0, f)),
                  pl.BlockSpec((None, d, tf), lambda i, f, pt: (1, 0, f)),
                  pl.BlockSpec((tf, d), lambda i, f, pt: (f, 0)),
                  pl.BlockSpec((w, tf), col), pl.BlockSpec((w, tf), col),
                  pl.BlockSpec((1, tf), col), pl.BlockSpec((1, tf), col),
                  pl.BlockSpec((SUBLANES, tf), col), pl.BlockSpec((SUBLANES, tf), col),
                  seq_blk(e), seq_blk(e), seq_blk(dv),
                  pl.BlockSpec((r, page * n_heads), const), pl.BlockSpec((r, page * n_heads), const),
                  pl.BlockSpec((r, 1), const), pl.BlockSpec((4, e // 2), const), pl.BlockSpec((1, dv), const)]
                 + [page_spec(p, e) for p in range(dec_pages)]
                 + [page_spec(p, dv) for p in range(dec_pages)],
        out_specs=[pl.BlockSpec((tm, d), row), tail, tail, seq_blk(dv)],
        scratch_shapes=[pltpu.VMEM((tm, d), BF16),
                        pltpu.VMEM((FF_SUB, tm + SUBLANES, ts), F32),
                        pltpu.VMEM((FF_SUB, tm + SUBLANES, ts), F32),
                        pltpu.VMEM((n_f, SUBLANES, tf), F32), pltpu.VMEM((n_f, SUBLANES, tf), F32),
                        pltpu.VMEM((r, e), BF16), pltpu.VMEM((r, 1), F32),
                        pltpu.VMEM((r, 1), F32), pltpu.VMEM((r, dv), F32)],
    )
    return pl.pallas_call(
        functools.partial(_ffn_kernel, blocks_per_seq=bps, conv_w=w, n_sub=FF_SUB, dec_pages=dec_pages,
                          dec_groups=groups, n_host=n_host, n_heads=n_heads, dk=e // 2),
        grid_spec=grid_spec,
        out_shape=[jax.ShapeDtypeStruct((m, d), F32),
                   jax.ShapeDtypeStruct((m // tm, SUBLANES, ffp), F32),
                   jax.ShapeDtypeStruct((m // tm, SUBLANES, ffp), F32),
                   jax.ShapeDtypeStruct((n_host, n_heads, dv), BF16)],
        compiler_params=_cparams("arbitrary", "arbitrary"),
        name="prompt_ffn",
    )(pt_steps, x, norm2_g.reshape(1, -1), final_g.reshape(1, -1), wab, wab, wd, cwa, cwg,
      cba.reshape(1, -1), cbg.reshape(1, -1), head_a, head_g,
      q[:n_host], k_new[:n_host], v_new[:n_host], mb, mbl, b0, lamv, subln_g.reshape(1, -1),
      *([cache_k] * dec_pages), *([cache_v] * dec_pages))


def _small_gate_kernel(ua_ref, ug_ref, sa_ref, sg_ref, cwa_ref, cwg_ref, cba_ref, cbg_ref, h_ref,
                       *, conv_w):
    def conv(u_ref, s_ref, cw_ref, cb_ref):
        y = u_ref[...] * cw_ref[conv_w - 1:conv_w, :] + cb_ref[...]
        for w in range(conv_w - 1):
            y = y + s_ref[w] * cw_ref[w:w + 1, :]
        return y

    a = conv(ua_ref, sa_ref, cwa_ref, cba_ref)
    g = conv(ug_ref, sg_ref, cwg_ref, cbg_ref)
    h_ref[...] = (_gelu(g) * a).astype(h_ref.dtype)


def small_gate(up_a, up_g, st_a, st_g, cwa, cwg, cba, cbg):
    return pl.pallas_call(
        functools.partial(_small_gate_kernel, conv_w=cwa.shape[0]),
        out_shape=jax.ShapeDtypeStruct(up_a.shape, BF16),
        name="small_gate",
    )(up_a, up_g, st_a, st_g, cwa, cwg, cba.reshape(1, -1), cbg.reshape(1, -1))


def _residual_norm_kernel(x_ref, d_ref, g_ref, y_ref):
    y_ref[...] = _rms(x_ref[...] + d_ref[...], g_ref[...])


def residual_norm(x, delta, g):
    return pl.pallas_call(
        _residual_norm_kernel,
        out_shape=jax.ShapeDtypeStruct(x.shape, F32),
        name="residual_norm",
    )(x, delta, g.reshape(1, -1))


def kernel(x_prompt, x_sample, cache_k, cache_v, page_table, state_conv, state_ffn_conv, meta_tokens,
           rel_bias, norm1_g, w_in, lambda_q1, lambda_k1, lambda_q2, lambda_k2, subln_g, conv_w, conv_b,
           conv_ln_g, conv_ln_b, w_o, norm2_g, ffn_w_up, ffn_conv_w, ffn_conv_b, ffn_w_down,
           final_norm_g):
    depth = w_in.shape[0]
    assert depth == 1, "single-layer step only"
    b, s, d = x_prompt.shape
    bd, t_dec = x_sample.shape[:2]
    assert t_dec == 1
    n_meta = meta_tokens.shape[0]
    n_pool, page, n_heads, e = cache_k.shape[1:]
    dv = cache_v.shape[-1]
    dk = e // 2
    d_attn = n_heads * dv
    d_conv = conv_w.shape[-1]
    cw_len = conv_w.shape[1]
    fw_len = ffn_conv_w.shape[1]
    d_ff = ffn_w_down.shape[1]
    n_buckets = rel_bias.shape[0]
    t = ATTN_T
    tm = ROW_TILE
    tf = FF_TILE
    assert e == dv == LANES and d_conv == d_attn and w_in.shape[-1] == 5 * d_attn
    assert s % tm == 0 and s % t == 0 and n_meta <= t and page_table.shape[1] % DEC_PAGES == 0
    assert s % FF_ROWS == 0 and (b * s) % IN_ROWS == 0 and s % KV_ROWS == 0
    me = n_buckets // 2
    sat = math.ceil(me * (MAX_DISTANCE / me) ** ((n_buckets - 1 - me) / (n_buckets - me))) + 1
    assert t + 1 >= sat and page + 1 >= sat
    halo = SUBLANES * ((cw_len - 1 + SUBLANES - 1) // SUBLANES)
    assert n_meta <= halo and tm % halo == 0 and tm % CONV_ROWS == 0 and fw_len - 1 <= SUBLANES

    ffp = tf * ((d_ff + tf - 1) // tf)
    pad_c = lambda a: jnp.pad(a, ((0, 0), (0, ffp - d_ff)))
    cwa, cwg = pad_c(ffn_conv_w[0, :, :d_ff]), pad_c(ffn_conv_w[0, :, d_ff:])
    cba, cbg = pad_c(ffn_conv_b[:, :d_ff])[0], pad_c(ffn_conv_b[:, d_ff:])[0]
    lamv = jnp.stack([lambda_q1[0], lambda_k1[0], lambda_q2[0], lambda_k2[0]]).astype(F32)
    q_scale = dk ** -0.5 * LOG2E

    ar = lambda n: jnp.arange(n, dtype=jnp.int32)
    by_rel = _bias_tile(rel_bias, ar(2 * t)[None, :])[:, 0, :]
    d0, d1 = toeplitz_tiles(jnp.concatenate([jnp.full((n_heads, t - 1), NEG, F32), by_rel[:, :t]], axis=1),
                            by_rel[:, 1:], t)
    dm = _bias_tile(rel_bias, n_meta + ar(t)[:, None] - ar(n_meta)[None, :])
    dmeta = _bias_tile(rel_bias, ar(n_meta)[:, None] - ar(n_meta)[None, :])
    stack2 = lambda a: jnp.concatenate([a, a], axis=0)
    same_head = (ar(n_heads)[:, None, None] == ar(n_heads)[None, None, :])
    blast = _bias_tile(rel_bias, (page - ar(page))[None, :])[:, 0, :]
    mb = stack2(jnp.broadcast_to(jnp.where(same_head, 0.0, NEG), (n_heads, page, n_heads))
                ).reshape(2 * n_heads, page * n_heads)
    mbl = stack2(jnp.where(same_head, blast[:, :, None], NEG)).reshape(2 * n_heads, page * n_heads)
    b0 = stack2(_bias_tile(rel_bias, jnp.zeros((1, 1), jnp.int32))[:, 0, :])

    x_small = jnp.concatenate([x_sample[:, 0, :], meta_tokens.astype(F32)], axis=0)
    xn_small = rmsnorm_cast(x_small, norm1_g[0])
    z, w_in_b = mm_cast(xn_small, w_in[0], W_COLS)
    q_small = (z[:, :d_attn] * q_scale).astype(BF16)
    k_small = z[:, d_attn:2 * d_attn]
    v_small = z[:, 2 * d_attn:3 * d_attn]
    state_t = jnp.transpose(state_conv[0], (1, 0, 2))
    u_small, c_small = small_conv_module(z[:, 3 * d_attn:4 * d_attn], z[:, 4 * d_attn:], state_t,
                                         conv_w[0], conv_b[0], conv_ln_g[0], conv_ln_b[0], bd)
    o_meta = meta_attention(q_small[bd:], k_small[bd:], v_small[bd:], dmeta, lamv, subln_g[0], n_heads)
    hm_delta, w_o_b = mm_cast(jnp.concatenate([o_meta, c_small[bd:]], axis=1), w_o[0], W_COLS)
    _, xn2_meta = add_rmsnorm(x_small[bd:], hm_delta, norm2_g[0])
    up_meta, wab_b = up_cast(xn2_meta, ffn_w_up[0], d_ff, ffp, tf)
    wd_b = cast_pad(ffn_w_down[0], ffp, WD_COLS)

    xp = x_prompt.reshape(b * s, d)
    qkv_b, kvu = prompt_inproj(xp, norm1_g[0], w_in_b, d_attn, q_scale, IN_ROWS)
    qkv_b = qkv_b.reshape(3, b, s, d_attn)
    kvu = kvu.reshape(3, b, s, d_attn)
    o_p = prompt_attention(qkv_b, k_small[bd:].astype(BF16), v_small[bd:].astype(BF16),
                           d0, d1, dm, lamv, subln_g[0], n_heads, t, ATTN_HEADS)
    head_u = jnp.concatenate([jnp.zeros((halo - n_meta, d_conv), F32), u_small[bd:]], axis=0)
    h1_p = conv_merge(kvu, head_u, conv_w[0], conv_b[0], conv_ln_g[0], conv_ln_b[0], o_p, x_prompt, w_o_b,
                      tm).reshape(b * s, d)
    n_t = fw_len - 1
    head_rows = lambda a: jnp.concatenate([jnp.zeros((SUBLANES - n_t, ffp), F32), a[-n_t:]], axis=0)
    hd = lambda a: a.reshape(bd, n_heads, -1)
    dec = (page_table, hd(q_small[:bd]), hd(k_small[:bd]), hd(v_small[:bd]), cache_k[0], cache_v[0],
           mb, mbl, b0, lamv, subln_g[0])
    y_p, tail_a, tail_g, o_host = prompt_ffn(h1_p, norm2_g[0], final_norm_g, wab_b, wd_b, cwa, cwg, cba, cbg,
                                             head_rows(up_meta[0]), head_rows(up_meta[1]), s, FF_ROWS, tf,
                                             dec, FF_DEC_PAGES)

    n_host = o_host.shape[0]
    o_dec = o_host
    if n_host < bd:
        o_rest = decode_attention(*(a[n_host:] for a in dec[:4]), *dec[4:], DEC_PAGES)
        o_dec = jnp.concatenate([o_host, o_rest], axis=0)
    oc_s = jnp.concatenate([o_dec.reshape(bd, d_attn), c_small[:bd]], axis=1)
    hs_delta = mm_bf16(oc_s, w_o_b[None], 0, W_COLS)
    h1_s, xn2_s = add_rmsnorm(x_small[:bd], hs_delta, norm2_g[0])
    up_a_s = mm_bf16(xn2_s, wab_b, 0, W_COLS)
    up_g_s = mm_bf16(xn2_s, wab_b, 1, W_COLS)
    st = jnp.transpose(state_ffn_conv[0], (1, 0, 2))
    st_pad = lambda a: pad_c(a.reshape(-1, d_ff)).reshape(fw_len - 1, bd, ffp)
    hid = small_gate(up_a_s, up_g_s, st_pad(st[:, :, :d_ff]), st_pad(st[:, :, d_ff:]), cwa, cwg, cba, cbg)
    down = mm_bf16(hid, wd_b[None], 0, WD_COLS)
    y_sample = residual_norm(h1_s, down, final_norm_g)

    k_prompt = kv_rows_with_meta(kvu, 0, k_small[bd:], n_heads, KV_ROWS)[None]
    v_prompt = kv_rows_with_meta(kvu, 1, v_small[bd:], n_heads, KV_ROWS)[None]
    conv_prompt = kvu[2:3, :, s - (cw_len - 1):, :]
    bps = s // FF_ROWS
    last_blk = lambda a: a[bps - 1::bps, SUBLANES - n_t:, :d_ff]
    ffn_prompt = jnp.concatenate([last_blk(tail_a), last_blk(tail_g)], axis=-1)[None]
    k_sample = k_small[:bd].reshape(1, bd, 1, n_heads, e)
    v_sample = v_small[:bd].reshape(1, bd, 1, n_heads, dv)
    conv_sample = jnp.concatenate([state_conv[0][:, 1:], u_small[:bd, None, :]], axis=1)[None]
    up_s = jnp.concatenate([up_a_s[:, :d_ff], up_g_s[:, :d_ff]], axis=-1)
    ffn_sample = jnp.concatenate([state_ffn_conv[0][:, 1:], up_s[:, None, :]], axis=1)[None]
    return (y_p.reshape(b, s, d), y_sample.reshape(bd, 1, d), k_prompt, v_prompt, conv_prompt, ffn_prompt,
            k_sample, v_sample, conv_sample, ffn_sample)
```
